```python
import jax, jax.numpy as jnp
from jax import lax
import numpy as np

D_MODEL = 1024
BATCH = 16
SEQ = 2048
DEPTH = 2

HEAD_DIM = 64
RET_HEADS = 8
NSA_HEADS = 8
NSA_KV_GROUPS = 2
NSA_GROUP_SIZE = NSA_HEADS // NSA_KV_GROUPS
RET_W = RET_HEADS * HEAD_DIM
NSA_W = NSA_HEADS * HEAD_DIM
NSA_KV_W = NSA_KV_GROUPS * HEAD_DIM
N_BRANCHES = 3
EVEN_IN_W = 4 * RET_W + NSA_W + 2 * N_BRANCHES * NSA_KV_W + NSA_HEADS * N_BRANCHES
RET_CHUNK = 128
CMP_STRIDE = 16
CMP_BLOCK = 2 * CMP_STRIDE
CMP_HIDDEN = 128
SEL_BLOCK = 64
TOP_N = 8
WINDOW = 256
NSA_QBLOCK = 64
CONV_WIDTH = 3
N_GROUPS = 4
EXPERTS_PER_GROUP = 8
N_EXPERTS = N_GROUPS * EXPERTS_PER_GROUP
TOP_K_IN_GROUP = 2
D_EXPERT = 256
MOE_ROW_BLOCK = 256
N_EVEN = (DEPTH + 1) // 2
N_ODD = DEPTH // 2
RMS_EPS = 1e-6
NEG_INF = -1e30
F32 = jnp.float32

kernel_name = "hybrid_retention_nsa_shortconv_hmoe"


def rms_norm(x, g):
    xf = x.astype(F32)
    y = xf * lax.rsqrt(jnp.mean(xf * xf, axis=-1, keepdims=True) + RMS_EPS)
    return (y * g.astype(F32)).astype(x.dtype)


def masked_softmax(scores, mask):
    s = jnp.where(mask, scores.astype(F32), NEG_INF)
    m = jnp.max(s, axis=-1, keepdims=True)
    p = jnp.where(mask, jnp.exp(s - m), 0.0)
    return p / jnp.maximum(jnp.sum(p, axis=-1, keepdims=True), 1e-30)


def alibi_slopes(n_heads):
    return 2.0 ** (-8.0 * jnp.arange(1, n_heads + 1, dtype=F32) / n_heads)


def retention(q, k, v):
    b, h, s, d = q.shape
    nc = s // RET_CHUNK
    log_g = jnp.log(1.0 - 2.0 ** (-5.0 - jnp.arange(h, dtype=F32)))
    pos = jnp.arange(RET_CHUNK, dtype=F32)
    diff = pos[:, None] - pos[None, :]
    decay = jnp.where(diff >= 0, jnp.exp(log_g[:, None, None] * jnp.maximum(diff, 0.0)), 0.0)
    q_c = q.reshape(b, h, nc, RET_CHUNK, d)
    k_c = k.reshape(b, h, nc, RET_CHUNK, d) * (d ** -0.5)
    v_c = v.reshape(b, h, nc, RET_CHUNK, d)
    scores = jnp.einsum('bhcid,bhcjd->bhcij', q_c, k_c) * decay[None, :, None]
    y_intra = jnp.einsum('bhcij,bhcje->bhcie', scores, v_c)
    k_decay = jnp.exp(log_g[:, None] * (RET_CHUNK - 1 - pos))[None, :, None, :, None]
    q_decay = jnp.exp(log_g[:, None] * (pos + 1.0))[None, :, None, :, None]
    chunk_kv = jnp.einsum('bhcjd,bhcje->cbhde', k_c * k_decay, v_c)
    chunk_decay = jnp.exp(log_g * RET_CHUNK)[None, :, None, None]

    def step(state, kv):
        return chunk_decay * state + kv, state

    _, prev_states = lax.scan(step, jnp.zeros((b, h, d, d), chunk_kv.dtype), chunk_kv)
    y_inter = jnp.einsum('bhcid,cbhde->bhcie', q_c * q_decay, prev_states)
    return (y_intra + y_inter).reshape(b, h, s, d)


def compress_blocks(x, pe, w1, w2):
    b, g, s, d = x.shape
    chunks = x.reshape(b, g, s // CMP_STRIDE, CMP_STRIDE, d)
    blocks = jnp.concatenate([chunks[:, :, :-1], chunks[:, :, 1:]], axis=3) + pe
    flat = blocks.reshape(b, g, blocks.shape[2], CMP_BLOCK * d)
    return jax.nn.gelu(flat @ w1) @ w2


def native_sparse_attention(q, kc, vc, ks, vs, kw, vw, gates, cmp_pe, w_cmp1, w_cmp2,
                            q_norm_g, k_norm_g):
    b, s = q.shape[:2]
    G, R, d = NSA_KV_GROUPS, NSA_GROUP_SIZE, HEAD_DIM
    QB = NSA_QBLOCK
    scale = d ** -0.5
    slopes = alibi_slopes(NSA_HEADS).reshape(G, R)[None, :, :, None, None]
    q = rms_norm(q, q_norm_g).transpose(0, 2, 1, 3).reshape(b, G, R, s, d)
    kc, vc, ks, vs, kw, vw = [t.transpose(0, 2, 1, 3) for t in (kc, vc, ks, vs, kw, vw)]
    t_pos = jnp.arange(s)

    k_cmp = rms_norm(compress_blocks(kc, cmp_pe[0], w_cmp1[0], w_cmp2[0]), k_norm_g[0])
    v_cmp = compress_blocks(vc, cmp_pe[1], w_cmp1[1], w_cmp2[1])
    n_cmp = k_cmp.shape[2]
    cmp_start = jnp.arange(n_cmp) * CMP_STRIDE
    dist_cmp = t_pos[:, None] - (cmp_start + CMP_BLOCK - 1)[None, :]
    sc = jnp.einsum('bgrqd,bgnd->bgrqn', q, k_cmp) * scale - slopes * dist_cmp.astype(F32)
    p_cmp = masked_softmax(sc, dist_cmp >= 0)
    o_cmp = jnp.einsum('bgrqn,bgnd->bgrqd', p_cmp.astype(v_cmp.dtype), v_cmp)

    n_sel = s // SEL_BLOCK
    top_n = min(TOP_N, n_sel)
    sel_ids = jnp.arange(n_sel)
    sel_start = sel_ids * SEL_BLOCK
    overlap = ((cmp_start[:, None] < sel_start[None, :] + SEL_BLOCK)
               & (cmp_start[:, None] + CMP_BLOCK > sel_start[None, :])).astype(F32)
    imp = jnp.einsum('bgrqn,nj->bgqj', p_cmp, overlap)
    q_blk = t_pos // SEL_BLOCK
    valid = sel_start[None, :] <= t_pos[:, None]
    forced = ((sel_ids[None, :] == 0) | (sel_ids[None, :] == q_blk[:, None])
              | (sel_ids[None, :] == q_blk[:, None] - 1))
    imp = jnp.where(forced, imp + 1e3, jnp.where(valid, imp, -1e3))
    _, sel_idx = lax.top_k(imp, top_n)

    ks_blocks = rms_norm(ks, k_norm_g[1]).reshape(b, G, n_sel, SEL_BLOCK, d)
    vs_blocks = vs.reshape(b, G, n_sel, SEL_BLOCK, d)
    pad = ((0, 0), (0, 0), (WINDOW, 0), (0, 0))
    kw_pad = jnp.pad(rms_norm(kw, k_norm_g[2]), pad)
    vw_pad = jnp.pad(vw, pad)
    nq = s // QB
    q_blocks = q.reshape(b, G, R, nq, QB, d).transpose(3, 0, 1, 2, 4, 5)
    idx_blocks = sel_idx.reshape(b, G, nq, QB, top_n).transpose(2, 0, 1, 3, 4)
    b_ar = jnp.arange(b)[:, None, None, None]
    g_ar = jnp.arange(G)[None, :, None, None]
    sel_off = jnp.arange(SEL_BLOCK)
    win_off = jnp.arange(WINDOW + QB)

    def sparse_block(args):
        qb, idx, i = args
        t = i * QB + jnp.arange(QB)
        k_sel = ks_blocks[b_ar, g_ar, idx]
        v_sel = vs_blocks[b_ar, g_ar, idx]
        s_pos = idx[..., None] * SEL_BLOCK + sel_off
        d_sel = (t[:, None, None] - s_pos)[:, :, None]
        sc_s = (jnp.einsum('bgrqd,bgqnkd->bgrqnk', qb, k_sel) * scale
                - slopes[..., None] * d_sel.astype(F32))
        m_sel = top_n * SEL_BLOCK
        p_s = masked_softmax(sc_s.reshape(b, G, R, QB, m_sel), (d_sel >= 0).reshape(b, G, 1, QB, m_sel))
        o_s = jnp.einsum('bgrqm,bgqmd->bgrqd', p_s.astype(v_sel.dtype), v_sel.reshape(b, G, QB, m_sel, d))
        k_win = lax.dynamic_slice_in_dim(kw_pad, i * QB, WINDOW + QB, axis=2)
        v_win = lax.dynamic_slice_in_dim(vw_pad, i * QB, WINDOW + QB, axis=2)
        w_pos = i * QB - WINDOW + win_off
        d_win = t[:, None] - w_pos[None, :]
        mask = (d_win >= 0) & (d_win < WINDOW) & (w_pos[None, :] >= 0)
        sc_w = jnp.einsum('bgrqd,bgkd->bgrqk', qb, k_win) * scale - slopes * d_win.astype(F32)
        p_w = masked_softmax(sc_w, mask)
        o_w = jnp.einsum('bgrqk,bgkd->bgrqd', p_w.astype(v_win.dtype), v_win)
        return o_s, o_w

    o_sel, o_win = lax.map(sparse_block, (q_blocks, idx_blocks, jnp.arange(nq)))
    o_sel = o_sel.transpose(1, 2, 3, 0, 4, 5).reshape(b, G, R, s, d)
    o_win = o_win.transpose(1, 2, 3, 0, 4, 5).reshape(b, G, R, s, d)

    gt = jax.nn.sigmoid(gates).transpose(0, 2, 1, 3).reshape(b, G, R, s, N_BRANCHES)
    o = gt[..., 0:1] * o_cmp + gt[..., 1:2] * o_sel + gt[..., 2:3] * o_win
    return o.transpose(0, 3, 1, 2, 4).reshape(b, s, NSA_W)


def even_mixer(xn, w_in, cmp_pe, w_cmp1, w_cmp2, q_norm_g, k_norm_g, ret_norm_g, w_out):
    b, s, _ = xn.shape
    sizes = [RET_W] * 4 + [NSA_W] + [NSA_KV_W] * (2 * N_BRANCHES) + [NSA_HEADS * N_BRANCHES]
    cuts = [int(c) for c in np.cumsum(sizes)[:-1]]
    rq, rk, rv, rg, nq_, kc, vc, ks, vs, kw, vw, ng = jnp.split(xn @ w_in, cuts, axis=-1)
    heads = lambda t, h: t.reshape(b, s, h, HEAD_DIM)
    y_ret = retention(*[heads(t, RET_HEADS).transpose(0, 2, 1, 3) for t in (rq, rk, rv)])
    y_ret = rms_norm(y_ret, ret_norm_g).transpose(0, 2, 1, 3).reshape(b, s, RET_W) * jax.nn.silu(rg)
    y_nsa = native_sparse_attention(
        heads(nq_, NSA_HEADS), *[heads(t, NSA_KV_GROUPS) for t in (kc, vc, ks, vs, kw, vw)],
        ng.reshape(b, s, NSA_HEADS, N_BRANCHES), cmp_pe, w_cmp1, w_cmp2, q_norm_g, k_norm_g)
    return jnp.concatenate([y_ret, y_nsa], axis=-1) @ w_out


def odd_mixer(xn, w_in, conv_w, conv_b, w_out):
    gate_b, gate_c, h = jnp.split(xn @ w_in, 3, axis=-1)
    u = gate_c * h
    y = lax.conv_general_dilated(u, conv_w[:, None, :], window_strides=(1,),
                                 padding=[(CONV_WIDTH - 1, 0)],
                                 dimension_numbers=('NWC', 'WIO', 'NWC'),
                                 feature_group_count=u.shape[-1]) + conv_b
    return (gate_b * y) @ w_out


def hierarchical_moe(xn, wg_group, bg_group, wg_expert, bg_expert, w_gate, w_up, w_down):
    b, s, d = xn.shape
    n_tok = b * s
    x_tok = xn.reshape(n_tok, d)
    xf = x_tok.astype(F32)
    grp_prob = jax.nn.softmax(xf @ wg_group.astype(F32) + bg_group.astype(F32), axis=-1)
    grp_p, grp_idx = lax.top_k(grp_prob, 1)
    e_logits = (xf @ wg_expert.astype(F32) + bg_expert.astype(F32)).reshape(n_tok, N_GROUPS, EXPERTS_PER_GROUP)
    in_grp = jnp.take_along_axis(e_logits, grp_idx[:, :, None], axis=1)[:, 0]
    top_p, top_local = lax.top_k(jax.nn.softmax(in_grp, axis=-1), TOP_K_IN_GROUP)
    gate = grp_p * top_p / jnp.sum(top_p, axis=-1, keepdims=True)
    expert_id = grp_idx * EXPERTS_PER_GROUP + top_local

    n_slot = n_tok * TOP_K_IN_GROUP
    flat_e = expert_id.reshape(-1)
    order = jnp.argsort(flat_e)
    e_sorted = flat_e[order]
    tok_sorted = (jnp.arange(n_slot) // TOP_K_IN_GROUP)[order]
    gate_sorted = gate.reshape(-1)[order]
    counts = jax.ops.segment_sum(jnp.ones_like(flat_e), flat_e, num_segments=N_EXPERTS)
    padded = (counts + MOE_ROW_BLOCK - 1) // MOE_ROW_BLOCK * MOE_ROW_BLOCK
    start = jnp.cumsum(counts) - counts
    pend = jnp.cumsum(padded)
    dest = (pend - padded)[e_sorted] + (jnp.arange(n_slot) - start[e_sorted])
    n_blocks = -(-n_slot // MOE_ROW_BLOCK) + N_EXPERTS
    buf = jnp.zeros((n_blocks * MOE_ROW_BLOCK, d), xn.dtype).at[dest].set(x_tok[tok_sorted])
    block_expert = jnp.minimum(
        jnp.searchsorted(pend, jnp.arange(n_blocks) * MOE_ROW_BLOCK, side='right'), N_EXPERTS - 1)

    def expert_block(args):
        xb, e = args
        return (jax.nn.silu(xb @ w_gate[e]) * (xb @ w_up[e])) @ w_down[e]

    y_buf = lax.map(expert_block, (buf.reshape(n_blocks, MOE_ROW_BLOCK, d), block_expert)).reshape(-1, d)
    y = jnp.zeros_like(x_tok).at[tok_sorted].add(y_buf[dest] * gate_sorted[:, None].astype(y_buf.dtype))
    return y.reshape(b, s, d)


def setup_inputs(seed: int = 0) -> dict:
    key = jax.random.key(seed)
    k = jax.random.split(key, 24)
    nrm = lambda kk, shape, sc: jax.random.normal(kk, shape, F32) * sc
    d = HEAD_DIM
    return {
        "x": nrm(k[0], (BATCH, SEQ, D_MODEL), 1.0),
        "mix_norm_g": 1.0 + nrm(k[1], (DEPTH, D_MODEL), 0.02),
        "ffn_norm_g": 1.0 + nrm(k[2], (DEPTH, D_MODEL), 0.02),
        "ev_w_in": nrm(k[3], (N_EVEN, D_MODEL, EVEN_IN_W), D_MODEL ** -0.5),
        "ev_cmp_pe": nrm(k[4], (N_EVEN, 2, CMP_BLOCK, d), 0.1),
        "ev_w_cmp1": nrm(k[5], (N_EVEN, 2, CMP_BLOCK * d, CMP_HIDDEN), (CMP_BLOCK * d) ** -0.5),
        "ev_w_cmp2": nrm(k[6], (N_EVEN, 2, CMP_HIDDEN, d), CMP_HIDDEN ** -0.5),
        "ev_q_norm_g": 1.0 + nrm(k[7], (N_EVEN, d), 0.02),
        "ev_k_norm_g": 1.0 + nrm(k[8], (N_EVEN, N_BRANCHES, d), 0.02),
        "ev_ret_norm_g": 1.0 + nrm(k[9], (N_EVEN, d), 0.02),
        "ev_w_out": nrm(k[10], (N_EVEN, RET_W + NSA_W, D_MODEL), (RET_W + NSA_W) ** -0.5),
        "od_w_in": nrm(k[11], (N_ODD, D_MODEL, 3 * D_MODEL), D_MODEL ** -0.5),
        "od_conv_w": nrm(k[12], (N_ODD, CONV_WIDTH, D_MODEL), CONV_WIDTH ** -0.5),
        "od_conv_b": nrm(k[13], (N_ODD, D_MODEL), 0.01),
        "od_w_out": nrm(k[14], (N_ODD, D_MODEL, D_MODEL), D_MODEL ** -0.5),
        "moe_wg_group": nrm(k[15], (DEPTH, D_MODEL, N_GROUPS), D_MODEL ** -0.5),
        "moe_bg_group": nrm(k[16], (DEPTH, N_GROUPS), 0.01),
        "moe_wg_expert": nrm(k[17], (DEPTH, D_MODEL, N_EXPERTS), D_MODEL ** -0.5),
        "moe_bg_expert": nrm(k[18], (DEPTH, N_EXPERTS), 0.01),
        "moe_w_gate": nrm(k[19], (DEPTH, N_EXPERTS, D_MODEL, D_EXPERT), D_MODEL ** -0.5),
        "moe_w_up": nrm(k[20], (DEPTH, N_EXPERTS, D_MODEL, D_EXPERT), D_MODEL ** -0.5),
        "moe_w_down": nrm(k[21], (DEPTH, N_EXPERTS, D_EXPERT, D_MODEL), D_EXPERT ** -0.5),
    }


def reference(x, mix_norm_g, ffn_norm_g, ev_w_in, ev_cmp_pe, ev_w_cmp1, ev_w_cmp2,
              ev_q_norm_g, ev_k_norm_g, ev_ret_norm_g, ev_w_out, od_w_in, od_conv_w,
              od_conv_b, od_w_out, moe_wg_group, moe_bg_group, moe_wg_expert,
              moe_bg_expert, moe_w_gate, moe_w_up, moe_w_down):
    h = x
    for layer in range(DEPTH):
        i = layer // 2
        xn = rms_norm(h, mix_norm_g[layer])
        if layer % 2 == 0:
            h = h + even_mixer(xn, ev_w_in[i], ev_cmp_pe[i], ev_w_cmp1[i], ev_w_cmp2[i],
                               ev_q_norm_g[i], ev_k_norm_g[i], ev_ret_norm_g[i], ev_w_out[i])
        else:
            h = h + odd_mixer(xn, od_w_in[i], od_conv_w[i], od_conv_b[i], od_w_out[i])
        xn = rms_norm(h, ffn_norm_g[layer])
        h = h + hierarchical_moe(xn, moe_wg_group[layer], moe_bg_group[layer],
                                 moe_wg_expert[layer], moe_bg_expert[layer],
                                 moe_w_gate[layer], moe_w_up[layer], moe_w_down[layer])
    return h
```

```python
import functools

import jax
import jax.numpy as jnp
import numpy as np
from jax import lax
from jax.experimental import pallas as pl
from jax.experimental.pallas import tpu as pltpu

F32 = jnp.float32
BF16 = jnp.bfloat16
I32 = jnp.int32
HIGHEST = lax.Precision.HIGHEST

D_MODEL = 1024
HEAD_DIM = 64
RET_HEADS = 8
NSA_HEADS = 8
NSA_KV_GROUPS = 2
NSA_GROUP_SIZE = NSA_HEADS // NSA_KV_GROUPS
RET_W = RET_HEADS * HEAD_DIM
NSA_W = NSA_HEADS * HEAD_DIM
NSA_KV_W = NSA_KV_GROUPS * HEAD_DIM
N_BRANCHES = 3
RET_CHUNK = 128
CMP_STRIDE = 16
CMP_BLOCK = 2 * CMP_STRIDE
CMP_HIDDEN = 128
SEL_BLOCK = 64
TOP_N = 8
WINDOW = 256
N_GROUPS = 4
EXPERTS_PER_GROUP = 8
N_EXPERTS = N_GROUPS * EXPERTS_PER_GROUP
TOP_K_IN_GROUP = 2
D_EXPERT = 256
RMS_EPS = 1e-6
NEG_INF = -1e30
ATTN_SCALE = HEAD_DIM ** -0.5

LANES = 128
VMEM_LIMIT = 48 * 1024 * 1024
ROW_TILE = 512
NSA_Q_TILE = 128
NSA_KV_CHUNK = 512
MOE_BLOCK = 256
ROUTER_ROWS = 40


def _params(*sem):
    return pltpu.CompilerParams(dimension_semantics=sem, vmem_limit_bytes=VMEM_LIMIT)


def _rms(x, g):
    return x * lax.rsqrt(jnp.mean(x * x, axis=-1, keepdims=True) + RMS_EPS) * g


def _group_rms(x, bd, gain):
    ms = jnp.dot(x * x, bd, precision=HIGHEST, preferred_element_type=F32)
    return x * lax.rsqrt(ms + RMS_EPS) * gain


def _sigmoid(x):
    return 1.0 / (1.0 + jnp.exp(-x))


def _block_diag_mean(n):
    idx = np.arange(n) // HEAD_DIM
    return jnp.asarray((idx[:, None] == idx[None, :]).astype(np.float32) / HEAD_DIM)


def _even_inproj_kernel(x_ref, g_ref, wret_ref, wnq_ref, wkv_ref, wng_ref, qgain_ref, kgain_ref,
                        bd512_ref, bd128_ref, ret_ref, nq_ref, kv_ref, ng_ref):
    xn = _rms(x_ref[...], g_ref[...]).astype(BF16)
    ret_ref[...] = jnp.dot(xn, wret_ref[...], preferred_element_type=F32).astype(BF16)
    nq = jnp.dot(xn, wnq_ref[...], preferred_element_type=F32)
    nq_ref[...] = _group_rms(nq, bd512_ref[...], qgain_ref[...]).astype(BF16)
    kv = jnp.dot(xn, wkv_ref[...], preferred_element_type=F32)
    w = NSA_KV_W
    kv_ref[:, 0:2 * w] = kv[:, 0:2 * w].astype(BF16)
    kv_ref[:, 2 * w:3 * w] = _group_rms(kv[:, 2 * w:3 * w], bd128_ref[...], kgain_ref[0:1, :]).astype(BF16)
    kv_ref[:, 3 * w:4 * w] = kv[:, 3 * w:4 * w].astype(BF16)
    kv_ref[:, 4 * w:5 * w] = _group_rms(kv[:, 4 * w:5 * w], bd128_ref[...], kgain_ref[1:2, :]).astype(BF16)
    kv_ref[:, 5 * w:6 * w] = kv[:, 5 * w:6 * w].astype(BF16)
    ng_ref[...] = jnp.dot(xn, wng_ref[...], preferred_element_type=F32)


def _even_inproj(h, g, w_in, q_norm_g, k_norm_g):
    t = h.shape[0]
    tm = ROW_TILE
    c_ret, c_nq, c_kv = 4 * RET_W, 4 * RET_W + NSA_W, 4 * RET_W + NSA_W + 6 * NSA_KV_W
    n_gate = NSA_HEADS * N_BRANCHES
    wb = w_in.astype(BF16)
    wret, wnq, wkv = wb[:, :c_ret], wb[:, c_ret:c_nq], wb[:, c_nq:c_kv]
    wng = jnp.pad(wb[:, c_kv:], ((0, 0), (0, LANES - n_gate)))
    qgain = jnp.tile(q_norm_g, NSA_HEADS)[None, :]
    kgain = jnp.stack([jnp.tile(k_norm_g[1], NSA_KV_GROUPS), jnp.tile(k_norm_g[2], NSA_KV_GROUPS)])
    full = lambda a: pl.BlockSpec(a.shape, lambda i: (0,) * a.ndim)
    row = lambda n: pl.BlockSpec((tm, n), lambda i: (i, 0))
    bd512, bd128 = _block_diag_mean(NSA_W), _block_diag_mean(NSA_KV_W)
    gg = g[None, :]
    return pl.pallas_call(
        _even_inproj_kernel,
        grid=(t // tm,),
        in_specs=[row(D_MODEL), full(gg), full(wret), full(wnq), full(wkv), full(wng), full(qgain),
                  full(kgain), full(bd512), full(bd128)],
        out_specs=[row(c_ret), row(NSA_W), row(6 * NSA_KV_W), row(LANES)],
        out_shape=[jax.ShapeDtypeStruct((t, c_ret), BF16), jax.ShapeDtypeStruct((t, NSA_W), BF16),
                   jax.ShapeDtypeStruct((t, 6 * NSA_KV_W), BF16), jax.ShapeDtypeStruct((t, LANES), F32)],
        compiler_params=_params("arbitrary"),
    )(h, gg, wret, wnq, wkv, wng, qgain, kgain, bd512, bd128)


def _retention_kernel(q_ref, k_ref, v_ref, rg_ref, decay_ref, qd_ref, kd_ref, cd_ref, gain_ref, o_ref):
    n_chunks = q_ref.shape[0] // RET_CHUNK
    gain = gain_ref[...]
    nt = (((1,), (1,)), ((), ()))
    tn = (((0,), (0,)), ((), ()))

    def body(c, states):
        r0 = pl.multiple_of(c * RET_CHUNK, RET_CHUNK)
        rows = pl.ds(r0, RET_CHUNK)
        q2, k2, v2 = q_ref[rows, :], k_ref[rows, :], v_ref[rows, :]
        rg2 = rg_ref[rows, :].astype(F32)
        outs, new_states = [], []
        for hh in range(2):
            lanes = slice(hh * HEAD_DIM, (hh + 1) * HEAD_DIM)
            q, v = q2[:, lanes], v2[:, lanes]
            kf = k2[:, lanes].astype(F32) * ATTN_SCALE
            state = states[hh]
            s = lax.dot_general(q, kf.astype(BF16), nt, preferred_element_type=F32) * decay_ref[hh]
            y = jnp.dot(s.astype(BF16), v, preferred_element_type=F32)
            qd = (q.astype(F32) * qd_ref[hh]).astype(BF16)
            y = y + jnp.dot(qd, state.astype(BF16), preferred_element_type=F32)
            kd = (kf * kd_ref[hh]).astype(BF16)
            new_states.append(cd_ref[hh] * state + lax.dot_general(kd, v, tn, preferred_element_type=F32))
            rg = rg2[:, lanes]
            outs.append(_rms(y, gain) * (rg * _sigmoid(rg)))
        o_ref[rows, :] = jnp.concatenate(outs, axis=-1).astype(BF16)
        return tuple(new_states)

    zero = jnp.zeros((HEAD_DIM, HEAD_DIM), F32)
    lax.fori_loop(0, n_chunks, body, (zero, zero))


def _retention_tables():
    h = np.arange(RET_HEADS, dtype=np.float64)
    log_g = np.log(1.0 - 2.0 ** (-5.0 - h))
    pos = np.arange(RET_CHUNK, dtype=np.float64)
    diff = pos[:, None] - pos[None, :]
    decay = np.where(diff >= 0, np.exp(log_g[:, None, None] * np.maximum(diff, 0.0)), 0.0)
    qd = np.exp(log_g[:, None] * (pos + 1.0))[:, :, None] * np.ones((1, 1, HEAD_DIM))
    kd = np.exp(log_g[:, None] * (RET_CHUNK - 1 - pos))[:, :, None] * np.ones((1, 1, HEAD_DIM))
    cd = np.exp(log_g * RET_CHUNK)[:, None, None] * np.ones((1, HEAD_DIM, HEAD_DIM))
    return [jnp.asarray(a, F32) for a in (decay, qd, kd, cd)]


def _retention(ret, ret_norm_g, b, s):
    t = b * s
    n_pairs = RET_HEADS // 2
    decay, qd, kd, cd = _retention_tables()
    col = lambda off: pl.BlockSpec((s, LANES), lambda bi, p: (bi, off * n_pairs + p))
    tab = lambda a: pl.BlockSpec((2,) + a.shape[1:], lambda bi, p: (p, 0, 0))
    gain = ret_norm_g[None, :]
    return pl.pallas_call(
        _retention_kernel,
        grid=(b, n_pairs),
        in_specs=[col(0), col(1), col(2), col(3), tab(decay), tab(qd), tab(kd), tab(cd),
                  pl.BlockSpec(gain.shape, lambda bi, p: (0, 0))],
        out_specs=pl.BlockSpec((s, LANES), lambda bi, p: (bi, p)),
        out_shape=jax.ShapeDtypeStruct((t, RET_W), BF16),
        compiler_params=_params("arbitrary", "arbitrary"),
    )(ret, ret, ret, ret, decay, qd, kd, cd, gain)


def _compress_kernel(x_ref, pe_ref, w1_ref, w2_ref, kgain_ref, o_ref):
    is_key = pl.program_id(0) == 0
    half = CMP_STRIDE * HEAD_DIM
    x = x_ref[...]
    n = x.shape[0]
    a = jnp.dot(x, w1_ref[0:half, :], preferred_element_type=F32)
    bm = jnp.dot(x, w1_ref[half:2 * half, :], preferred_element_type=F32)
    pew = jnp.dot(pe_ref[...], w1_ref[...], preferred_element_type=F32)[0:1, :]
    pre = a + pltpu.roll(bm, n - 1, axis=0) + pew
    hid = 0.5 * pre * (1.0 + jnp.tanh(np.sqrt(2.0 / np.pi) * (pre + 0.044715 * pre * pre * pre)))
    out = jnp.dot(hid.astype(BF16), w2_ref[...], preferred_element_type=F32)
    normed = _rms(out, kgain_ref[...])
    o_ref[...] = jnp.where(is_key, normed, out).astype(BF16)


def _compress(kc, vc, cmp_pe, w_cmp1, w_cmp2, k_norm_g, b, s):
    n = s // CMP_STRIDE
    g = NSA_KV_GROUPS
    flat = CMP_STRIDE * HEAD_DIM

    def chunks(a):
        return a.reshape(b, n, CMP_STRIDE, g, HEAD_DIM).transpose(0, 3, 1, 2, 4).reshape(b, g, n, flat)

    x = jnp.stack([chunks(kc), chunks(vc)])
    pe = jnp.broadcast_to(cmp_pe.reshape(2, 1, 2 * flat), (2, 8, 2 * flat)).astype(BF16)
    w1, w2 = w_cmp1.astype(BF16), w_cmp2.astype(BF16)
    kgain = k_norm_g[0][None, :]
    return pl.pallas_call(
        _compress_kernel,
        grid=(2, b, g),
        in_specs=[pl.BlockSpec((None, None, None, n, flat), lambda kv, bi, gi: (kv, bi, gi, 0, 0)),
                  pl.BlockSpec((None, 8, 2 * flat), lambda kv, bi, gi: (kv, 0, 0)),
                  pl.BlockSpec((None, 2 * flat, CMP_HIDDEN), lambda kv, bi, gi: (kv, 0, 0)),
                  pl.BlockSpec((None, CMP_HIDDEN, HEAD_DIM), lambda kv, bi, gi: (kv, 0, 0)),
                  pl.BlockSpec(kgain.shape, lambda kv, bi, gi: (0, 0))],
        out_specs=pl.BlockSpec((None, None, None, n, HEAD_DIM), lambda kv, bi, gi: (kv, bi, gi, 0, 0)),
        out_shape=jax.ShapeDtypeStruct((2, b, g, n, HEAD_DIM), BF16),
        compiler_params=_params("arbitrary", "arbitrary", "arbitrary"),
    )(x, pe, w1, w2, kgain)


def _nsa_kernel(slopes_ref, q_ref, cmp_ref, ks_ref, vs_ref, kw_ref, vw_ref, gate_ref, ovl_ref,
                expand_ref, o_ref, *, n_sel, n_cmp):
    r_heads = NSA_GROUP_SIZE
    tq = q_ref.shape[0]
    gi = pl.program_id(1)
    t0 = pl.program_id(2) * tq
    nt = (((1,), (1,)), ((), ()))
    slopes = [slopes_ref[gi * r_heads + r] for r in range(r_heads)]

    q = q_ref[...]
    q4 = jnp.concatenate([q[:, r * HEAD_DIM:(r + 1) * HEAD_DIM] for r in range(r_heads)], axis=0)
    t_col = t0 + lax.broadcasted_iota(I32, (tq, 1), 0)

    def scores(k, dist, mask):
        s = lax.dot_general(q4, k, nt, preferred_element_type=F32)
        return [jnp.where(mask, s[r * tq:(r + 1) * tq] * ATTN_SCALE - slopes[r] * dist, NEG_INF)
                for r in range(r_heads)]

    def softmax_rows(s, mask):
        m = jnp.max(s, axis=-1, keepdims=True)
        p = jnp.where(mask, jnp.exp(s - m), 0.0)
        return p, jnp.sum(p, axis=-1, keepdims=True)

    def attend(k, v, dist, mask, normalise_first):
        ps, ls = zip(*[softmax_rows(s, mask) for s in scores(k, dist, mask)])
        inv = [1.0 / jnp.maximum(l, 1e-30) for l in ls]
        if normalise_first:
            ps = [p * i for p, i in zip(ps, inv)]
        p4 = jnp.concatenate(ps, axis=0).astype(BF16)
        o4 = jnp.dot(p4, v, preferred_element_type=F32)
        outs = [o4[r * tq:(r + 1) * tq] for r in range(r_heads)]
        if not normalise_first:
            outs = [o * i for o, i in zip(outs, inv)]
        return outs, p4

    n_pad = cmp_ref.shape[1]
    c_idx = lax.broadcasted_iota(I32, (tq, n_pad), 1)
    dist_c = t_col - (c_idx * CMP_STRIDE + (CMP_BLOCK - 1))
    mask_c = (dist_c >= 0) & (c_idx < n_cmp)
    o_cmp, p4 = attend(cmp_ref[0], cmp_ref[1], dist_c.astype(F32), mask_c, True)

    imp4 = jnp.dot(p4, ovl_ref[...], preferred_element_type=F32)
    imp = imp4[0:tq]
    for r in range(1, r_heads):
        imp = imp + imp4[r * tq:(r + 1) * tq]
    j_idx = lax.broadcasted_iota(I32, (tq, LANES), 1)
    q_blk = t_col // SEL_BLOCK
    valid = (j_idx * SEL_BLOCK <= t_col) & (j_idx < n_sel)
    forced = (j_idx == 0) | (j_idx == q_blk) | (j_idx == q_blk - 1)
    val = jnp.where(j_idx < n_sel, jnp.where(forced, imp + 1e3, jnp.where(valid, imp, -1e3)), -2e3)
    rank = jnp.zeros((tq, LANES), F32)
    for k in range(n_sel):
        vk = val[:, k:k + 1]
        ahead = jnp.where(vk > val, 1.0, jnp.where(vk == val, jnp.where(j_idx > k, 1.0, 0.0), 0.0))
        rank = rank + ahead
    sel = jnp.where(valid, jnp.where(rank < TOP_N, 1.0, 0.0), 0.0).astype(BF16)

    ck = NSA_KV_CHUNK

    def sel_body(c, carry):
        c0 = pl.multiple_of(c * ck, ck)
        k = ks_ref[pl.ds(c0, ck), :]
        v = vs_ref[pl.ds(c0, ck), :]
        chosen = jnp.dot(sel, expand_ref[:, pl.ds(c0, ck)], preferred_element_type=F32)
        dist = t_col - (c0 + lax.broadcasted_iota(I32, (tq, ck), 1))
        mask = jnp.where(dist >= 0, chosen, 0.0) > 0.5
        ss = scores(k, dist.astype(F32), mask)
        ps, new = [], []
        for r in range(r_heads):
            m_old, l_old, acc_old = carry[r]
            m_new = jnp.maximum(m_old, jnp.max(ss[r], axis=-1, keepdims=True))
            p = jnp.where(mask, jnp.exp(ss[r] - m_new), 0.0)
            alpha = jnp.exp(m_old - m_new)
            ps.append(p)
            new.append((m_new, alpha * l_old + jnp.sum(p, axis=-1, keepdims=True), alpha * acc_old))
        pv = jnp.dot(jnp.concatenate(ps, axis=0).astype(BF16), v, preferred_element_type=F32)
        return tuple((m, l, acc + pv[r * tq:(r + 1) * tq]) for r, (m, l, acc) in enumerate(new))

    init = tuple((jnp.full((tq, 1), NEG_INF, F32), jnp.zeros((tq, 1), F32), jnp.zeros((tq, HEAD_DIM), F32))
                 for _ in range(r_heads))
    fin = lax.fori_loop(0, (t0 + tq + ck - 1) // ck, sel_body, init)
    o_sel = [acc / jnp.maximum(l, 1e-30) for (_, l, acc) in fin]

    n_win = WINDOW + tq
    w0 = pl.multiple_of(jnp.maximum(t0 - WINDOW, 0), tq)
    dist_w = t_col - (w0 + lax.broadcasted_iota(I32, (tq, n_win), 1))
    mask_w = (dist_w >= 0) & (dist_w < WINDOW)
    o_win, _ = attend(kw_ref[pl.ds(w0, n_win), :], vw_ref[pl.ds(w0, n_win), :], dist_w.astype(F32), mask_w, False)

    gate = _sigmoid(gate_ref[...])
    outs = []
    for r in range(r_heads):
        c = N_BRANCHES * r
        outs.append(gate[:, c:c + 1] * o_cmp[r] + gate[:, c + 1:c + 2] * o_sel[r] + gate[:, c + 2:c + 3] * o_win[r])
    o_ref[...] = jnp.concatenate(outs, axis=-1).astype(BF16)


def _nsa_attention(nq, cmp_kv, ks, vs, kw, vw, gates, b, s):
    t = b * s
    g, r = NSA_KV_GROUPS, NSA_GROUP_SIZE
    tq = NSA_Q_TILE
    n_q = s // tq
    n_sel = s // SEL_BLOCK
    n_pad = s // CMP_STRIDE
    n_cmp = n_pad - 1
    assert n_sel <= LANES and s % NSA_KV_CHUNK == 0 and s >= WINDOW + tq
    slopes = jnp.asarray(2.0 ** (-8.0 * np.arange(1, NSA_HEADS + 1) / NSA_HEADS), F32)
    cmp_start = np.arange(n_pad) * CMP_STRIDE
    sel_start = np.arange(LANES) * SEL_BLOCK
    ovl = ((cmp_start[:, None] < sel_start[None, :] + SEL_BLOCK) & (cmp_start[:, None] + CMP_BLOCK > sel_start[None, :])
           & (np.arange(n_pad)[:, None] < n_cmp) & (np.arange(LANES)[None, :] < n_sel))
    ovl = jnp.asarray(ovl.astype(np.float32), BF16)
    expand = jnp.asarray((np.arange(LANES)[:, None] == (np.arange(s) // SEL_BLOCK)[None, :]).astype(np.float32), BF16)
    seq = lambda: pl.BlockSpec((None, None, s, HEAD_DIM), lambda bi, gi, qi: (bi, gi, 0, 0))
    n_gate = r * N_BRANCHES
    return pl.pallas_call(
        functools.partial(_nsa_kernel, n_sel=n_sel, n_cmp=n_cmp),
        grid=(b, g, n_q),
        in_specs=[pl.BlockSpec(memory_space=pltpu.SMEM),
                  pl.BlockSpec((tq, r * HEAD_DIM), lambda bi, gi, qi: (bi * n_q + qi, gi)),
                  pl.BlockSpec((2, None, None, n_pad, HEAD_DIM), lambda bi, gi, qi: (0, bi, gi, 0, 0)),
                  seq(), seq(), seq(), seq(),
                  pl.BlockSpec((None, None, tq, n_gate), lambda bi, gi, qi: (bi, gi, qi, 0)),
                  pl.BlockSpec(ovl.shape, lambda bi, gi, qi: (0, 0)),
                  pl.BlockSpec(expand.shape, lambda bi, gi, qi: (0, 0))],
        out_specs=pl.BlockSpec((tq, r * HEAD_DIM), lambda bi, gi, qi: (bi * n_q + qi, gi)),
        out_shape=jax.ShapeDtypeStruct((t, NSA_W), BF16),
        compiler_params=_params("arbitrary", "arbitrary", "arbitrary"),
    )(slopes, nq, cmp_kv, ks, vs, kw, vw, gates, ovl, expand)


def _even_outproj_kernel(h_ref, a_ref, b_ref, wa_ref, wb_ref, o_ref):
    y = jnp.dot(a_ref[...], wa_ref[...], preferred_element_type=F32)
    y = y + jnp.dot(b_ref[...], wb_ref[...], preferred_element_type=F32)
    o_ref[...] = h_ref[...] + y


def _even_outproj(h, y_ret, y_nsa, w_out):
    t = h.shape[0]
    tm = ROW_TILE
    wb = w_out.astype(BF16)
    wa, wbt = wb[:RET_W], wb[RET_W:]
    row = lambda n: pl.BlockSpec((tm, n), lambda i: (i, 0))
    full = lambda a: pl.BlockSpec(a.shape, lambda i: (0, 0))
    return pl.pallas_call(
        _even_outproj_kernel,
        grid=(t // tm,),
        in_specs=[row(D_MODEL), row(RET_W), row(NSA_W), full(wa), full(wbt)],
        out_specs=row(D_MODEL),
        out_shape=jax.ShapeDtypeStruct((t, D_MODEL), F32),
        compiler_params=_params("arbitrary"),
    )(h, y_ret, y_nsa, wa, wbt)


def _odd_mixer_kernel(h_ref, g_ref, win_ref, cw_ref, cb_ref, wout_ref, o_ref, tail_ref):
    @pl.when(pl.program_id(1) == 0)
    def _():
        tail_ref[...] = jnp.zeros_like(tail_ref)

    h = h_ref[...]
    ts = h.shape[0]
    xn = _rms(h, g_ref[...]).astype(BF16)
    proj = jnp.dot(xn, win_ref[...], preferred_element_type=F32)
    gate_b, gate_c, hid = proj[:, 0:D_MODEL], proj[:, D_MODEL:2 * D_MODEL], proj[:, 2 * D_MODEL:3 * D_MODEL]
    u = gate_c * hid
    row = lax.broadcasted_iota(I32, (ts, D_MODEL), 0)
    tail = tail_ref[...]
    prev1, prev2 = tail[7:8, :], tail[6:7, :]
    u1 = jnp.where(row >= 1, pltpu.roll(u, 1, axis=0), prev1)
    u2 = jnp.where(row >= 2, pltpu.roll(u, 2, axis=0), jnp.where(row == 1, prev1, prev2))
    cw = cw_ref[...]
    y = cw[0:1, :] * u2 + cw[1:2, :] * u1 + cw[2:3, :] * u + cb_ref[...]
    tail_ref[...] = u[ts - 8:ts, :]
    z = (gate_b * y).astype(BF16)
    o_ref[...] = h + jnp.dot(z, wout_ref[...], preferred_element_type=F32)


def _odd_mixer(h, g, w_in, conv_w, conv_b, w_out, b, s):
    t = b * s
    ts = ROW_TILE
    n_s = s // ts
    win, wout = w_in.astype(BF16), w_out.astype(BF16)
    gg, cb = g[None, :], conv_b[None, :]
    cw = jnp.pad(conv_w, ((0, 8 - conv_w.shape[0]), (0, 0)))
    full = lambda a: pl.BlockSpec(a.shape, lambda bi, si: (0, 0))
    row = pl.BlockSpec((ts, D_MODEL), lambda bi, si: (bi * n_s + si, 0))
    return pl.pallas_call(
        _odd_mixer_kernel,
        grid=(b, n_s),
        in_specs=[row, full(gg), full(win), full(cw), full(cb), full(wout)],
        out_specs=row,
        out_shape=jax.ShapeDtypeStruct((t, D_MODEL), F32),
        scratch_shapes=[pltpu.VMEM((8, D_MODEL), F32)],
        compiler_params=_params("arbitrary", "arbitrary"),
    )(h, gg, win, cw, cb, wout)


def _router_kernel(h_ref, g_ref, wr_ref, br_ref, xn_ref, ids_ref, gates_ref):
    xn = _rms(h_ref[...], g_ref[...])
    xn_ref[...] = xn
    logits = lax.dot_general(wr_ref[...], xn, (((1,), (1,)), ((), ())), precision=HIGHEST,
                             preferred_element_type=F32) + br_ref[...]
    row = lambda i: logits[i:i + 1, :]

    def softmax(xs):
        m = functools.reduce(jnp.maximum, xs)
        es = [jnp.exp(x - m) for x in xs]
        tot = functools.reduce(lambda a, c: a + c, es)
        return [e / tot for e in es]

    def argmax(ps):
        best_p, best_i = ps[0], jnp.zeros_like(ps[0], dtype=I32)
        for i in range(1, len(ps)):
            upd = ps[i] > best_p
            best_p = jnp.where(upd, ps[i], best_p)
            best_i = jnp.where(upd, i, best_i)
        return best_p, best_i

    grp_p, grp_i = argmax(softmax([row(i) for i in range(N_GROUPS)]))
    in_grp = []
    for e in range(EXPERTS_PER_GROUP):
        x = row(N_GROUPS + e)
        for gidx in range(1, N_GROUPS):
            x = jnp.where(grp_i == gidx, row(N_GROUPS + gidx * EXPERTS_PER_GROUP + e), x)
        in_grp.append(x)
    pe = softmax(in_grp)
    p1, i1 = argmax(pe)
    p2, i2 = argmax([jnp.where(i1 == e, -1.0, pe[e]) for e in range(EXPERTS_PER_GROUP)])
    tot = p1 + p2
    ids_ref[...] = jnp.concatenate([grp_i * EXPERTS_PER_GROUP + i1, grp_i * EXPERTS_PER_GROUP + i2], axis=0)
    gates_ref[...] = jnp.concatenate([grp_p * p1 / tot, grp_p * p2 / tot], axis=0)


def _router(h, g, wg_group, bg_group, wg_expert, bg_expert):
    t = h.shape[0]
    tm = ROW_TILE
    n_logit = N_GROUPS + N_EXPERTS
    wr = jnp.pad(jnp.concatenate([wg_group, wg_expert], axis=1).T, ((0, ROUTER_ROWS - n_logit), (0, 0)))
    br = jnp.pad(jnp.concatenate([bg_group, bg_expert]), (0, ROUTER_ROWS - n_logit))
    br = jnp.broadcast_to(br[:, None], (ROUTER_ROWS, tm))
    gg = g[None, :]
    full = lambda a: pl.BlockSpec(a.shape, lambda i: (0, 0))
    return pl.pallas_call(
        _router_kernel,
        grid=(t // tm,),
        in_specs=[pl.BlockSpec((tm, D_MODEL), lambda i: (i, 0)), full(gg), full(wr), full(br)],
        out_specs=[pl.BlockSpec((tm, D_MODEL), lambda i: (i, 0)),
                   pl.BlockSpec((TOP_K_IN_GROUP, tm), lambda i: (0, i)),
                   pl.BlockSpec((TOP_K_IN_GROUP, tm), lambda i: (0, i))],
        out_shape=[jax.ShapeDtypeStruct((t, D_MODEL), F32), jax.ShapeDtypeStruct((TOP_K_IN_GROUP, t), I32),
                   jax.ShapeDtypeStruct((TOP_K_IN_GROUP, t), F32)],
        compiler_params=_params("arbitrary"),
    )(h, gg, wr, br)


def _expert_kernel(blk_expert_ref, src_ref, dst_ref, x_hbm, wg_ref, wu_ref, wd_ref, y_hbm, xbuf, ybuf, sem_in, sem_out):
    del blk_expert_ref
    rows = xbuf.shape[0]

    def gather(r, _):
        pltpu.make_async_copy(x_hbm.at[pl.ds(src_ref[0, r], 1)], xbuf.at[pl.ds(r, 1)], sem_in).start()
        return 0

    def gather_wait(r, _):
        pltpu.make_async_copy(x_hbm.at[pl.ds(0, 1)], xbuf.at[pl.ds(r, 1)], sem_in).wait()
        return 0

    def scatter(r, _):
        pltpu.make_async_copy(ybuf.at[pl.ds(r, 1)], y_hbm.at[pl.ds(dst_ref[0, r], 1)], sem_out).start()
        return 0

    def scatter_wait(r, _):
        pltpu.make_async_copy(ybuf.at[pl.ds(r, 1)], y_hbm.at[pl.ds(0, 1)], sem_out).wait()
        return 0

    lax.fori_loop(0, rows, gather, 0)
    lax.fori_loop(0, rows, gather_wait, 0)
    x = xbuf[...].astype(BF16)
    a = jnp.dot(x, wg_ref[...], preferred_element_type=F32)
    u = jnp.dot(x, wu_ref[...], preferred_element_type=F32)
    hid = (a * _sigmoid(a) * u).astype(BF16)
    ybuf[...] = jnp.dot(hid, wd_ref[...], preferred_element_type=F32)
    lax.fori_loop(0, rows, scatter, 0)
    lax.fori_loop(0, rows, scatter_wait, 0)


def _moe_plan(ids, t):
    n_slot = t * TOP_K_IN_GROUP
    n_blocks = n_slot // MOE_BLOCK + N_EXPERTS
    flat_e = ids.T.reshape(-1)
    order = jnp.argsort(flat_e).astype(I32)
    counts = jnp.sum(flat_e[:, None] == jnp.arange(N_EXPERTS, dtype=I32)[None, :], axis=0, dtype=I32)
    padded = (counts + MOE_BLOCK - 1) // MOE_BLOCK * MOE_BLOCK
    start = jnp.cumsum(counts) - counts
    pend = jnp.cumsum(padded)
    blk_expert = jnp.minimum(jnp.searchsorted(pend, jnp.arange(n_blocks, dtype=I32) * MOE_BLOCK, side='right'),
                             N_EXPERTS - 1).astype(I32)
    pos = jnp.arange(n_blocks * MOE_BLOCK, dtype=I32)
    e_pos = jnp.repeat(blk_expert, MOE_BLOCK)
    off = pos - (pend - padded)[e_pos]
    is_slot = (off >= 0) & (off < counts[e_pos])
    slot = order[jnp.clip(start[e_pos] + off, 0, n_slot - 1)]
    src = jnp.where(is_slot, slot // TOP_K_IN_GROUP, 0).astype(I32)
    dst = jnp.where(is_slot, slot, n_slot + pos % MOE_BLOCK).astype(I32)
    return blk_expert, src.reshape(n_blocks, 1, MOE_BLOCK), dst.reshape(n_blocks, 1, MOE_BLOCK)


def _experts(xn, ids, w_gate, w_up, w_down):
    t = xn.shape[0]
    blk_expert, src, dst = _moe_plan(ids, t)
    n_blocks = blk_expert.shape[0]
    wg, wu, wd = w_gate.astype(BF16), w_up.astype(BF16), w_down.astype(BF16)
    idx_spec = pl.BlockSpec((None, 1, MOE_BLOCK), lambda i, be: (i, 0, 0), memory_space=pltpu.SMEM)
    w_spec = lambda a: pl.BlockSpec((None,) + a.shape[1:], lambda i, be: (be[i], 0, 0))
    return pl.pallas_call(
        _expert_kernel,
        grid_spec=pltpu.PrefetchScalarGridSpec(
            num_scalar_prefetch=1,
            grid=(n_blocks,),
            in_specs=[idx_spec, idx_spec, pl.BlockSpec(memory_space=pl.ANY), w_spec(wg), w_spec(wu), w_spec(wd)],
            out_specs=pl.BlockSpec(memory_space=pl.ANY),
            scratch_shapes=[pltpu.VMEM((MOE_BLOCK, D_MODEL), F32), pltpu.VMEM((MOE_BLOCK, D_MODEL), F32),
                            pltpu.SemaphoreType.DMA(()), pltpu.SemaphoreType.DMA(())]),
        out_shape=jax.ShapeDtypeStruct((t * TOP_K_IN_GROUP + MOE_BLOCK, D_MODEL), F32),
        compiler_params=_params("arbitrary"),
    )(blk_expert, src, dst, xn, wg, wu, wd)


def _combine_kernel(h_ref, y_ref, gate_ref, o_ref):
    gate = gate_ref[...]
    o_ref[...] = h_ref[...] + gate[:, 0:1] * y_ref[:, 0:D_MODEL] + gate[:, 1:2] * y_ref[:, D_MODEL:2 * D_MODEL]


def _combine(h, y_slots, gates):
    t = h.shape[0]
    tm = ROW_TILE
    y2 = y_slots.reshape(-1, TOP_K_IN_GROUP * D_MODEL)
    return pl.pallas_call(
        _combine_kernel,
        grid=(t // tm,),
        in_specs=[pl.BlockSpec((tm, D_MODEL), lambda i: (i, 0)),
                  pl.BlockSpec((tm, TOP_K_IN_GROUP * D_MODEL), lambda i: (i, 0)),
                  pl.BlockSpec((tm, TOP_K_IN_GROUP), lambda i: (i, 0))],
        out_specs=pl.BlockSpec((tm, D_MODEL), lambda i: (i, 0)),
        out_shape=jax.ShapeDtypeStruct((t, D_MODEL), F32),
        compiler_params=_params("arbitrary"),
    )(h, y2, gates.T)


def _moe(h, g, wg_group, bg_group, wg_expert, bg_expert, w_gate, w_up, w_down):
    xn, ids, gates = _router(h, g, wg_group, bg_group, wg_expert, bg_expert)
    y_slots = _experts(xn, ids, w_gate, w_up, w_down)
    return _combine(h, y_slots, gates)


def _even_mixer(h, g, w_in, cmp_pe, w_cmp1, w_cmp2, q_norm_g, k_norm_g, ret_norm_g, w_out, b, s):
    ret, nq, kv, ng = _even_inproj(h, g, w_in, q_norm_g, k_norm_g)
    y_ret = _retention(ret, ret_norm_g, b, s)
    w = NSA_KV_W
    kc, vc, ks, vs, kw, vw = [kv[:, i * w:(i + 1) * w] for i in range(6)]
    cmp_kv = _compress(kc, vc, cmp_pe, w_cmp1, w_cmp2, k_norm_g, b, s)
    by_group = lambda a: a.reshape(b, s, NSA_KV_GROUPS, HEAD_DIM).transpose(0, 2, 1, 3)
    n_gate = NSA_GROUP_SIZE * N_BRANCHES
    gates = ng[:, :NSA_HEADS * N_BRANCHES].reshape(b, s, NSA_KV_GROUPS, n_gate).transpose(0, 2, 1, 3)
    y_nsa = _nsa_attention(nq, cmp_kv, by_group(ks), by_group(vs), by_group(kw), by_group(vw), gates, b, s)
    return _even_outproj(h, y_ret, y_nsa, w_out)


def kernel(x, mix_norm_g, ffn_norm_g, ev_w_in, ev_cmp_pe, ev_w_cmp1, ev_w_cmp2, ev_q_norm_g, ev_k_norm_g, ev_ret_norm_g, ev_w_out, od_w_in, od_conv_w, od_conv_b, od_w_out, moe_wg_group, moe_bg_group, moe_wg_expert, moe_bg_expert, moe_w_gate, moe_w_up, moe_w_down):
    b, s, d = x.shape
    h = x.reshape(b * s, d)
    for layer in range(mix_norm_g.shape[0]):
        i = layer // 2
        if layer % 2 == 0:
            h = _even_mixer(h, mix_norm_g[layer], ev_w_in[i], ev_cmp_pe[i], ev_w_cmp1[i], ev_w_cmp2[i],
                            ev_q_norm_g[i], ev_k_norm_g[i], ev_ret_norm_g[i], ev_w_out[i], b, s)
        else:
            h = _odd_mixer(h, mix_norm_g[layer], od_w_in[i], od_conv_w[i], od_conv_b[i], od_w_out[i], b, s)
        h = _moe(h, ffn_norm_g[layer], moe_wg_group[layer], moe_bg_group[layer], moe_wg_expert[layer],
                 moe_bg_expert[layer], moe_w_gate[layer], moe_w_up[layer], moe_w_down[layer])
    return h.reshape(b, s, d)
```

```python
import functools

import jax
import jax.numpy as jnp
import numpy as np
from jax import lax
from jax.experimental import pallas as pl
from jax.experimental.pallas import tpu as pltpu

F32 = jnp.float32
BF16 = jnp.bfloat16
I32 = jnp.int32
HIGHEST = lax.Precision.HIGHEST

D_MODEL = 1024
HEAD_DIM = 64
RET_HEADS = 8
NSA_HEADS = 8
NSA_KV_GROUPS = 2
NSA_GROUP_SIZE = NSA_HEADS // NSA_KV_GROUPS
RET_W = RET_HEADS * HEAD_DIM
NSA_W = NSA_HEADS * HEAD_DIM
NSA_KV_W = NSA_KV_GROUPS * HEAD_DIM
N_BRANCHES = 3
RET_CHUNK = 128
CMP_STRIDE = 16
CMP_BLOCK = 2 * CMP_STRIDE
CMP_HIDDEN = 128
SEL_BLOCK = 64
TOP_N = 8
WINDOW = 256
N_GROUPS = 4
EXPERTS_PER_GROUP = 8
N_EXPERTS = N_GROUPS * EXPERTS_PER_GROUP
TOP_K_IN_GROUP = 2
D_EXPERT = 256
RMS_EPS = 1e-6
NEG_INF = -1e30
ATTN_SCALE = HEAD_DIM ** -0.5

LANES = 128
VMEM_LIMIT = 48 * 1024 * 1024
ROW_TILE = 512
NSA_Q_TILE = 128
NSA_KV_CHUNK = 512
MOE_BLOCK = 256
ROUTER_ROWS = 40
SLAB = D_MODEL // LANES


def _params(*sem):
    return pltpu.CompilerParams(dimension_semantics=sem, vmem_limit_bytes=VMEM_LIMIT)


def _rms(x, g):
    return x * lax.rsqrt(jnp.mean(x * x, axis=-1, keepdims=True) + RMS_EPS) * g


def _group_rms(x, bd, gain):
    ms = jnp.dot(x * x, bd, precision=HIGHEST, preferred_element_type=F32)
    return x * lax.rsqrt(ms + RMS_EPS) * gain


def _sigmoid(x):
    return 1.0 / (1.0 + jnp.exp(-x))


def _block_diag_mean(n):
    idx = np.arange(n) // HEAD_DIM
    return jnp.asarray((idx[:, None] == idx[None, :]).astype(np.float32) / HEAD_DIM)


def _even_inproj_kernel(x_ref, g_ref, wret_ref, wnq_ref, wkv_ref, wng_ref, qgain_ref, kgain_ref,
                        bd512_ref, bd128_ref, ret_ref, nq_ref, kv_ref, ng_ref):
    xn = _rms(x_ref[...], g_ref[...]).astype(BF16)
    ret_ref[...] = jnp.dot(xn, wret_ref[...], preferred_element_type=F32).astype(BF16)
    nq = jnp.dot(xn, wnq_ref[...], preferred_element_type=F32)
    nq_ref[...] = _group_rms(nq, bd512_ref[...], qgain_ref[...]).astype(BF16)
    kv = jnp.dot(xn, wkv_ref[...], preferred_element_type=F32)
    w = NSA_KV_W
    kv_ref[:, 0:2 * w] = kv[:, 0:2 * w].astype(BF16)
    kv_ref[:, 2 * w:3 * w] = _group_rms(kv[:, 2 * w:3 * w], bd128_ref[...], kgain_ref[0:1, :]).astype(BF16)
    kv_ref[:, 3 * w:4 * w] = kv[:, 3 * w:4 * w].astype(BF16)
    kv_ref[:, 4 * w:5 * w] = _group_rms(kv[:, 4 * w:5 * w], bd128_ref[...], kgain_ref[1:2, :]).astype(BF16)
    kv_ref[:, 5 * w:6 * w] = kv[:, 5 * w:6 * w].astype(BF16)
    ng_ref[...] = jnp.dot(xn, wng_ref[...], preferred_element_type=F32)


def _even_inproj(h, g, w_in, q_norm_g, k_norm_g):
    t = h.shape[0]
    tm = ROW_TILE
    c_ret, c_nq, c_kv = 4 * RET_W, 4 * RET_W + NSA_W, 4 * RET_W + NSA_W + 6 * NSA_KV_W
    n_gate = NSA_HEADS * N_BRANCHES
    wb = w_in.astype(BF16)
    wret, wnq, wkv = wb[:, :c_ret], wb[:, c_ret:c_nq], wb[:, c_nq:c_kv]
    wng = jnp.pad(wb[:, c_kv:], ((0, 0), (0, LANES - n_gate)))
    qgain = jnp.tile(q_norm_g, NSA_HEADS)[None, :]
    kgain = jnp.stack([jnp.tile(k_norm_g[1], NSA_KV_GROUPS), jnp.tile(k_norm_g[2], NSA_KV_GROUPS)])
    full = lambda a: pl.BlockSpec(a.shape, lambda i: (0,) * a.ndim)
    row = lambda n: pl.BlockSpec((tm, n), lambda i: (i, 0))
    bd512, bd128 = _block_diag_mean(NSA_W), _block_diag_mean(NSA_KV_W)
    gg = g[None, :]
    return pl.pallas_call(
        _even_inproj_kernel,
        name="even_inproj",
        grid=(t // tm,),
        in_specs=[row(D_MODEL), full(gg), full(wret), full(wnq), full(wkv), full(wng), full(qgain),
                  full(kgain), full(bd512), full(bd128)],
        out_specs=[row(c_ret), row(NSA_W), row(6 * NSA_KV_W), row(LANES)],
        out_shape=[jax.ShapeDtypeStruct((t, c_ret), BF16), jax.ShapeDtypeStruct((t, NSA_W), BF16),
                   jax.ShapeDtypeStruct((t, 6 * NSA_KV_W), BF16), jax.ShapeDtypeStruct((t, LANES), F32)],
        compiler_params=_params("arbitrary"),
    )(h, gg, wret, wnq, wkv, wng, qgain, kgain, bd512, bd128)


def _retention_kernel(q_ref, k_ref, v_ref, rg_ref, decay_ref, qd_ref, kd_ref, cd_ref, gain_ref, o_ref):
    n_chunks = q_ref.shape[0] // RET_CHUNK
    gain = gain_ref[...]
    nt = (((1,), (1,)), ((), ()))
    tn = (((0,), (0,)), ((), ()))

    def body(c, states):
        r0 = pl.multiple_of(c * RET_CHUNK, RET_CHUNK)
        rows = pl.ds(r0, RET_CHUNK)
        q2, k2, v2 = q_ref[rows, :], k_ref[rows, :], v_ref[rows, :]
        rg2 = rg_ref[rows, :].astype(F32)
        outs, new_states = [], []
        for hh in range(2):
            lanes = slice(hh * HEAD_DIM, (hh + 1) * HEAD_DIM)
            q, v = q2[:, lanes], v2[:, lanes]
            kf = k2[:, lanes].astype(F32) * ATTN_SCALE
            state = states[hh]
            s = lax.dot_general(q, kf.astype(BF16), nt, preferred_element_type=F32) * decay_ref[hh]
            y = jnp.dot(s.astype(BF16), v, preferred_element_type=F32)
            qd = (q.astype(F32) * qd_ref[hh]).astype(BF16)
            y = y + jnp.dot(qd, state.astype(BF16), preferred_element_type=F32)
            kd = (kf * kd_ref[hh]).astype(BF16)
            new_states.append(cd_ref[hh] * state + lax.dot_general(kd, v, tn, preferred_element_type=F32))
            rg = rg2[:, lanes]
            outs.append(_rms(y, gain) * (rg * _sigmoid(rg)))
        o_ref[rows, :] = jnp.concatenate(outs, axis=-1).astype(BF16)
        return tuple(new_states)

    zero = jnp.zeros((HEAD_DIM, HEAD_DIM), F32)
    lax.fori_loop(0, n_chunks, body, (zero, zero))


def _retention_tables():
    h = np.arange(RET_HEADS, dtype=np.float64)
    log_g = np.log(1.0 - 2.0 ** (-5.0 - h))
    pos = np.arange(RET_CHUNK, dtype=np.float64)
    diff = pos[:, None] - pos[None, :]
    decay = np.where(diff >= 0, np.exp(log_g[:, None, None] * np.maximum(diff, 0.0)), 0.0)
    qd = np.exp(log_g[:, None] * (pos + 1.0))[:, :, None] * np.ones((1, 1, HEAD_DIM))
    kd = np.exp(log_g[:, None] * (RET_CHUNK - 1 - pos))[:, :, None] * np.ones((1, 1, HEAD_DIM))
    cd = np.exp(log_g * RET_CHUNK)[:, None, None] * np.ones((1, HEAD_DIM, HEAD_DIM))
    return [jnp.asarray(a, F32) for a in (decay, qd, kd, cd)]


def _retention(ret, ret_norm_g, b, s):
    t = b * s
    n_pairs = RET_HEADS // 2
    decay, qd, kd, cd = _retention_tables()
    col = lambda off: pl.BlockSpec((s, LANES), lambda bi, p: (bi, off * n_pairs + p))
    tab = lambda a: pl.BlockSpec((2,) + a.shape[1:], lambda bi, p: (p, 0, 0))
    gain = ret_norm_g[None, :]
    return pl.pallas_call(
        _retention_kernel,
        name="retention",
        grid=(b, n_pairs),
        in_specs=[col(0), col(1), col(2), col(3), tab(decay), tab(qd), tab(kd), tab(cd),
                  pl.BlockSpec(gain.shape, lambda bi, p: (0, 0))],
        out_specs=pl.BlockSpec((s, LANES), lambda bi, p: (bi, p)),
        out_shape=jax.ShapeDtypeStruct((t, RET_W), BF16),
        compiler_params=_params("arbitrary", "arbitrary"),
    )(ret, ret, ret, ret, decay, qd, kd, cd, gain)


def _compress_kernel(x_ref, pe_ref, w1_ref, w2_ref, kgain_ref, o_ref):
    is_key = pl.program_id(0) == 0
    half = CMP_STRIDE * HEAD_DIM
    x = x_ref[...]
    n = x.shape[0]
    a = jnp.dot(x, w1_ref[0:half, :], preferred_element_type=F32)
    bm = jnp.dot(x, w1_ref[half:2 * half, :], preferred_element_type=F32)
    pew = jnp.dot(pe_ref[...], w1_ref[...], preferred_element_type=F32)[0:1, :]
    pre = a + pltpu.roll(bm, n - 1, axis=0) + pew
    hid = 0.5 * pre * (1.0 + jnp.tanh(np.sqrt(2.0 / np.pi) * (pre + 0.044715 * pre * pre * pre)))
    out = jnp.dot(hid.astype(BF16), w2_ref[...], preferred_element_type=F32)
    normed = _rms(out, kgain_ref[...])
    o_ref[...] = jnp.where(is_key, normed, out).astype(BF16)


def _compress(kc, vc, cmp_pe, w_cmp1, w_cmp2, k_norm_g, b, s):
    n = s // CMP_STRIDE
    g = NSA_KV_GROUPS
    flat = CMP_STRIDE * HEAD_DIM

    def chunks(a):
        return a.reshape(b, n, CMP_STRIDE, g, HEAD_DIM).transpose(0, 3, 1, 2, 4).reshape(b, g, n, flat)

    x = jnp.stack([chunks(kc), chunks(vc)])
    pe = jnp.broadcast_to(cmp_pe.reshape(2, 1, 2 * flat), (2, 8, 2 * flat)).astype(BF16)
    w1, w2 = w_cmp1.astype(BF16), w_cmp2.astype(BF16)
    kgain = k_norm_g[0][None, :]
    return pl.pallas_call(
        _compress_kernel,
        name="nsa_compress",
        grid=(2, b, g),
        in_specs=[pl.BlockSpec((None, None, None, n, flat), lambda kv, bi, gi: (kv, bi, gi, 0, 0)),
                  pl.BlockSpec((None, 8, 2 * flat), lambda kv, bi, gi: (kv, 0, 0)),
                  pl.BlockSpec((None, 2 * flat, CMP_HIDDEN), lambda kv, bi, gi: (kv, 0, 0)),
                  pl.BlockSpec((None, CMP_HIDDEN, HEAD_DIM), lambda kv, bi, gi: (kv, 0, 0)),
                  pl.BlockSpec(kgain.shape, lambda kv, bi, gi: (0, 0))],
        out_specs=pl.BlockSpec((None, None, None, n, HEAD_DIM), lambda kv, bi, gi: (kv, bi, gi, 0, 0)),
        out_shape=jax.ShapeDtypeStruct((2, b, g, n, HEAD_DIM), BF16),
        compiler_params=_params("arbitrary", "arbitrary", "arbitrary"),
    )(x, pe, w1, w2, kgain)


def _nsa_kernel(slopes_ref, q_ref, kcmp_ref, vcmp_ref, ks_ref, vs_ref, kw_ref, vw_ref, gate_ref, ovl_ref, place_ref,
                o_ref, *, n_sel, n_cmp):
    r_heads = NSA_GROUP_SIZE
    tq = q_ref.shape[0]
    gi = pl.program_id(1)
    t0 = pl.program_id(2) * tq
    nt = (((1,), (1,)), ((), ()))
    tn = (((0,), (0,)), ((), ()))

    q = q_ref[...]
    lane = lax.broadcasted_iota(I32, (tq, HEAD_DIM), 1)
    qs, feats = [], []
    for r in range(r_heads):
        slope = slopes_ref[gi * r_heads + r]
        qs.append((q[:, r * HEAD_DIM:(r + 1) * HEAD_DIM].astype(F32) * ATTN_SCALE).astype(BF16))
        feats.append(jnp.where(lane == 0, slope * SEL_BLOCK, jnp.where(lane == 1, slope, 0.0)))

    def stack_q(extra):
        return jnp.concatenate([jnp.concatenate([qs[r], (feats[r] + extra).astype(BF16)], axis=-1)
                                for r in range(r_heads)], axis=0)

    q4 = stack_q(0.0)
    t_col = t0 + lax.broadcasted_iota(I32, (tq, 1), 0)

    def head_rows(x):
        return [x[r * tq:(r + 1) * tq] for r in range(r_heads)]

    n_pad = kcmp_ref.shape[0]
    c_idx = lax.broadcasted_iota(I32, (tq, n_pad), 1)
    mask_c = (t_col >= c_idx * CMP_STRIDE + (CMP_BLOCK - 1)) & (c_idx < n_cmp)
    s4 = lax.dot_general(q4, kcmp_ref[...], nt, preferred_element_type=F32)
    ps = []
    for s in head_rows(s4):
        s = jnp.where(mask_c, s, NEG_INF)
        p = jnp.where(mask_c, jnp.exp(s - jnp.max(s, axis=-1, keepdims=True)), 0.0)
        ps.append(p * (1.0 / jnp.maximum(jnp.sum(p, axis=-1, keepdims=True), 1e-30)))
    p4 = jnp.concatenate(ps, axis=0).astype(BF16)
    o_cmp = head_rows(jnp.dot(p4, vcmp_ref[...], preferred_element_type=F32))

    n_rows = ovl_ref.shape[0]
    imp4 = lax.dot_general(ovl_ref[...], p4, nt, preferred_element_type=F32)
    imp = imp4[:, 0:tq]
    for r in range(1, r_heads):
        imp = imp + imp4[:, r * tq:(r + 1) * tq]
    j_idx = lax.broadcasted_iota(I32, (n_rows, tq), 0)
    t_row = t0 + lax.broadcasted_iota(I32, (n_rows, tq), 1)
    q_blk = lax.shift_right_arithmetic(t_row, int(np.log2(SEL_BLOCK)))
    valid = (j_idx * SEL_BLOCK <= t_row) & (j_idx < n_sel)
    forced = (j_idx == 0) | (j_idx == q_blk) | (j_idx == q_blk - 1)
    val = jnp.where(j_idx < n_sel, jnp.where(forced, imp + 1e3, jnp.where(valid, imp, -1e3)), -2e3)
    rank = jnp.zeros((n_rows, tq), F32)
    for k in range(n_sel):
        vk = val[k:k + 1, :]
        rank = rank + jnp.where(vk > val, 1.0, jnp.where(vk == val, jnp.where(j_idx > k, 1.0, 0.0), 0.0))
    unselected = jnp.where(valid, jnp.where(rank < TOP_N, 0.0, NEG_INF), NEG_INF).astype(BF16)
    q4_sel = stack_q(lax.dot_general(unselected, place_ref[...], tn, preferred_element_type=F32))

    ck = NSA_KV_CHUNK

    def sel_step(c, carry, causal):
        c0 = pl.multiple_of(c * ck, ck)
        s4 = lax.dot_general(q4_sel, ks_ref[pl.ds(c0, ck), :], nt, preferred_element_type=F32)
        if causal:
            visible = t_col >= c0 + lax.broadcasted_iota(I32, (tq, ck), 1)
        ps, new = [], []
        for r, s in enumerate(head_rows(s4)):
            if causal:
                s = jnp.where(visible, s, NEG_INF)
            m_old, l_old, acc_old = carry[r]
            m_new = jnp.maximum(m_old, jnp.max(s, axis=-1, keepdims=True))
            p = jnp.exp(s - m_new)
            alpha = jnp.exp(m_old - m_new)
            ps.append(p)
            new.append((m_new, alpha * l_old + jnp.sum(p, axis=-1, keepdims=True), alpha * acc_old))
        pv = jnp.dot(jnp.concatenate(ps, axis=0).astype(BF16), vs_ref[pl.ds(c0, ck), :], preferred_element_type=F32)
        return tuple((m, l, acc + o) for (m, l, acc), o in zip(new, head_rows(pv)))

    init = tuple((jnp.full((tq, 1), NEG_INF, F32), jnp.zeros((tq, 1), F32), jnp.zeros((tq, HEAD_DIM), F32))
                 for _ in range(r_heads))
    c_last = t0 // ck
    carry = lax.fori_loop(0, c_last, lambda c, carry: sel_step(c, carry, False), init)
    o_sel = [acc / l for (_, l, acc) in sel_step(c_last, carry, True)]

    n_win = WINDOW + tq
    w0 = pl.multiple_of(jnp.maximum(t0 - WINDOW, 0), tq)
    dist_w = t_col - (w0 + lax.broadcasted_iota(I32, (tq, n_win), 1))
    mask_w = (dist_w >= 0) & (dist_w < WINDOW)
    s4 = lax.dot_general(q4, kw_ref[pl.ds(w0, n_win), :], nt, preferred_element_type=F32)
    ps, inv = [], []
    for s in head_rows(s4):
        s = jnp.where(mask_w, s, NEG_INF)
        p = jnp.exp(s - jnp.max(s, axis=-1, keepdims=True))
        ps.append(p)
        inv.append(1.0 / jnp.sum(p, axis=-1, keepdims=True))
    pv = jnp.dot(jnp.concatenate(ps, axis=0).astype(BF16), vw_ref[pl.ds(w0, n_win), :], preferred_element_type=F32)
    o_win = [o * i for o, i in zip(head_rows(pv), inv)]

    gate = _sigmoid(gate_ref[...])
    outs = []
    for r in range(r_heads):
        c = N_BRANCHES * r
        outs.append(gate[:, c:c + 1] * o_cmp[r] + gate[:, c + 1:c + 2] * o_sel[r] + gate[:, c + 2:c + 3] * o_win[r])
    o_ref[...] = jnp.concatenate(outs, axis=-1).astype(BF16)


def _nsa_attention(nq, cmp_kv, ks, vs, kw, vw, gates, b, s):
    t = b * s
    g, r = NSA_KV_GROUPS, NSA_GROUP_SIZE
    tq = NSA_Q_TILE
    n_q = s // tq
    n_sel = s // SEL_BLOCK
    n_pad = s // CMP_STRIDE
    n_cmp = n_pad - 1
    n_rows = -(-n_sel // 8) * 8
    assert 2 + n_sel <= HEAD_DIM and s % NSA_KV_CHUNK == 0 and s >= WINDOW + tq and NSA_KV_CHUNK % tq == 0
    slopes = jnp.asarray(2.0 ** (-8.0 * np.arange(1, NSA_HEADS + 1) / NSA_HEADS), F32)

    def key_features(pos, block_id):
        f = np.zeros((pos.shape[0], HEAD_DIM), np.float32)
        f[:, 0], f[:, 1] = pos // SEL_BLOCK, pos % SEL_BLOCK
        if block_id is not None:
            f[np.arange(pos.shape[0]), 2 + block_id] = 1.0
        return jnp.asarray(f, BF16)

    def augment(k, feat):
        return jnp.concatenate([k, jnp.broadcast_to(feat, k.shape[:-2] + feat.shape)], axis=-1)

    tok = np.arange(s)
    ks_aug = augment(ks, key_features(tok, tok // SEL_BLOCK))
    kw_aug = augment(kw, key_features(tok, None))
    cmp_start = np.arange(n_pad) * CMP_STRIDE
    kcmp_aug = augment(cmp_kv[0], key_features(cmp_start + CMP_BLOCK - 1, None))
    sel_start = np.arange(n_rows) * SEL_BLOCK
    ovl = ((cmp_start[None, :] < sel_start[:, None] + SEL_BLOCK) & (cmp_start[None, :] + CMP_BLOCK > sel_start[:, None])
           & (np.arange(n_pad)[None, :] < n_cmp) & (np.arange(n_rows)[:, None] < n_sel))
    ovl = jnp.asarray(ovl.astype(np.float32), BF16)
    place = jnp.asarray((np.arange(n_rows)[:, None] + 2 == np.arange(HEAD_DIM)[None, :]).astype(np.float32), BF16)
    seq = lambda w: pl.BlockSpec((None, None, s, w), lambda bi, gi, qi: (bi, gi, 0, 0))
    cmp = lambda w: pl.BlockSpec((None, None, n_pad, w), lambda bi, gi, qi: (bi, gi, 0, 0))
    n_gate = r * N_BRANCHES
    return pl.pallas_call(
        functools.partial(_nsa_kernel, n_sel=n_sel, n_cmp=n_cmp),
        name="nsa_attention",
        grid=(b, g, n_q),
        in_specs=[pl.BlockSpec(memory_space=pltpu.SMEM),
                  pl.BlockSpec((tq, r * HEAD_DIM), lambda bi, gi, qi: (bi * n_q + qi, gi)),
                  cmp(2 * HEAD_DIM), cmp(HEAD_DIM), seq(2 * HEAD_DIM), seq(HEAD_DIM), seq(2 * HEAD_DIM), seq(HEAD_DIM),
                  pl.BlockSpec((None, None, tq, n_gate), lambda bi, gi, qi: (bi, gi, qi, 0)),
                  pl.BlockSpec(ovl.shape, lambda bi, gi, qi: (0, 0)),
                  pl.BlockSpec(place.shape, lambda bi, gi, qi: (0, 0))],
        out_specs=pl.BlockSpec((tq, r * HEAD_DIM), lambda bi, gi, qi: (bi * n_q + qi, gi)),
        out_shape=jax.ShapeDtypeStruct((t, NSA_W), BF16),
        compiler_params=_params("arbitrary", "arbitrary", "arbitrary"),
    )(slopes, nq, kcmp_aug, cmp_kv[1], ks_aug, vs, kw_aug, vw, gates, ovl, place)


def _even_outproj_kernel(h_ref, a_ref, b_ref, wa_ref, wb_ref, o_ref):
    y = jnp.dot(a_ref[...], wa_ref[...], preferred_element_type=F32)
    y = y + jnp.dot(b_ref[...], wb_ref[...], preferred_element_type=F32)
    o_ref[...] = h_ref[...] + y


def _even_outproj(h, y_ret, y_nsa, w_out):
    t = h.shape[0]
    tm = ROW_TILE
    wb = w_out.astype(BF16)
    wa, wbt = wb[:RET_W], wb[RET_W:]
    row = lambda n: pl.BlockSpec((tm, n), lambda i: (i, 0))
    full = lambda a: pl.BlockSpec(a.shape, lambda i: (0, 0))
    return pl.pallas_call(
        _even_outproj_kernel,
        name="even_outproj",
        grid=(t // tm,),
        in_specs=[row(D_MODEL), row(RET_W), row(NSA_W), full(wa), full(wbt)],
        out_specs=row(D_MODEL),
        out_shape=jax.ShapeDtypeStruct((t, D_MODEL), F32),
        compiler_params=_params("arbitrary"),
    )(h, y_ret, y_nsa, wa, wbt)


def _odd_mixer_kernel(h_ref, g_ref, win_ref, cw_ref, cb_ref, wout_ref, o_ref, tail_ref):
    @pl.when(pl.program_id(1) == 0)
    def _():
        tail_ref[...] = jnp.zeros_like(tail_ref)

    h = h_ref[...]
    ts = h.shape[0]
    xn = _rms(h, g_ref[...]).astype(BF16)
    proj = jnp.dot(xn, win_ref[...], preferred_element_type=F32)
    gate_b, gate_c, hid = proj[:, 0:D_MODEL], proj[:, D_MODEL:2 * D_MODEL], proj[:, 2 * D_MODEL:3 * D_MODEL]
    u = gate_c * hid
    row = lax.broadcasted_iota(I32, (ts, D_MODEL), 0)
    tail = tail_ref[...]
    prev1, prev2 = tail[7:8, :], tail[6:7, :]
    u1 = jnp.where(row >= 1, pltpu.roll(u, 1, axis=0), prev1)
    u2 = jnp.where(row >= 2, pltpu.roll(u, 2, axis=0), jnp.where(row == 1, prev1, prev2))
    cw = cw_ref[...]
    y = cw[0:1, :] * u2 + cw[1:2, :] * u1 + cw[2:3, :] * u + cb_ref[...]
    tail_ref[...] = u[ts - 8:ts, :]
    z = (gate_b * y).astype(BF16)
    o_ref[...] = h + jnp.dot(z, wout_ref[...], preferred_element_type=F32)


def _odd_mixer(h, g, w_in, conv_w, conv_b, w_out, b, s):
    t = b * s
    ts = ROW_TILE
    n_s = s // ts
    win, wout = w_in.astype(BF16), w_out.astype(BF16)
    gg, cb = g[None, :], conv_b[None, :]
    cw = jnp.pad(conv_w, ((0, 8 - conv_w.shape[0]), (0, 0)))
    full = lambda a: pl.BlockSpec(a.shape, lambda bi, si: (0, 0))
    row = pl.BlockSpec((ts, D_MODEL), lambda bi, si: (bi * n_s + si, 0))
    return pl.pallas_call(
        _odd_mixer_kernel,
        name="odd_mixer",
        grid=(b, n_s),
        in_specs=[row, full(gg), full(win), full(cw), full(cb), full(wout)],
        out_specs=row,
        out_shape=jax.ShapeDtypeStruct((t, D_MODEL), F32),
        scratch_shapes=[pltpu.VMEM((8, D_MODEL), F32)],
        compiler_params=_params("arbitrary", "arbitrary"),
    )(h, gg, win, cw, cb, wout)


def _store_slabs(ref, x, stride):
    rows = x.shape[0]
    for j in range(SLAB):
        ref[pl.ds(j, rows, stride=stride), :] = x[:, j * LANES:(j + 1) * LANES]


def _load_slabs(ref, rows, stride, offset=0):
    return jnp.concatenate([ref[pl.ds(offset + j, rows, stride=stride), :] for j in range(SLAB)], axis=-1)


def _router_kernel(h_ref, g_ref, wr_ref, br_ref, xn_ref, ids_ref, gates_ref):
    xn = _rms(h_ref[...], g_ref[...])
    _store_slabs(xn_ref, xn, SLAB)
    logits = lax.dot_general(wr_ref[...], xn, (((1,), (1,)), ((), ())), precision=HIGHEST,
                             preferred_element_type=F32) + br_ref[...]
    row = lambda i: logits[i:i + 1, :]

    def softmax(xs):
        m = functools.reduce(jnp.maximum, xs)
        es = [jnp.exp(x - m) for x in xs]
        tot = functools.reduce(lambda a, c: a + c, es)
        return [e / tot for e in es]

    def argmax(ps):
        best_p, best_i = ps[0], jnp.zeros_like(ps[0], dtype=I32)
        for i in range(1, len(ps)):
            upd = ps[i] > best_p
            best_p = jnp.where(upd, ps[i], best_p)
            best_i = jnp.where(upd, i, best_i)
        return best_p, best_i

    grp_p, grp_i = argmax(softmax([row(i) for i in range(N_GROUPS)]))
    in_grp = []
    for e in range(EXPERTS_PER_GROUP):
        x = row(N_GROUPS + e)
        for gidx in range(1, N_GROUPS):
            x = jnp.where(grp_i == gidx, row(N_GROUPS + gidx * EXPERTS_PER_GROUP + e), x)
        in_grp.append(x)
    pe = softmax(in_grp)
    p1, i1 = argmax(pe)
    p2, i2 = argmax([jnp.where(i1 == e, -1.0, pe[e]) for e in range(EXPERTS_PER_GROUP)])
    tot = p1 + p2
    ids_ref[...] = jnp.concatenate([grp_i * EXPERTS_PER_GROUP + i1, grp_i * EXPERTS_PER_GROUP + i2], axis=0)
    gates_ref[...] = jnp.concatenate([grp_p * p1 / tot, grp_p * p2 / tot], axis=0)


def _router(h, g, wg_group, bg_group, wg_expert, bg_expert):
    t = h.shape[0]
    tm = ROW_TILE
    n_logit = N_GROUPS + N_EXPERTS
    wr = jnp.pad(jnp.concatenate([wg_group, wg_expert], axis=1).T, ((0, ROUTER_ROWS - n_logit), (0, 0)))
    br = jnp.pad(jnp.concatenate([bg_group, bg_expert]), (0, ROUTER_ROWS - n_logit))
    br = jnp.broadcast_to(br[:, None], (ROUTER_ROWS, tm))
    gg = g[None, :]
    full = lambda a: pl.BlockSpec(a.shape, lambda i: (0, 0))
    return pl.pallas_call(
        _router_kernel,
        name="moe_router",
        grid=(t // tm,),
        in_specs=[pl.BlockSpec((tm, D_MODEL), lambda i: (i, 0)), full(gg), full(wr), full(br)],
        out_specs=[pl.BlockSpec((tm * SLAB, LANES), lambda i: (i, 0)),
                   pl.BlockSpec((TOP_K_IN_GROUP, tm), lambda i: (0, i)),
                   pl.BlockSpec((TOP_K_IN_GROUP, tm), lambda i: (0, i))],
        out_shape=[jax.ShapeDtypeStruct((t * SLAB, LANES), F32), jax.ShapeDtypeStruct((TOP_K_IN_GROUP, t), I32),
                   jax.ShapeDtypeStruct((TOP_K_IN_GROUP, t), F32)],
        compiler_params=_params("arbitrary"),
    )(h, gg, wr, br)


def _expert_kernel(blk_expert_ref, n_used_ref, src0_ref, src_next_ref, dst_prev_ref, x_hbm, wg_ref, wu_ref, wd_ref,
                   y_hbm, xbuf, ybuf, sem_in, sem_out):
    del blk_expert_ref
    i = pl.program_id(0)
    n_used = n_used_ref[0]
    slot = i % 2
    other = 1 - slot

    def row_copy(hbm, hbm_row, buf, r, sem, to_hbm):
        h = hbm.at[pl.ds(pl.multiple_of(hbm_row * SLAB, SLAB), SLAB)]
        v = buf.at[pl.ds(r * SLAB, SLAB)]
        return pltpu.make_async_copy(v, h, sem) if to_hbm else pltpu.make_async_copy(h, v, sem)

    def start_gather(idx_ref, s):
        for r in range(MOE_BLOCK):
            row_copy(x_hbm, idx_ref[0, r], xbuf.at[s], r, sem_in.at[s], False).start()

    def wait_gather(s):
        for r in range(MOE_BLOCK):
            row_copy(x_hbm, 0, xbuf.at[s], r, sem_in.at[s], False).wait()

    def wait_scatter(s):
        for r in range(MOE_BLOCK):
            row_copy(y_hbm, 0, ybuf.at[s], r, sem_out.at[s], True).wait()

    @pl.when(i == 0)
    def _():
        ybuf[...] = jnp.zeros_like(ybuf)
        start_gather(src0_ref, 0)

    @pl.when(i <= n_used)
    def _():
        wait_gather(slot)

        @pl.when(i >= 1)
        def _():
            wait_scatter(slot)

        start_gather(src_next_ref, other)
        for r in range(MOE_BLOCK):
            row_copy(y_hbm, dst_prev_ref[0, r], ybuf.at[other], r, sem_out.at[other], True).start()
        x = _load_slabs(xbuf.at[slot], MOE_BLOCK, SLAB).astype(BF16)
        a = jnp.dot(x, wg_ref[...], preferred_element_type=F32)
        u = jnp.dot(x, wu_ref[...], preferred_element_type=F32)
        hid = (a * _sigmoid(a) * u).astype(BF16)
        _store_slabs(ybuf.at[slot], jnp.dot(hid, wd_ref[...], preferred_element_type=F32), SLAB)

        @pl.when(i == n_used)
        def _():
            wait_gather(other)
            wait_scatter(other)


def _moe_plan(ids, t):
    n_slot = t * TOP_K_IN_GROUP
    n_blocks = n_slot // MOE_BLOCK + N_EXPERTS
    flat_e = ids.T.reshape(-1)
    order = jnp.argsort(flat_e).astype(I32)
    counts = jnp.sum(flat_e[:, None] == jnp.arange(N_EXPERTS, dtype=I32)[None, :], axis=0, dtype=I32)
    padded = (counts + MOE_BLOCK - 1) // MOE_BLOCK * MOE_BLOCK
    start = jnp.cumsum(counts) - counts
    pend = jnp.cumsum(padded)
    blk_expert = jnp.minimum(jnp.searchsorted(pend, jnp.arange(n_blocks, dtype=I32) * MOE_BLOCK, side='right'),
                             N_EXPERTS - 1).astype(I32)
    pos = jnp.arange(n_blocks * MOE_BLOCK, dtype=I32)
    e_pos = jnp.repeat(blk_expert, MOE_BLOCK)
    off = pos - (pend - padded)[e_pos]
    is_slot = (off >= 0) & (off < counts[e_pos])
    slot = order[jnp.clip(start[e_pos] + off, 0, n_slot - 1)]
    src = jnp.where(is_slot, slot // TOP_K_IN_GROUP, 0).astype(I32).reshape(n_blocks, 1, MOE_BLOCK)
    dst = jnp.where(is_slot, slot, n_slot + pos % MOE_BLOCK).astype(I32).reshape(n_blocks, 1, MOE_BLOCK)
    trash = (n_slot + jnp.arange(MOE_BLOCK, dtype=I32)).reshape(1, 1, MOE_BLOCK)
    src = jnp.concatenate([src, jnp.zeros((1, 1, MOE_BLOCK), I32)])
    dst_prev = jnp.concatenate([trash, dst])
    n_used = (pend[-1] // MOE_BLOCK).astype(I32).reshape(1)
    return blk_expert, n_used, src, dst_prev


def _experts(xn_slabs, ids, w_gate, w_up, w_down):
    t = xn_slabs.shape[0] // SLAB
    blk_expert, n_used, src, dst_prev = _moe_plan(ids, t)
    n_blocks = blk_expert.shape[0]
    wg, wu, wd = w_gate.astype(BF16), w_up.astype(BF16), w_down.astype(BF16)
    idx_spec = lambda f: pl.BlockSpec((None, 1, MOE_BLOCK), lambda i, be, nu: (f(i), 0, 0), memory_space=pltpu.SMEM)
    w_spec = lambda a: pl.BlockSpec((None,) + a.shape[1:], lambda i, be, nu: (be[i], 0, 0))
    buf = pltpu.VMEM((2, MOE_BLOCK * SLAB, LANES), F32)
    return pl.pallas_call(
        _expert_kernel,
        name="moe_experts",
        grid_spec=pltpu.PrefetchScalarGridSpec(
            num_scalar_prefetch=2,
            grid=(n_blocks,),
            in_specs=[idx_spec(lambda i: 0), idx_spec(lambda i: i + 1), idx_spec(lambda i: i),
                      pl.BlockSpec(memory_space=pl.ANY), w_spec(wg), w_spec(wu), w_spec(wd)],
            out_specs=pl.BlockSpec(memory_space=pl.ANY),
            scratch_shapes=[buf, buf, pltpu.SemaphoreType.DMA((2,)), pltpu.SemaphoreType.DMA((2,))]),
        out_shape=jax.ShapeDtypeStruct(((t * TOP_K_IN_GROUP + MOE_BLOCK) * SLAB, LANES), F32),
        compiler_params=_params("arbitrary"),
    )(blk_expert, n_used, src, src, dst_prev, xn_slabs, wg, wu, wd)


def _combine_kernel(h_ref, y_ref, gate_ref, o_ref):
    gate = gate_ref[...]
    rows = h_ref.shape[0]
    y0 = _load_slabs(y_ref, rows, TOP_K_IN_GROUP * SLAB)
    y1 = _load_slabs(y_ref, rows, TOP_K_IN_GROUP * SLAB, SLAB)
    o_ref[...] = h_ref[...] + gate[:, 0:1] * y0 + gate[:, 1:2] * y1


def _combine(h, y_slabs, gates):
    t = h.shape[0]
    tm = ROW_TILE
    return pl.pallas_call(
        _combine_kernel,
        name="moe_combine",
        grid=(t // tm,),
        in_specs=[pl.BlockSpec((tm, D_MODEL), lambda i: (i, 0)),
                  pl.BlockSpec((tm * TOP_K_IN_GROUP * SLAB, LANES), lambda i: (i, 0)),
                  pl.BlockSpec((tm, TOP_K_IN_GROUP), lambda i: (i, 0))],
        out_specs=pl.BlockSpec((tm, D_MODEL), lambda i: (i, 0)),
        out_shape=jax.ShapeDtypeStruct((t, D_MODEL), F32),
        compiler_params=_params("arbitrary"),
    )(h, y_slabs, gates.T)


def _moe(h, g, wg_group, bg_group, wg_expert, bg_expert, w_gate, w_up, w_down):
    xn_slabs, ids, gates = _router(h, g, wg_group, bg_group, wg_expert, bg_expert)
    y_slabs = _experts(xn_slabs, ids, w_gate, w_up, w_down)
    return _combine(h, y_slabs, gates)


def _even_mixer(h, g, w_in, cmp_pe, w_cmp1, w_cmp2, q_norm_g, k_norm_g, ret_norm_g, w_out, b, s):
    ret, nq, kv, ng = _even_inproj(h, g, w_in, q_norm_g, k_norm_g)
    y_ret = _retention(ret, ret_norm_g, b, s)
    w = NSA_KV_W
    kc, vc, ks, vs, kw, vw = [kv[:, i * w:(i + 1) * w] for i in range(6)]
    cmp_kv = _compress(kc, vc, cmp_pe, w_cmp1, w_cmp2, k_norm_g, b, s)
    by_group = lambda a: a.reshape(b, s, NSA_KV_GROUPS, HEAD_DIM).transpose(0, 2, 1, 3)
    n_gate = NSA_GROUP_SIZE * N_BRANCHES
    gates = ng[:, :NSA_HEADS * N_BRANCHES].reshape(b, s, NSA_KV_GROUPS, n_gate).transpose(0, 2, 1, 3)
    y_nsa = _nsa_attention(nq, cmp_kv, by_group(ks), by_group(vs), by_group(kw), by_group(vw), gates, b, s)
    return _even_outproj(h, y_ret, y_nsa, w_out)


def kernel(x, mix_norm_g, ffn_norm_g, ev_w_in, ev_cmp_pe, ev_w_cmp1, ev_w_cmp2, ev_q_norm_g, ev_k_norm_g, ev_ret_norm_g, ev_w_out, od_w_in, od_conv_w, od_conv_b, od_w_out, moe_wg_group, moe_bg_group, moe_wg_expert, moe_bg_expert, moe_w_gate, moe_w_up, moe_w_down):
    b, s, d = x.shape
    h = x.reshape(b * s, d)
    for layer in range(mix_norm_g.shape[0]):
        i = layer // 2
        if layer % 2 == 0:
            h = _even_mixer(h, mix_norm_g[layer], ev_w_in[i], ev_cmp_pe[i], ev_w_cmp1[i], ev_w_cmp2[i],
                            ev_q_norm_g[i], ev_k_norm_g[i], ev_ret_norm_g[i], ev_w_out[i], b, s)
        else:
            h = _odd_mixer(h, mix_norm_g[layer], od_w_in[i], od_conv_w[i], od_conv_b[i], od_w_out[i], b, s)
        h = _moe(h, ffn_norm_g[layer], moe_wg_group[layer], moe_bg_group[layer], moe_wg_expert[layer],
                 moe_bg_expert[layer], moe_w_gate[layer], moe_w_up[layer], moe_w_down[layer])
    return h.reshape(b, s, d)
```

```python
import functools

import jax
import jax.numpy as jnp
import numpy as np
from jax import lax
from jax.experimental import pallas as pl
from jax.experimental.pallas import tpu as pltpu

F32 = jnp.float32
BF16 = jnp.bfloat16
I32 = jnp.int32
HIGHEST = lax.Precision.HIGHEST

D_MODEL = 1024
HEAD_DIM = 64
RET_HEADS = 8
NSA_HEADS = 8
NSA_KV_GROUPS = 2
NSA_GROUP_SIZE = NSA_HEADS // NSA_KV_GROUPS
RET_W = RET_HEADS * HEAD_DIM
NSA_W = NSA_HEADS * HEAD_DIM
NSA_KV_W = NSA_KV_GROUPS * HEAD_DIM
N_BRANCHES = 3
RET_CHUNK = 128
CMP_STRIDE = 16
CMP_BLOCK = 2 * CMP_STRIDE
CMP_HIDDEN = 128
SEL_BLOCK = 64
TOP_N = 8
WINDOW = 256
N_GROUPS = 4
EXPERTS_PER_GROUP = 8
N_EXPERTS = N_GROUPS * EXPERTS_PER_GROUP
TOP_K_IN_GROUP = 2
D_EXPERT = 256
RMS_EPS = 1e-6
NEG_INF = -1e30
ATTN_SCALE = HEAD_DIM ** -0.5

LANES = 128
VMEM_LIMIT = 48 * 1024 * 1024
ROW_TILE = 512
NSA_Q_TILE = 128
NSA_KV_CHUNK = 512
MOE_BLOCK = 256
ROUTER_ROWS = 40
SLAB = D_MODEL // LANES


def _params(*sem):
    return pltpu.CompilerParams(dimension_semantics=sem, vmem_limit_bytes=VMEM_LIMIT)


def _rms(x, g):
    return x * lax.rsqrt(jnp.mean(x * x, axis=-1, keepdims=True) + RMS_EPS) * g


def _group_rms(x, bd, gain):
    ms = jnp.dot(x * x, bd, precision=HIGHEST, preferred_element_type=F32)
    return x * lax.rsqrt(ms + RMS_EPS) * gain


def _sigmoid(x):
    return 1.0 / (1.0 + jnp.exp(-x))


def _block_diag_mean(n):
    idx = np.arange(n) // HEAD_DIM
    return jnp.asarray((idx[:, None] == idx[None, :]).astype(np.float32) / HEAD_DIM)


def _even_inproj_kernel(x_ref, g_ref, wret_ref, wnq_ref, wkv_ref, wng_ref, qgain_ref, kgain_ref, bd512_ref, bd128_ref,
                        ret_ref, nq_ref, kcv_ref, ks_ref, vs_ref, kw_ref, vw_ref, gate_ref, *, seq_tiles):
    tm = x_ref.shape[0]
    xn = _rms(x_ref[...], g_ref[...]).astype(BF16)
    ret_ref[...] = jnp.dot(xn, wret_ref[...], preferred_element_type=F32).astype(BF16)
    nq = jnp.dot(xn, wnq_ref[...], preferred_element_type=F32)
    nq_ref[...] = _group_rms(nq, bd512_ref[...], qgain_ref[...]).astype(BF16)
    kv = jnp.dot(xn, wkv_ref[...], preferred_element_type=F32)
    w = NSA_KV_W
    kcv_ref[...] = kv[:, 0:2 * w].astype(BF16)
    ks = _group_rms(kv[:, 2 * w:3 * w], bd128_ref[...], kgain_ref[0:1, :])
    kw = _group_rms(kv[:, 4 * w:5 * w], bd128_ref[...], kgain_ref[1:2, :])
    vs, vw = kv[:, 3 * w:4 * w], kv[:, 5 * w:6 * w]
    pos = (pl.program_id(0) % seq_tiles) * tm + lax.broadcasted_iota(I32, (tm, HEAD_DIM), 0)
    lane = lax.broadcasted_iota(I32, (tm, HEAD_DIM), 1)
    blk = lax.shift_right_arithmetic(pos, int(np.log2(SEL_BLOCK)))
    feat_w = jnp.where(lane == 0, blk, jnp.where(lane == 1, pos & (SEL_BLOCK - 1), 0)).astype(F32)
    feat_s = feat_w + jnp.where(lane == blk + 2, 1.0, 0.0)
    for g in range(NSA_KV_GROUPS):
        sl = slice(g * HEAD_DIM, (g + 1) * HEAD_DIM)
        ks_ref[g] = jnp.concatenate([ks[:, sl], feat_s], axis=-1).astype(BF16)
        kw_ref[g] = jnp.concatenate([kw[:, sl], feat_w], axis=-1).astype(BF16)
        vs_ref[g] = vs[:, sl].astype(BF16)
        vw_ref[g] = vw[:, sl].astype(BF16)
    ng = jnp.dot(xn, wng_ref[...], preferred_element_type=F32)
    for g in range(NSA_KV_GROUPS):
        gate_ref[g] = ng[:, g * LANES:(g + 1) * LANES]


def _even_inproj(h, g, w_in, q_norm_g, k_norm_g, s):
    t = h.shape[0]
    tm = ROW_TILE
    c_ret, c_nq, c_kv = 4 * RET_W, 4 * RET_W + NSA_W, 4 * RET_W + NSA_W + 6 * NSA_KV_W
    n_gate = NSA_GROUP_SIZE * N_BRANCHES
    wb = w_in.astype(BF16)
    wret, wnq, wkv = wb[:, :c_ret], wb[:, c_ret:c_nq], wb[:, c_nq:c_kv]
    wng = jnp.concatenate([jnp.pad(wb[:, c_kv + gi * n_gate:c_kv + (gi + 1) * n_gate], ((0, 0), (0, LANES - n_gate)))
                           for gi in range(NSA_KV_GROUPS)], axis=1)
    qgain = jnp.tile(q_norm_g, NSA_HEADS)[None, :]
    kgain = jnp.stack([jnp.tile(k_norm_g[1], NSA_KV_GROUPS), jnp.tile(k_norm_g[2], NSA_KV_GROUPS)])
    full = lambda a: pl.BlockSpec(a.shape, lambda i: (0,) * a.ndim)
    row = lambda n: pl.BlockSpec((tm, n), lambda i: (i, 0))
    grp = lambda n: pl.BlockSpec((NSA_KV_GROUPS, tm, n), lambda i: (0, i, 0))
    grp_shape = lambda n, dt: jax.ShapeDtypeStruct((NSA_KV_GROUPS, t, n), dt)
    bd512, bd128 = _block_diag_mean(NSA_W), _block_diag_mean(NSA_KV_W)
    gg = g[None, :]
    return pl.pallas_call(
        functools.partial(_even_inproj_kernel, seq_tiles=s // tm),
        name="even_inproj",
        grid=(t // tm,),
        in_specs=[row(D_MODEL), full(gg), full(wret), full(wnq), full(wkv), full(wng), full(qgain),
                  full(kgain), full(bd512), full(bd128)],
        out_specs=[row(c_ret), row(NSA_W), row(2 * NSA_KV_W), grp(2 * HEAD_DIM), grp(HEAD_DIM), grp(2 * HEAD_DIM),
                   grp(HEAD_DIM), grp(LANES)],
        out_shape=[jax.ShapeDtypeStruct((t, c_ret), BF16), jax.ShapeDtypeStruct((t, NSA_W), BF16),
                   jax.ShapeDtypeStruct((t, 2 * NSA_KV_W), BF16), grp_shape(2 * HEAD_DIM, BF16), grp_shape(HEAD_DIM, BF16),
                   grp_shape(2 * HEAD_DIM, BF16), grp_shape(HEAD_DIM, BF16), grp_shape(LANES, F32)],
        compiler_params=_params("arbitrary"),
    )(h, gg, wret, wnq, wkv, wng, qgain, kgain, bd512, bd128)


def _retention_kernel(q_ref, k_ref, v_ref, rg_ref, decay_ref, qd_ref, kd_ref, cd_ref, gain_ref, o_ref):
    n_chunks = q_ref.shape[0] // RET_CHUNK
    gain = gain_ref[...]
    nt = (((1,), (1,)), ((), ()))
    tn = (((0,), (0,)), ((), ()))

    def body(c, states):
        r0 = pl.multiple_of(c * RET_CHUNK, RET_CHUNK)
        rows = pl.ds(r0, RET_CHUNK)
        q2, k2, v2 = q_ref[rows, :], k_ref[rows, :], v_ref[rows, :]
        rg2 = rg_ref[rows, :].astype(F32)
        outs, new_states = [], []
        for hh in range(2):
            lanes = slice(hh * HEAD_DIM, (hh + 1) * HEAD_DIM)
            q, v = q2[:, lanes], v2[:, lanes]
            kf = k2[:, lanes].astype(F32) * ATTN_SCALE
            state = states[hh]
            s = lax.dot_general(q, kf.astype(BF16), nt, preferred_element_type=F32) * decay_ref[hh]
            y = jnp.dot(s.astype(BF16), v, preferred_element_type=F32)
            qd = (q.astype(F32) * qd_ref[hh]).astype(BF16)
            y = y + jnp.dot(qd, state.astype(BF16), preferred_element_type=F32)
            kd = (kf * kd_ref[hh]).astype(BF16)
            new_states.append(cd_ref[hh] * state + lax.dot_general(kd, v, tn, preferred_element_type=F32))
            rg = rg2[:, lanes]
            outs.append(_rms(y, gain) * (rg * _sigmoid(rg)))
        o_ref[rows, :] = jnp.concatenate(outs, axis=-1).astype(BF16)
        return tuple(new_states)

    zero = jnp.zeros((HEAD_DIM, HEAD_DIM), F32)
    lax.fori_loop(0, n_chunks, body, (zero, zero), unroll=True)


def _retention_tables():
    h = np.arange(RET_HEADS, dtype=np.float64)
    log_g = np.log(1.0 - 2.0 ** (-5.0 - h))
    pos = np.arange(RET_CHUNK, dtype=np.float64)
    diff = pos[:, None] - pos[None, :]
    decay = np.where(diff >= 0, np.exp(log_g[:, None, None] * np.maximum(diff, 0.0)), 0.0)
    qd = np.exp(log_g[:, None] * (pos + 1.0))[:, :, None] * np.ones((1, 1, HEAD_DIM))
    kd = np.exp(log_g[:, None] * (RET_CHUNK - 1 - pos))[:, :, None] * np.ones((1, 1, HEAD_DIM))
    cd = np.exp(log_g * RET_CHUNK)[:, None, None] * np.ones((1, HEAD_DIM, HEAD_DIM))
    return [jnp.asarray(a, F32) for a in (decay, qd, kd, cd)]


def _retention(ret, ret_norm_g, b, s):
    t = b * s
    n_pairs = RET_HEADS // 2
    decay, qd, kd, cd = _retention_tables()
    col = lambda off: pl.BlockSpec((s, LANES), lambda bi, p: (bi, off * n_pairs + p))
    tab = lambda a: pl.BlockSpec((2,) + a.shape[1:], lambda bi, p: (p, 0, 0))
    gain = ret_norm_g[None, :]
    return pl.pallas_call(
        _retention_kernel,
        name="retention",
        grid=(b, n_pairs),
        in_specs=[col(0), col(1), col(2), col(3), tab(decay), tab(qd), tab(kd), tab(cd),
                  pl.BlockSpec(gain.shape, lambda bi, p: (0, 0))],
        out_specs=pl.BlockSpec((s, LANES), lambda bi, p: (bi, p)),
        out_shape=jax.ShapeDtypeStruct((t, RET_W), BF16),
        compiler_params=_params("arbitrary", "arbitrary"),
    )(ret, ret, ret, ret, decay, qd, kd, cd, gain)


def _compress_kernel(x_ref, pe_ref, w1_ref, w2_ref, kgain_ref, o_ref):
    is_key = pl.program_id(0) == 0
    half = CMP_STRIDE * HEAD_DIM
    x = x_ref[...]
    n = x.shape[0]
    a = jnp.dot(x, w1_ref[0:half, :], preferred_element_type=F32)
    bm = jnp.dot(x, w1_ref[half:2 * half, :], preferred_element_type=F32)
    pew = jnp.dot(pe_ref[...], w1_ref[...], preferred_element_type=F32)[0:1, :]
    pre = a + pltpu.roll(bm, n - 1, axis=0) + pew
    hid = 0.5 * pre * (1.0 + jnp.tanh(np.sqrt(2.0 / np.pi) * (pre + 0.044715 * pre * pre * pre)))
    out = jnp.dot(hid.astype(BF16), w2_ref[...], preferred_element_type=F32)
    normed = _rms(out, kgain_ref[...])
    o_ref[...] = jnp.where(is_key, normed, out).astype(BF16)


def _compress(kc, vc, cmp_pe, w_cmp1, w_cmp2, k_norm_g, b, s):
    n = s // CMP_STRIDE
    g = NSA_KV_GROUPS
    flat = CMP_STRIDE * HEAD_DIM

    def chunks(a):
        return a.reshape(b, n, CMP_STRIDE, g, HEAD_DIM).transpose(0, 3, 1, 2, 4).reshape(b, g, n, flat)

    x = jnp.stack([chunks(kc), chunks(vc)])
    pe = jnp.broadcast_to(cmp_pe.reshape(2, 1, 2 * flat), (2, 8, 2 * flat)).astype(BF16)
    w1, w2 = w_cmp1.astype(BF16), w_cmp2.astype(BF16)
    kgain = k_norm_g[0][None, :]
    return pl.pallas_call(
        _compress_kernel,
        name="nsa_compress",
        grid=(2, b, g),
        in_specs=[pl.BlockSpec((None, None, None, n, flat), lambda kv, bi, gi: (kv, bi, gi, 0, 0)),
                  pl.BlockSpec((None, 8, 2 * flat), lambda kv, bi, gi: (kv, 0, 0)),
                  pl.BlockSpec((None, 2 * flat, CMP_HIDDEN), lambda kv, bi, gi: (kv, 0, 0)),
                  pl.BlockSpec((None, CMP_HIDDEN, HEAD_DIM), lambda kv, bi, gi: (kv, 0, 0)),
                  pl.BlockSpec(kgain.shape, lambda kv, bi, gi: (0, 0))],
        out_specs=pl.BlockSpec((None, None, None, n, HEAD_DIM), lambda kv, bi, gi: (kv, bi, gi, 0, 0)),
        out_shape=jax.ShapeDtypeStruct((2, b, g, n, HEAD_DIM), BF16),
        compiler_params=_params("arbitrary", "arbitrary", "arbitrary"),
    )(x, pe, w1, w2, kgain)


def _nsa_kernel(slopes_ref, q_ref, kcmp_ref, vcmp_ref, ks_ref, vs_ref, kw_ref, vw_ref, gate_ref, ovl_ref, place_ref,
                o_ref, *, n_sel, n_cmp):
    r_heads = NSA_GROUP_SIZE
    tq = q_ref.shape[0]
    gi = pl.program_id(1)
    t0 = pl.program_id(2) * tq
    nt = (((1,), (1,)), ((), ()))
    tn = (((0,), (0,)), ((), ()))

    q = q_ref[...]
    lane = lax.broadcasted_iota(I32, (tq, HEAD_DIM), 1)
    qs, feats = [], []
    for r in range(r_heads):
        slope = slopes_ref[gi * r_heads + r]
        qs.append((q[:, r * HEAD_DIM:(r + 1) * HEAD_DIM].astype(F32) * ATTN_SCALE).astype(BF16))
        feats.append(jnp.where(lane == 0, slope * SEL_BLOCK, jnp.where(lane == 1, slope, 0.0)))

    def stack_q(extra):
        return jnp.concatenate([jnp.concatenate([qs[r], (feats[r] + extra).astype(BF16)], axis=-1)
                                for r in range(r_heads)], axis=0)

    q4 = stack_q(0.0)
    t_col = t0 + lax.broadcasted_iota(I32, (tq, 1), 0)

    def head_rows(x):
        return [x[r * tq:(r + 1) * tq] for r in range(r_heads)]

    n_pad = kcmp_ref.shape[0]
    c_idx = lax.broadcasted_iota(I32, (tq, n_pad), 1)
    mask_c = (t_col >= c_idx * CMP_STRIDE + (CMP_BLOCK - 1)) & (c_idx < n_cmp)
    s4 = lax.dot_general(q4, kcmp_ref[...], nt, preferred_element_type=F32)
    ps = []
    for s in head_rows(s4):
        s = jnp.where(mask_c, s, NEG_INF)
        p = jnp.where(mask_c, jnp.exp(s - jnp.max(s, axis=-1, keepdims=True)), 0.0)
        ps.append(p * (1.0 / jnp.maximum(jnp.sum(p, axis=-1, keepdims=True), 1e-30)))
    p4 = jnp.concatenate(ps, axis=0).astype(BF16)
    o_cmp = head_rows(jnp.dot(p4, vcmp_ref[...], preferred_element_type=F32))

    n_rows = ovl_ref.shape[0]
    imp4 = lax.dot_general(ovl_ref[...], p4, nt, preferred_element_type=F32)
    imp = imp4[:, 0:tq]
    for r in range(1, r_heads):
        imp = imp + imp4[:, r * tq:(r + 1) * tq]
    j_idx = lax.broadcasted_iota(I32, (n_rows, tq), 0)
    t_row = t0 + lax.broadcasted_iota(I32, (n_rows, tq), 1)
    q_blk = lax.shift_right_arithmetic(t_row, int(np.log2(SEL_BLOCK)))
    valid = (j_idx * SEL_BLOCK <= t_row) & (j_idx < n_sel)
    forced = (j_idx == 0) | (j_idx == q_blk) | (j_idx == q_blk - 1)
    val = jnp.where(j_idx < n_sel, jnp.where(forced, imp + 1e3, jnp.where(valid, imp, -1e3)), -2e3)
    rank = jnp.zeros((n_rows, tq), F32)
    for k in range(n_sel):
        vk = val[k:k + 1, :]
        rank = rank + jnp.where(vk > val, 1.0, jnp.where(vk == val, jnp.where(j_idx > k, 1.0, 0.0), 0.0))
    unselected = jnp.where(valid, jnp.where(rank < TOP_N, 0.0, NEG_INF), NEG_INF).astype(BF16)
    q4_sel = stack_q(lax.dot_general(unselected, place_ref[...], tn, preferred_element_type=F32))

    ck = NSA_KV_CHUNK

    def sel_step(c, carry, causal):
        c0 = pl.multiple_of(c * ck, ck)
        s4 = lax.dot_general(q4_sel, ks_ref[pl.ds(c0, ck), :], nt, preferred_element_type=F32)
        if causal:
            visible = t_col >= c0 + lax.broadcasted_iota(I32, (tq, ck), 1)
        ps, new = [], []
        for r, s in enumerate(head_rows(s4)):
            if causal:
                s = jnp.where(visible, s, NEG_INF)
            m_old, l_old, acc_old = carry[r]
            m_new = jnp.maximum(m_old, jnp.max(s, axis=-1, keepdims=True))
            p = jnp.exp(s - m_new)
            alpha = jnp.exp(m_old - m_new)
            ps.append(p)
            new.append((m_new, alpha * l_old + jnp.sum(p, axis=-1, keepdims=True), alpha * acc_old))
        pv = jnp.dot(jnp.concatenate(ps, axis=0).astype(BF16), vs_ref[pl.ds(c0, ck), :], preferred_element_type=F32)
        return tuple((m, l, acc + o) for (m, l, acc), o in zip(new, head_rows(pv)))

    init = tuple((jnp.full((tq, 1), NEG_INF, F32), jnp.zeros((tq, 1), F32), jnp.zeros((tq, HEAD_DIM), F32))
                 for _ in range(r_heads))
    c_last = t0 // ck
    carry = lax.fori_loop(0, c_last, lambda c, carry: sel_step(c, carry, False), init)
    o_sel = [acc / l for (_, l, acc) in sel_step(c_last, carry, True)]

    n_win = WINDOW + tq
    w0 = pl.multiple_of(jnp.maximum(t0 - WINDOW, 0), tq)
    dist_w = t_col - (w0 + lax.broadcasted_iota(I32, (tq, n_win), 1))
    mask_w = (dist_w >= 0) & (dist_w < WINDOW)
    s4 = lax.dot_general(q4, kw_ref[pl.ds(w0, n_win), :], nt, preferred_element_type=F32)
    ps, inv = [], []
    for s in head_rows(s4):
        s = jnp.where(mask_w, s, NEG_INF)
        p = jnp.exp(s - jnp.max(s, axis=-1, keepdims=True))
        ps.append(p)
        inv.append(1.0 / jnp.sum(p, axis=-1, keepdims=True))
    pv = jnp.dot(jnp.concatenate(ps, axis=0).astype(BF16), vw_ref[pl.ds(w0, n_win), :], preferred_element_type=F32)
    o_win = [o * i for o, i in zip(head_rows(pv), inv)]

    gate = _sigmoid(gate_ref[...])
    outs = []
    for r in range(r_heads):
        c = N_BRANCHES * r
        outs.append(gate[:, c:c + 1] * o_cmp[r] + gate[:, c + 1:c + 2] * o_sel[r] + gate[:, c + 2:c + 3] * o_win[r])
    o_ref[...] = jnp.concatenate(outs, axis=-1).astype(BF16)


def _nsa_attention(nq, cmp_kv, ks_aug, vs, kw_aug, vw, gates, b, s):
    t = b * s
    g, r = NSA_KV_GROUPS, NSA_GROUP_SIZE
    tq = NSA_Q_TILE
    n_q = s // tq
    n_sel = s // SEL_BLOCK
    n_pad = s // CMP_STRIDE
    n_cmp = n_pad - 1
    n_rows = -(-n_sel // 8) * 8
    assert 2 + n_sel <= HEAD_DIM and s % NSA_KV_CHUNK == 0 and s >= WINDOW + tq and NSA_KV_CHUNK % tq == 0
    slopes = jnp.asarray(2.0 ** (-8.0 * np.arange(1, NSA_HEADS + 1) / NSA_HEADS), F32)
    cmp_start = np.arange(n_pad) * CMP_STRIDE
    cmp_end = cmp_start + CMP_BLOCK - 1
    feat = np.zeros((n_pad, HEAD_DIM), np.float32)
    feat[:, 0], feat[:, 1] = cmp_end // SEL_BLOCK, cmp_end % SEL_BLOCK
    kcmp_aug = jnp.concatenate([cmp_kv[0], jnp.broadcast_to(jnp.asarray(feat, BF16), cmp_kv[0].shape)], axis=-1)
    sel_start = np.arange(n_rows) * SEL_BLOCK
    ovl = ((cmp_start[None, :] < sel_start[:, None] + SEL_BLOCK) & (cmp_start[None, :] + CMP_BLOCK > sel_start[:, None])
           & (np.arange(n_pad)[None, :] < n_cmp) & (np.arange(n_rows)[:, None] < n_sel))
    ovl = jnp.asarray(ovl.astype(np.float32), BF16)
    place = jnp.asarray((np.arange(n_rows)[:, None] + 2 == np.arange(HEAD_DIM)[None, :]).astype(np.float32), BF16)
    seq = lambda w: pl.BlockSpec((None, s, w), lambda bi, gi, qi: (gi, bi, 0))
    cmp = lambda w: pl.BlockSpec((None, None, n_pad, w), lambda bi, gi, qi: (bi, gi, 0, 0))
    return pl.pallas_call(
        functools.partial(_nsa_kernel, n_sel=n_sel, n_cmp=n_cmp),
        name="nsa_attention",
        grid=(b, g, n_q),
        in_specs=[pl.BlockSpec(memory_space=pltpu.SMEM),
                  pl.BlockSpec((tq, r * HEAD_DIM), lambda bi, gi, qi: (bi * n_q + qi, gi)),
                  cmp(2 * HEAD_DIM), cmp(HEAD_DIM), seq(2 * HEAD_DIM), seq(HEAD_DIM), seq(2 * HEAD_DIM), seq(HEAD_DIM),
                  pl.BlockSpec((None, tq, LANES), lambda bi, gi, qi: (gi, bi * n_q + qi, 0)),
                  pl.BlockSpec(ovl.shape, lambda bi, gi, qi: (0, 0)),
                  pl.BlockSpec(place.shape, lambda bi, gi, qi: (0, 0))],
        out_specs=pl.BlockSpec((tq, r * HEAD_DIM), lambda bi, gi, qi: (bi * n_q + qi, gi)),
        out_shape=jax.ShapeDtypeStruct((t, NSA_W), BF16),
        compiler_params=_params("arbitrary", "arbitrary", "arbitrary"),
    )(slopes, nq, kcmp_aug, cmp_kv[1], ks_aug, vs, kw_aug, vw, gates, ovl, place)


def _even_outproj_kernel(h_ref, a_ref, b_ref, wa_ref, wb_ref, o_ref):
    y = jnp.dot(a_ref[...], wa_ref[...], preferred_element_type=F32)
    y = y + jnp.dot(b_ref[...], wb_ref[...], preferred_element_type=F32)
    o_ref[...] = h_ref[...] + y


def _even_outproj(h, y_ret, y_nsa, w_out):
    t = h.shape[0]
    tm = ROW_TILE
    wb = w_out.astype(BF16)
    wa, wbt = wb[:RET_W], wb[RET_W:]
    row = lambda n: pl.BlockSpec((tm, n), lambda i: (i, 0))
    full = lambda a: pl.BlockSpec(a.shape, lambda i: (0, 0))
    return pl.pallas_call(
        _even_outproj_kernel,
        name="even_outproj",
        grid=(t // tm,),
        in_specs=[row(D_MODEL), row(RET_W), row(NSA_W), full(wa), full(wbt)],
        out_specs=row(D_MODEL),
        out_shape=jax.ShapeDtypeStruct((t, D_MODEL), F32),
        compiler_params=_params("arbitrary"),
    )(h, y_ret, y_nsa, wa, wbt)


def _odd_mixer_kernel(h_ref, g_ref, win_ref, cw_ref, cb_ref, wout_ref, o_ref, tail_ref):
    @pl.when(pl.program_id(1) == 0)
    def _():
        tail_ref[...] = jnp.zeros_like(tail_ref)

    h = h_ref[...]
    ts = h.shape[0]
    xn = _rms(h, g_ref[...]).astype(BF16)
    proj = jnp.dot(xn, win_ref[...], preferred_element_type=F32)
    gate_b, gate_c, hid = proj[:, 0:D_MODEL], proj[:, D_MODEL:2 * D_MODEL], proj[:, 2 * D_MODEL:3 * D_MODEL]
    u = gate_c * hid
    row = lax.broadcasted_iota(I32, (ts, D_MODEL), 0)
    tail = tail_ref[...]
    prev1, prev2 = tail[7:8, :], tail[6:7, :]
    u1 = jnp.where(row >= 1, pltpu.roll(u, 1, axis=0), prev1)
    u2 = jnp.where(row >= 2, pltpu.roll(u, 2, axis=0), jnp.where(row == 1, prev1, prev2))
    cw = cw_ref[...]
    y = cw[0:1, :] * u2 + cw[1:2, :] * u1 + cw[2:3, :] * u + cb_ref[...]
    tail_ref[...] = u[ts - 8:ts, :]
    z = (gate_b * y).astype(BF16)
    o_ref[...] = h + jnp.dot(z, wout_ref[...], preferred_element_type=F32)


def _odd_mixer(h, g, w_in, conv_w, conv_b, w_out, b, s):
    t = b * s
    ts = ROW_TILE
    n_s = s // ts
    win, wout = w_in.astype(BF16), w_out.astype(BF16)
    gg, cb = g[None, :], conv_b[None, :]
    cw = jnp.pad(conv_w, ((0, 8 - conv_w.shape[0]), (0, 0)))
    full = lambda a: pl.BlockSpec(a.shape, lambda bi, si: (0, 0))
    row = pl.BlockSpec((ts, D_MODEL), lambda bi, si: (bi * n_s + si, 0))
    return pl.pallas_call(
        _odd_mixer_kernel,
        name="odd_mixer",
        grid=(b, n_s),
        in_specs=[row, full(gg), full(win), full(cw), full(cb), full(wout)],
        out_specs=row,
        out_shape=jax.ShapeDtypeStruct((t, D_MODEL), F32),
        scratch_shapes=[pltpu.VMEM((8, D_MODEL), F32)],
        compiler_params=_params("arbitrary", "arbitrary"),
    )(h, gg, win, cw, cb, wout)


def _store_slabs(ref, x, stride):
    rows = x.shape[0]
    for j in range(SLAB):
        ref[pl.ds(j, rows, stride=stride), :] = x[:, j * LANES:(j + 1) * LANES]


def _load_slabs(ref, rows, stride, offset=0):
    return jnp.concatenate([ref[pl.ds(offset + j, rows, stride=stride), :] for j in range(SLAB)], axis=-1)


def _router_kernel(h_ref, g_ref, wr_ref, br_ref, xn_ref, ids_ref, gates_ref):
    xn = _rms(h_ref[...], g_ref[...])
    _store_slabs(xn_ref, xn, SLAB)
    logits = lax.dot_general(wr_ref[...], xn, (((1,), (1,)), ((), ())), precision=HIGHEST,
                             preferred_element_type=F32) + br_ref[...]
    row = lambda i: logits[i:i + 1, :]

    def softmax(xs):
        m = functools.reduce(jnp.maximum, xs)
        es = [jnp.exp(x - m) for x in xs]
        tot = functools.reduce(lambda a, c: a + c, es)
        return [e / tot for e in es]

    def argmax(ps):
        best_p, best_i = ps[0], jnp.zeros_like(ps[0], dtype=I32)
        for i in range(1, len(ps)):
            upd = ps[i] > best_p
            best_p = jnp.where(upd, ps[i], best_p)
            best_i = jnp.where(upd, i, best_i)
        return best_p, best_i

    grp_p, grp_i = argmax(softmax([row(i) for i in range(N_GROUPS)]))
    in_grp = []
    for e in range(EXPERTS_PER_GROUP):
        x = row(N_GROUPS + e)
        for gidx in range(1, N_GROUPS):
            x = jnp.where(grp_i == gidx, row(N_GROUPS + gidx * EXPERTS_PER_GROUP + e), x)
        in_grp.append(x)
    pe = softmax(in_grp)
    p1, i1 = argmax(pe)
    p2, i2 = argmax([jnp.where(i1 == e, -1.0, pe[e]) for e in range(EXPERTS_PER_GROUP)])
    tot = p1 + p2
    ids_ref[...] = jnp.concatenate([grp_i * EXPERTS_PER_GROUP + i1, grp_i * EXPERTS_PER_GROUP + i2], axis=0)
    gates_ref[...] = jnp.concatenate([grp_p * p1 / tot, grp_p * p2 / tot], axis=0)


def _router(h, g, wg_group, bg_group, wg_expert, bg_expert):
    t = h.shape[0]
    tm = ROW_TILE
    n_logit = N_GROUPS + N_EXPERTS
    wr = jnp.pad(jnp.concatenate([wg_group, wg_expert], axis=1).T, ((0, ROUTER_ROWS - n_logit), (0, 0)))
    br = jnp.pad(jnp.concatenate([bg_group, bg_expert]), (0, ROUTER_ROWS - n_logit))
    br = jnp.broadcast_to(br[:, None], (ROUTER_ROWS, tm))
    gg = g[None, :]
    full = lambda a: pl.BlockSpec(a.shape, lambda i: (0, 0))
    return pl.pallas_call(
        _router_kernel,
        name="moe_router",
        grid=(t // tm,),
        in_specs=[pl.BlockSpec((tm, D_MODEL), lambda i: (i, 0)), full(gg), full(wr), full(br)],
        out_specs=[pl.BlockSpec((tm * SLAB, LANES), lambda i: (i, 0)),
                   pl.BlockSpec((TOP_K_IN_GROUP, tm), lambda i: (0, i)),
                   pl.BlockSpec((TOP_K_IN_GROUP, tm), lambda i: (0, i))],
        out_shape=[jax.ShapeDtypeStruct((t * SLAB, LANES), F32), jax.ShapeDtypeStruct((TOP_K_IN_GROUP, t), I32),
                   jax.ShapeDtypeStruct((TOP_K_IN_GROUP, t), F32)],
        compiler_params=_params("arbitrary"),
    )(h, gg, wr, br)


def _expert_kernel(blk_expert_ref, n_used_ref, src0_ref, src_next_ref, dst_prev_ref, x_hbm, wg_ref, wu_ref, wd_ref,
                   y_hbm, xbuf, ybuf, sem_in, sem_out):
    del blk_expert_ref
    i = pl.program_id(0)
    n_used = n_used_ref[0]
    slot = i % 2
    other = 1 - slot

    def row_copy(hbm, hbm_row, buf, r, sem, to_hbm):
        h = hbm.at[pl.ds(pl.multiple_of(hbm_row * SLAB, SLAB), SLAB)]
        v = buf.at[pl.ds(r * SLAB, SLAB)]
        return pltpu.make_async_copy(v, h, sem) if to_hbm else pltpu.make_async_copy(h, v, sem)

    def start_gather(idx_ref, s):
        for r in range(MOE_BLOCK):
            row_copy(x_hbm, idx_ref[0, r], xbuf.at[s], r, sem_in.at[s], False).start()

    def wait_gather(s):
        for r in range(MOE_BLOCK):
            row_copy(x_hbm, 0, xbuf.at[s], r, sem_in.at[s], False).wait()

    def wait_scatter(s):
        for r in range(MOE_BLOCK):
            row_copy(y_hbm, 0, ybuf.at[s], r, sem_out.at[s], True).wait()

    @pl.when(i == 0)
    def _():
        ybuf[...] = jnp.zeros_like(ybuf)
        start_gather(src0_ref, 0)

    @pl.when(i <= n_used)
    def _():
        wait_gather(slot)

        @pl.when(i >= 1)
        def _():
            wait_scatter(slot)

        start_gather(src_next_ref, other)
        for r in range(MOE_BLOCK):
            row_copy(y_hbm, dst_prev_ref[0, r], ybuf.at[other], r, sem_out.at[other], True).start()
        x = _load_slabs(xbuf.at[slot], MOE_BLOCK, SLAB).astype(BF16)
        a = jnp.dot(x, wg_ref[...].astype(BF16), preferred_element_type=F32)
        u = jnp.dot(x, wu_ref[...].astype(BF16), preferred_element_type=F32)
        hid = (a * _sigmoid(a) * u).astype(BF16)
        _store_slabs(ybuf.at[slot], jnp.dot(hid, wd_ref[...].astype(BF16), preferred_element_type=F32), SLAB)

        @pl.when(i == n_used)
        def _():
            wait_gather(other)
            wait_scatter(other)


def _moe_plan(ids, t):
    n_slot = t * TOP_K_IN_GROUP
    n_blocks = n_slot // MOE_BLOCK + N_EXPERTS
    flat_e = ids.T.reshape(-1)
    order = jnp.argsort(flat_e).astype(I32)
    experts = jnp.arange(N_EXPERTS, dtype=I32)
    counts = jnp.sum(flat_e[:, None] == experts[None, :], axis=0, dtype=I32)
    n_blk = (counts + MOE_BLOCK - 1) // MOE_BLOCK
    blk_end = jnp.cumsum(n_blk)
    start = jnp.cumsum(counts) - counts
    blocks = jnp.arange(n_blocks, dtype=I32)
    blk_expert = jnp.minimum(jnp.sum(blk_end[None, :] <= blocks[:, None], axis=1, dtype=I32), N_EXPERTS - 1)
    row0 = (blocks - (blk_end - n_blk)[blk_expert]) * MOE_BLOCK
    rows = row0[:, None] + jnp.arange(MOE_BLOCK, dtype=I32)[None, :]
    is_slot = (rows >= 0) & (rows < counts[blk_expert][:, None])
    slot = order[jnp.clip(start[blk_expert][:, None] + rows, 0, n_slot - 1)]
    trash = n_slot + jnp.arange(MOE_BLOCK, dtype=I32)[None, :]
    src = jnp.where(is_slot, slot // TOP_K_IN_GROUP, 0).astype(I32)[:, None, :]
    dst = jnp.where(is_slot, slot, trash).astype(I32)[:, None, :]
    src = jnp.concatenate([src, jnp.zeros((1, 1, MOE_BLOCK), I32)])
    dst_prev = jnp.concatenate([trash[None].astype(I32), dst])
    return blk_expert, blk_end[-1:].astype(I32), src, dst_prev


def _experts(xn_slabs, ids, w_gate, w_up, w_down, layer):
    t = xn_slabs.shape[0] // SLAB
    blk_expert, n_used, src, dst_prev = _moe_plan(ids, t)
    n_blocks = blk_expert.shape[0]
    idx_spec = lambda f: pl.BlockSpec((None, 1, MOE_BLOCK), lambda i, be, nu: (f(i), 0, 0), memory_space=pltpu.SMEM)
    w_spec = lambda a: pl.BlockSpec((None, None) + a.shape[2:], lambda i, be, nu: (layer, be[i], 0, 0))
    buf = pltpu.VMEM((2, MOE_BLOCK * SLAB, LANES), F32)
    return pl.pallas_call(
        _expert_kernel,
        name="moe_experts",
        grid_spec=pltpu.PrefetchScalarGridSpec(
            num_scalar_prefetch=2,
            grid=(n_blocks,),
            in_specs=[idx_spec(lambda i: 0), idx_spec(lambda i: i + 1), idx_spec(lambda i: i),
                      pl.BlockSpec(memory_space=pl.ANY), w_spec(w_gate), w_spec(w_up), w_spec(w_down)],
            out_specs=pl.BlockSpec(memory_space=pl.ANY),
            scratch_shapes=[buf, buf, pltpu.SemaphoreType.DMA((2,)), pltpu.SemaphoreType.DMA((2,))]),
        out_shape=jax.ShapeDtypeStruct(((t * TOP_K_IN_GROUP + MOE_BLOCK) * SLAB, LANES), F32),
        compiler_params=_params("arbitrary"),
    )(blk_expert, n_used, src, src, dst_prev, xn_slabs, w_gate, w_up, w_down)


def _combine_kernel(h_ref, y_ref, gate_ref, o_ref):
    gate = gate_ref[...]
    rows = h_ref.shape[0]
    y0 = _load_slabs(y_ref, rows, TOP_K_IN_GROUP * SLAB)
    y1 = _load_slabs(y_ref, rows, TOP_K_IN_GROUP * SLAB, SLAB)
    o_ref[...] = h_ref[...] + gate[:, 0:1] * y0 + gate[:, 1:2] * y1


def _combine(h, y_slabs, gates):
    t = h.shape[0]
    tm = ROW_TILE
    return pl.pallas_call(
        _combine_kernel,
        name="moe_combine",
        grid=(t // tm,),
        in_specs=[pl.BlockSpec((tm, D_MODEL), lambda i: (i, 0)),
                  pl.BlockSpec((tm * TOP_K_IN_GROUP * SLAB, LANES), lambda i: (i, 0)),
                  pl.BlockSpec((tm, TOP_K_IN_GROUP), lambda i: (i, 0))],
        out_specs=pl.BlockSpec((tm, D_MODEL), lambda i: (i, 0)),
        out_shape=jax.ShapeDtypeStruct((t, D_MODEL), F32),
        compiler_params=_params("arbitrary"),
    )(h, y_slabs, gates.T)


def _moe(h, g, wg_group, bg_group, wg_expert, bg_expert, w_gate, w_up, w_down, layer):
    xn_slabs, ids, gates = _router(h, g, wg_group, bg_group, wg_expert, bg_expert)
    y_slabs = _experts(xn_slabs, ids, w_gate, w_up, w_down, layer)
    return _combine(h, y_slabs, gates)


def _even_mixer(h, g, w_in, cmp_pe, w_cmp1, w_cmp2, q_norm_g, k_norm_g, ret_norm_g, w_out, b, s):
    ret, nq, kcv, ks_aug, vs, kw_aug, vw, gates = _even_inproj(h, g, w_in, q_norm_g, k_norm_g, s)
    y_ret = _retention(ret, ret_norm_g, b, s)
    cmp_kv = _compress(kcv[:, :NSA_KV_W], kcv[:, NSA_KV_W:], cmp_pe, w_cmp1, w_cmp2, k_norm_g, b, s)
    y_nsa = _nsa_attention(nq, cmp_kv, ks_aug, vs, kw_aug, vw, gates, b, s)
    return _even_outproj(h, y_ret, y_nsa, w_out)


def kernel(x, mix_norm_g, ffn_norm_g, ev_w_in, ev_cmp_pe, ev_w_cmp1, ev_w_cmp2, ev_q_norm_g, ev_k_norm_g, ev_ret_norm_g, ev_w_out, od_w_in, od_conv_w, od_conv_b, od_w_out, moe_wg_group, moe_bg_group, moe_wg_expert, moe_bg_expert, moe_w_gate, moe_w_up, moe_w_down):
    b, s, d = x.shape
    h = x.reshape(b * s, d)
    for layer in range(mix_norm_g.shape[0]):
        i = layer // 2
        if layer % 2 == 0:
            h = _even_mixer(h, mix_norm_g[layer], ev_w_in[i], ev_cmp_pe[i], ev_w_cmp1[i], ev_w_cmp2[i],
                            ev_q_norm_g[i], ev_k_norm_g[i], ev_ret_norm_g[i], ev_w_out[i], b, s)
        else:
            h = _odd_mixer(h, mix_norm_g[layer], od_w_in[i], od_conv_w[i], od_conv_b[i], od_w_out[i], b, s)
        h = _moe(h, ffn_norm_g[layer], moe_wg_group[layer], moe_bg_group[layer], moe_wg_expert[layer],
                 moe_bg_expert[layer], moe_w_gate, moe_w_up, moe_w_down, layer)
    return h.reshape(b, s, d)
```

```python
import functools

import jax
import jax.numpy as jnp
import numpy as np
from jax import lax
from jax.experimental import pallas as pl
from jax.experimental.pallas import tpu as pltpu

F32 = jnp.float32
BF16 = jnp.bfloat16
I32 = jnp.int32
HIGHEST = lax.Precision.HIGHEST

D_MODEL = 1024
HEAD_DIM = 64
RET_HEADS = 8
NSA_HEADS = 8
NSA_KV_GROUPS = 2
NSA_GROUP_SIZE = NSA_HEADS // NSA_KV_GROUPS
RET_W = RET_HEADS * HEAD_DIM
NSA_W = NSA_HEADS * HEAD_DIM
NSA_KV_W = NSA_KV_GROUPS * HEAD_DIM
N_BRANCHES = 3
RET_CHUNK = 128
CMP_STRIDE = 16
CMP_BLOCK = 2 * CMP_STRIDE
CMP_HIDDEN = 128
SEL_BLOCK = 64
TOP_N = 8
WINDOW = 256
N_GROUPS = 4
EXPERTS_PER_GROUP = 8
N_EXPERTS = N_GROUPS * EXPERTS_PER_GROUP
TOP_K_IN_GROUP = 2
D_EXPERT = 256
RMS_EPS = 1e-6
NEG_INF = -1e30
ATTN_SCALE = HEAD_DIM ** -0.5

LANES = 128
VMEM_LIMIT = 48 * 1024 * 1024
ROW_TILE = 512
NSA_Q_TILE = 256
NSA_KV_CHUNK = 512
MOE_BLOCK = 256
ROUTER_ROWS = 40
SLAB = D_MODEL // LANES


def _params(*sem):
    return pltpu.CompilerParams(dimension_semantics=sem, vmem_limit_bytes=VMEM_LIMIT)


def _rms(x, g):
    return x * lax.rsqrt(jnp.mean(x * x, axis=-1, keepdims=True) + RMS_EPS) * g


def _group_rms(x, bd, gain):
    sq = x * x
    hi = sq.astype(BF16)
    lo = (sq - hi.astype(F32)).astype(BF16)
    ms = jnp.dot(hi, bd, preferred_element_type=F32) + jnp.dot(lo, bd, preferred_element_type=F32)
    return x * lax.rsqrt(ms + RMS_EPS) * gain


def _sigmoid(x):
    return 1.0 / (1.0 + jnp.exp(-x))


def _block_diag_mean(n):
    idx = np.arange(n) // HEAD_DIM
    return jnp.asarray((idx[:, None] == idx[None, :]).astype(np.float32) / HEAD_DIM, BF16)


def _even_inproj_kernel(x_ref, g_ref, wret_ref, wnq_ref, wkv_ref, wng_ref, qgain_ref, kgain_ref, bd512_ref, bd128_ref,
                        ret_ref, nq_ref, kcv_ref, ks_ref, vs_ref, kw_ref, vw_ref, gate_ref, *, seq_tiles):
    tm = x_ref.shape[0]
    xn = _rms(x_ref[...], g_ref[...]).astype(BF16)
    ret_ref[...] = jnp.dot(xn, wret_ref[...], preferred_element_type=F32).astype(BF16)
    nq = jnp.dot(xn, wnq_ref[...], preferred_element_type=F32)
    nq_ref[...] = _group_rms(nq, bd512_ref[...], qgain_ref[...]).astype(BF16)
    kv = jnp.dot(xn, wkv_ref[...], preferred_element_type=F32)
    w = NSA_KV_W
    kcv_ref[...] = kv[:, 0:2 * w].astype(BF16)
    ks = _group_rms(kv[:, 2 * w:3 * w], bd128_ref[...], kgain_ref[0:1, :])
    kw = _group_rms(kv[:, 4 * w:5 * w], bd128_ref[...], kgain_ref[1:2, :])
    vs, vw = kv[:, 3 * w:4 * w], kv[:, 5 * w:6 * w]
    pos = (pl.program_id(0) % seq_tiles) * tm + lax.broadcasted_iota(I32, (tm, HEAD_DIM), 0)
    lane = lax.broadcasted_iota(I32, (tm, HEAD_DIM), 1)
    blk = lax.shift_right_arithmetic(pos, int(np.log2(SEL_BLOCK)))
    feat_w = jnp.where(lane == 0, blk, jnp.where(lane == 1, pos & (SEL_BLOCK - 1), 0)).astype(F32)
    feat_s = feat_w + jnp.where(lane == blk + 2, 1.0, 0.0)
    ones_col = jnp.where(lane == 0, 1.0, 0.0)
    for g in range(NSA_KV_GROUPS):
        sl = slice(g * HEAD_DIM, (g + 1) * HEAD_DIM)
        ks_ref[g] = jnp.concatenate([ks[:, sl], feat_s], axis=-1).astype(BF16)
        kw_ref[g] = jnp.concatenate([kw[:, sl], feat_w], axis=-1).astype(BF16)
        vs_ref[g] = jnp.concatenate([vs[:, sl], ones_col], axis=-1).astype(BF16)
        vw_ref[g] = jnp.concatenate([vw[:, sl], ones_col], axis=-1).astype(BF16)
    ng = jnp.dot(xn, wng_ref[...], preferred_element_type=F32)
    for g in range(NSA_KV_GROUPS):
        gate_ref[g] = ng[:, g * LANES:(g + 1) * LANES]


def _even_inproj(h, g, w_in, q_norm_g, k_norm_g, s):
    t = h.shape[0]
    tm = ROW_TILE
    c_ret, c_nq, c_kv = 4 * RET_W, 4 * RET_W + NSA_W, 4 * RET_W + NSA_W + 6 * NSA_KV_W
    n_gate = NSA_GROUP_SIZE * N_BRANCHES
    wb = w_in.astype(BF16)
    wret, wnq, wkv = wb[:, :c_ret], wb[:, c_ret:c_nq], wb[:, c_nq:c_kv]
    wng = jnp.concatenate([jnp.pad(wb[:, c_kv + gi * n_gate:c_kv + (gi + 1) * n_gate], ((0, 0), (0, LANES - n_gate)))
                           for gi in range(NSA_KV_GROUPS)], axis=1)
    qgain = jnp.tile(q_norm_g, NSA_HEADS)[None, :]
    kgain = jnp.stack([jnp.tile(k_norm_g[1], NSA_KV_GROUPS), jnp.tile(k_norm_g[2], NSA_KV_GROUPS)])
    full = lambda a: pl.BlockSpec(a.shape, lambda i: (0,) * a.ndim)
    row = lambda n: pl.BlockSpec((tm, n), lambda i: (i, 0))
    grp = lambda n: pl.BlockSpec((NSA_KV_GROUPS, tm, n), lambda i: (0, i, 0))
    grp_shape = lambda n, dt: jax.ShapeDtypeStruct((NSA_KV_GROUPS, t, n), dt)
    bd512, bd128 = _block_diag_mean(NSA_W), _block_diag_mean(NSA_KV_W)
    gg = g[None, :]
    return pl.pallas_call(
        functools.partial(_even_inproj_kernel, seq_tiles=s // tm),
        name="even_inproj",
        grid=(t // tm,),
        in_specs=[row(D_MODEL), full(gg), full(wret), full(wnq), full(wkv), full(wng), full(qgain),
                  full(kgain), full(bd512), full(bd128)],
        out_specs=[row(c_ret), row(NSA_W), row(2 * NSA_KV_W), grp(LANES), grp(LANES), grp(LANES), grp(LANES), grp(LANES)],
        out_shape=[jax.ShapeDtypeStruct((t, c_ret), BF16), jax.ShapeDtypeStruct((t, NSA_W), BF16),
                   jax.ShapeDtypeStruct((t, 2 * NSA_KV_W), BF16), grp_shape(LANES, BF16), grp_shape(LANES, BF16),
                   grp_shape(LANES, BF16), grp_shape(LANES, BF16), grp_shape(LANES, F32)],
        compiler_params=_params("arbitrary"),
    )(h, gg, wret, wnq, wkv, wng, qgain, kgain, bd512, bd128)


def _retention_kernel(q_ref, k_ref, v_ref, rg_ref, decay_ref, qd_ref, kd_ref, cd_ref, gain_ref, o_ref):
    n_chunks = q_ref.shape[0] // RET_CHUNK
    gain = gain_ref[...]
    nt = (((1,), (1,)), ((), ()))
    tn = (((0,), (0,)), ((), ()))

    def body(c, states):
        r0 = pl.multiple_of(c * RET_CHUNK, RET_CHUNK)
        rows = pl.ds(r0, RET_CHUNK)
        q2, k2, v2 = q_ref[rows, :], k_ref[rows, :], v_ref[rows, :]
        rg2 = rg_ref[rows, :].astype(F32)
        outs, new_states = [], []
        for hh in range(2):
            lanes = slice(hh * HEAD_DIM, (hh + 1) * HEAD_DIM)
            q, v = q2[:, lanes], v2[:, lanes]
            kf = k2[:, lanes].astype(F32) * ATTN_SCALE
            state = states[hh]
            s = lax.dot_general(q, kf.astype(BF16), nt, preferred_element_type=F32) * decay_ref[hh]
            y = jnp.dot(s.astype(BF16), v, preferred_element_type=F32)
            qd = (q.astype(F32) * qd_ref[hh]).astype(BF16)
            y = y + jnp.dot(qd, state.astype(BF16), preferred_element_type=F32)
            kd = (kf * kd_ref[hh]).astype(BF16)
            new_states.append(cd_ref[hh] * state + lax.dot_general(kd, v, tn, preferred_element_type=F32))
            rg = rg2[:, lanes]
            outs.append(_rms(y, gain) * (rg * _sigmoid(rg)))
        o_ref[rows, :] = jnp.concatenate(outs, axis=-1).astype(BF16)
        return tuple(new_states)

    zero = jnp.zeros((HEAD_DIM, HEAD_DIM), F32)
    lax.fori_loop(0, n_chunks, body, (zero, zero), unroll=True)


def _retention_tables():
    h = np.arange(RET_HEADS, dtype=np.float64)
    log_g = np.log(1.0 - 2.0 ** (-5.0 - h))
    pos = np.arange(RET_CHUNK, dtype=np.float64)
    diff = pos[:, None] - pos[None, :]
    decay = np.where(diff >= 0, np.exp(log_g[:, None, None] * np.maximum(diff, 0.0)), 0.0)
    qd = np.exp(log_g[:, None] * (pos + 1.0))[:, :, None] * np.ones((1, 1, HEAD_DIM))
    kd = np.exp(log_g[:, None] * (RET_CHUNK - 1 - pos))[:, :, None] * np.ones((1, 1, HEAD_DIM))
    cd = np.exp(log_g * RET_CHUNK)[:, None, None] * np.ones((1, HEAD_DIM, HEAD_DIM))
    return [jnp.asarray(a, F32) for a in (decay, qd, kd, cd)]


def _retention(ret, ret_norm_g, b, s):
    t = b * s
    n_pairs = RET_HEADS // 2
    decay, qd, kd, cd = _retention_tables()
    col = lambda off: pl.BlockSpec((s, LANES), lambda bi, p: (bi, off * n_pairs + p))
    tab = lambda a: pl.BlockSpec((2,) + a.shape[1:], lambda bi, p: (p, 0, 0))
    gain = ret_norm_g[None, :]
    return pl.pallas_call(
        _retention_kernel,
        name="retention",
        grid=(b, n_pairs),
        in_specs=[col(0), col(1), col(2), col(3), tab(decay), tab(qd), tab(kd), tab(cd),
                  pl.BlockSpec(gain.shape, lambda bi, p: (0, 0))],
        out_specs=pl.BlockSpec((s, LANES), lambda bi, p: (bi, p)),
        out_shape=jax.ShapeDtypeStruct((t, RET_W), BF16),
        compiler_params=_params("arbitrary", "arbitrary"),
    )(ret, ret, ret, ret, decay, qd, kd, cd, gain)


def _compress_kernel(x_ref, pe_ref, w1_ref, w2_ref, kgain_ref, o_ref):
    is_key = pl.program_id(0) == 0
    half = CMP_STRIDE * HEAD_DIM
    x = x_ref[...]
    n = x.shape[0]
    a = jnp.dot(x, w1_ref[0:half, :], preferred_element_type=F32)
    bm = jnp.dot(x, w1_ref[half:2 * half, :], preferred_element_type=F32)
    pew = jnp.dot(pe_ref[...], w1_ref[...], preferred_element_type=F32)[0:1, :]
    pre = a + pltpu.roll(bm, n - 1, axis=0) + pew
    hid = 0.5 * pre * (1.0 + jnp.tanh(np.sqrt(2.0 / np.pi) * (pre + 0.044715 * pre * pre * pre)))
    out = jnp.dot(hid.astype(BF16), w2_ref[...], preferred_element_type=F32)
    normed = _rms(out, kgain_ref[...])
    o_ref[...] = jnp.where(is_key, normed, out).astype(BF16)


def _compress(kc, vc, cmp_pe, w_cmp1, w_cmp2, k_norm_g, b, s):
    n = s // CMP_STRIDE
    g = NSA_KV_GROUPS
    flat = CMP_STRIDE * HEAD_DIM

    def chunks(a):
        return a.reshape(b, n, CMP_STRIDE, g, HEAD_DIM).transpose(0, 3, 1, 2, 4).reshape(b, g, n, flat)

    x = jnp.stack([chunks(kc), chunks(vc)])
    pe = jnp.broadcast_to(cmp_pe.reshape(2, 1, 2 * flat), (2, 8, 2 * flat)).astype(BF16)
    w1, w2 = w_cmp1.astype(BF16), w_cmp2.astype(BF16)
    kgain = k_norm_g[0][None, :]
    return pl.pallas_call(
        _compress_kernel,
        name="nsa_compress",
        grid=(2, b, g),
        in_specs=[pl.BlockSpec((None, None, None, n, flat), lambda kv, bi, gi: (kv, bi, gi, 0, 0)),
                  pl.BlockSpec((None, 8, 2 * flat), lambda kv, bi, gi: (kv, 0, 0)),
                  pl.BlockSpec((None, 2 * flat, CMP_HIDDEN), lambda kv, bi, gi: (kv, 0, 0)),
                  pl.BlockSpec((None, CMP_HIDDEN, HEAD_DIM), lambda kv, bi, gi: (kv, 0, 0)),
                  pl.BlockSpec(kgain.shape, lambda kv, bi, gi: (0, 0))],
        out_specs=pl.BlockSpec((None, None, None, n, HEAD_DIM), lambda kv, bi, gi: (kv, bi, gi, 0, 0)),
        out_shape=jax.ShapeDtypeStruct((2, b, g, n, HEAD_DIM), BF16),
        compiler_params=_params("arbitrary", "arbitrary", "arbitrary"),
    )(x, pe, w1, w2, kgain)


def _nsa_kernel(slopes_ref, q_ref, kcmp_ref, vcmp_ref, ks_ref, vs_ref, kw_ref, vw_ref, gate_ref, ovl_ref, place_ref,
                o_ref, *, n_sel, n_cmp):
    r_heads = NSA_GROUP_SIZE
    tq = q_ref.shape[0]
    gi = pl.program_id(1)
    t0 = pl.program_id(2) * tq
    nt = (((1,), (1,)), ((), ()))
    tn = (((0,), (0,)), ((), ()))

    q = q_ref[...]
    lane = lax.broadcasted_iota(I32, (tq, HEAD_DIM), 1)
    qs, feats = [], []
    for r in range(r_heads):
        slope = slopes_ref[gi * r_heads + r]
        qs.append((q[:, r * HEAD_DIM:(r + 1) * HEAD_DIM].astype(F32) * ATTN_SCALE).astype(BF16))
        feats.append(jnp.where(lane == 0, slope * SEL_BLOCK, jnp.where(lane == 1, slope, 0.0)))

    def stack_q(extra):
        return jnp.concatenate([jnp.concatenate([qs[r], (feats[r] + extra).astype(BF16)], axis=-1)
                                for r in range(r_heads)], axis=0)

    q4 = stack_q(0.0)
    t_col = t0 + lax.broadcasted_iota(I32, (tq, 1), 0)

    def head_rows(x):
        return [x[r * tq:(r + 1) * tq] for r in range(r_heads)]

    n_pad = kcmp_ref.shape[0]
    c_idx = lax.broadcasted_iota(I32, (tq, n_pad), 1)
    mask_c = (t_col >= c_idx * CMP_STRIDE + (CMP_BLOCK - 1)) & (c_idx < n_cmp)
    s4 = lax.dot_general(q4, kcmp_ref[...], nt, preferred_element_type=F32)
    ps = []
    for s in head_rows(s4):
        s = jnp.where(mask_c, s, NEG_INF)
        p = jnp.where(mask_c, jnp.exp(s - jnp.max(s, axis=-1, keepdims=True)), 0.0)
        ps.append(p * (1.0 / jnp.maximum(jnp.sum(p, axis=-1, keepdims=True), 1e-30)))
    p4 = jnp.concatenate(ps, axis=0).astype(BF16)
    o_cmp = head_rows(jnp.dot(p4, vcmp_ref[...], preferred_element_type=F32))

    n_rows = ovl_ref.shape[0]
    imp4 = lax.dot_general(ovl_ref[...], p4, nt, preferred_element_type=F32)
    imp = imp4[:, 0:tq]
    for r in range(1, r_heads):
        imp = imp + imp4[:, r * tq:(r + 1) * tq]
    j_idx = lax.broadcasted_iota(I32, (n_rows, tq), 0)
    t_row = t0 + lax.broadcasted_iota(I32, (n_rows, tq), 1)
    q_blk = lax.shift_right_arithmetic(t_row, int(np.log2(SEL_BLOCK)))
    valid = (j_idx * SEL_BLOCK <= t_row) & (j_idx < n_sel)
    forced = (j_idx == 0) | (j_idx == q_blk) | (j_idx == q_blk - 1)
    val = jnp.where(j_idx < n_sel, jnp.where(forced, imp + 1e3, jnp.where(valid, imp, -1e3)), -2e3)
    rank = jnp.zeros((n_rows, tq), F32)
    for k in range(n_sel):
        vk = val[k:k + 1, :]
        rank = rank + jnp.where(vk > val, 1.0, jnp.where(vk == val, jnp.where(j_idx > k, 1.0, 0.0), 0.0))
    unselected = jnp.where(valid, jnp.where(rank < TOP_N, 0.0, NEG_INF), NEG_INF).astype(BF16)
    q4_sel = stack_q(lax.dot_general(unselected, place_ref[...], tn, preferred_element_type=F32))

    ck = NSA_KV_CHUNK

    def sel_step(c, carry, causal):
        c0 = pl.multiple_of(c * ck, ck)
        s4 = lax.dot_general(q4_sel, ks_ref[pl.ds(c0, ck), :], nt, preferred_element_type=F32)
        if causal:
            visible = t_col >= c0 + lax.broadcasted_iota(I32, (tq, ck), 1)
        ps, new = [], []
        for r, s in enumerate(head_rows(s4)):
            if causal:
                s = jnp.where(visible, s, NEG_INF)
            m_old, acc_old = carry[r]
            m_new = jnp.maximum(m_old, jnp.max(s, axis=-1, keepdims=True))
            ps.append(jnp.exp(s - m_new))
            new.append((m_new, jnp.exp(m_old - m_new) * acc_old))
        pv = jnp.dot(jnp.concatenate(ps, axis=0).astype(BF16), vs_ref[pl.ds(c0, ck), :], preferred_element_type=F32)
        return tuple((m, acc + o) for (m, acc), o in zip(new, head_rows(pv)))

    def normalise(acc):
        return acc[:, 0:HEAD_DIM] * (1.0 / acc[:, HEAD_DIM:HEAD_DIM + 1])

    init = tuple((jnp.full((tq, 1), NEG_INF, F32), jnp.zeros((tq, LANES), F32)) for _ in range(r_heads))
    c_last = t0 // ck
    carry = lax.fori_loop(0, c_last, lambda c, carry: sel_step(c, carry, False), init)
    o_sel = [normalise(acc) for (_, acc) in sel_step(c_last, carry, True)]

    n_win = WINDOW + tq
    w0 = pl.multiple_of(jnp.maximum(t0 - WINDOW, 0), tq)
    dist_w = t_col - (w0 + lax.broadcasted_iota(I32, (tq, n_win), 1))
    mask_w = (dist_w >= 0) & (dist_w < WINDOW)
    s4 = lax.dot_general(q4, kw_ref[pl.ds(w0, n_win), :], nt, preferred_element_type=F32)
    ps = []
    for s in head_rows(s4):
        s = jnp.where(mask_w, s, NEG_INF)
        ps.append(jnp.exp(s - jnp.max(s, axis=-1, keepdims=True)))
    pv = jnp.dot(jnp.concatenate(ps, axis=0).astype(BF16), vw_ref[pl.ds(w0, n_win), :], preferred_element_type=F32)
    o_win = [normalise(o) for o in head_rows(pv)]

    gate = _sigmoid(gate_ref[...])
    outs = []
    for r in range(r_heads):
        c = N_BRANCHES * r
        outs.append(gate[:, c:c + 1] * o_cmp[r] + gate[:, c + 1:c + 2] * o_sel[r] + gate[:, c + 2:c + 3] * o_win[r])
    o_ref[...] = jnp.concatenate(outs, axis=-1).astype(BF16)


def _nsa_attention(nq, cmp_kv, ks_aug, vs, kw_aug, vw, gates, b, s):
    t = b * s
    g, r = NSA_KV_GROUPS, NSA_GROUP_SIZE
    tq = NSA_Q_TILE
    n_q = s // tq
    n_sel = s // SEL_BLOCK
    n_pad = s // CMP_STRIDE
    n_cmp = n_pad - 1
    n_rows = -(-n_sel // 8) * 8
    assert 2 + n_sel <= HEAD_DIM and s % NSA_KV_CHUNK == 0 and s >= WINDOW + tq and NSA_KV_CHUNK % tq == 0
    slopes = jnp.asarray(2.0 ** (-8.0 * np.arange(1, NSA_HEADS + 1) / NSA_HEADS), F32)
    cmp_start = np.arange(n_pad) * CMP_STRIDE
    cmp_end = cmp_start + CMP_BLOCK - 1
    feat = np.zeros((n_pad, HEAD_DIM), np.float32)
    feat[:, 0], feat[:, 1] = cmp_end // SEL_BLOCK, cmp_end % SEL_BLOCK
    kcmp_aug = jnp.concatenate([cmp_kv[0], jnp.broadcast_to(jnp.asarray(feat, BF16), cmp_kv[0].shape)], axis=-1)
    sel_start = np.arange(n_rows) * SEL_BLOCK
    ovl = ((cmp_start[None, :] < sel_start[:, None] + SEL_BLOCK) & (cmp_start[None, :] + CMP_BLOCK > sel_start[:, None])
           & (np.arange(n_pad)[None, :] < n_cmp) & (np.arange(n_rows)[:, None] < n_sel))
    ovl = jnp.asarray(ovl.astype(np.float32), BF16)
    place = jnp.asarray((np.arange(n_rows)[:, None] + 2 == np.arange(HEAD_DIM)[None, :]).astype(np.float32), BF16)
    seq = lambda w: pl.BlockSpec((None, s, w), lambda bi, gi, qi: (gi, bi, 0))
    cmp = lambda w: pl.BlockSpec((None, None, n_pad, w), lambda bi, gi, qi: (bi, gi, 0, 0))
    return pl.pallas_call(
        functools.partial(_nsa_kernel, n_sel=n_sel, n_cmp=n_cmp),
        name="nsa_attention",
        grid=(b, g, n_q),
        in_specs=[pl.BlockSpec(memory_space=pltpu.SMEM),
                  pl.BlockSpec((tq, r * HEAD_DIM), lambda bi, gi, qi: (bi * n_q + qi, gi)),
                  cmp(2 * HEAD_DIM), cmp(HEAD_DIM), seq(LANES), seq(LANES), seq(LANES), seq(LANES),
                  pl.BlockSpec((None, tq, LANES), lambda bi, gi, qi: (gi, bi * n_q + qi, 0)),
                  pl.BlockSpec(ovl.shape, lambda bi, gi, qi: (0, 0)),
                  pl.BlockSpec(place.shape, lambda bi, gi, qi: (0, 0))],
        out_specs=pl.BlockSpec((tq, r * HEAD_DIM), lambda bi, gi, qi: (bi * n_q + qi, gi)),
        out_shape=jax.ShapeDtypeStruct((t, NSA_W), BF16),
        compiler_params=_params("arbitrary", "arbitrary", "arbitrary"),
    )(slopes, nq, kcmp_aug, cmp_kv[1], ks_aug, vs, kw_aug, vw, gates, ovl, place)


def _even_outproj_kernel(h_ref, a_ref, b_ref, wa_ref, wb_ref, o_ref):
    y = jnp.dot(a_ref[...], wa_ref[...], preferred_element_type=F32)
    y = y + jnp.dot(b_ref[...], wb_ref[...], preferred_element_type=F32)
    o_ref[...] = h_ref[...] + y


def _even_outproj(h, y_ret, y_nsa, w_out):
    t = h.shape[0]
    tm = ROW_TILE
    wb = w_out.astype(BF16)
    wa, wbt = wb[:RET_W], wb[RET_W:]
    row = lambda n: pl.BlockSpec((tm, n), lambda i: (i, 0))
    full = lambda a: pl.BlockSpec(a.shape, lambda i: (0, 0))
    return pl.pallas_call(
        _even_outproj_kernel,
        name="even_outproj",
        grid=(t // tm,),
        in_specs=[row(D_MODEL), row(RET_W), row(NSA_W), full(wa), full(wbt)],
        out_specs=row(D_MODEL),
        out_shape=jax.ShapeDtypeStruct((t, D_MODEL), F32),
        compiler_params=_params("arbitrary"),
    )(h, y_ret, y_nsa, wa, wbt)


def _odd_mixer_kernel(h_ref, g_ref, win_ref, cw_ref, cb_ref, wout_ref, o_ref, tail_ref):
    @pl.when(pl.program_id(1) == 0)
    def _():
        tail_ref[...] = jnp.zeros_like(tail_ref)

    h = h_ref[...]
    ts = h.shape[0]
    xn = _rms(h, g_ref[...]).astype(BF16)
    proj = jnp.dot(xn, win_ref[...], preferred_element_type=F32)
    gate_b, gate_c, hid = proj[:, 0:D_MODEL], proj[:, D_MODEL:2 * D_MODEL], proj[:, 2 * D_MODEL:3 * D_MODEL]
    u = gate_c * hid
    row = lax.broadcasted_iota(I32, (ts, D_MODEL), 0)
    tail = tail_ref[...]
    prev1, prev2 = tail[7:8, :], tail[6:7, :]
    u1 = jnp.where(row >= 1, pltpu.roll(u, 1, axis=0), prev1)
    u2 = jnp.where(row >= 2, pltpu.roll(u, 2, axis=0), jnp.where(row == 1, prev1, prev2))
    cw = cw_ref[...]
    y = cw[0:1, :] * u2 + cw[1:2, :] * u1 + cw[2:3, :] * u + cb_ref[...]
    tail_ref[...] = u[ts - 8:ts, :]
    z = (gate_b * y).astype(BF16)
    o_ref[...] = h + jnp.dot(z, wout_ref[...], preferred_element_type=F32)


def _odd_mixer(h, g, w_in, conv_w, conv_b, w_out, b, s):
    t = b * s
    ts = ROW_TILE
    n_s = s // ts
    win, wout = w_in.astype(BF16), w_out.astype(BF16)
    gg, cb = g[None, :], conv_b[None, :]
    cw = jnp.pad(conv_w, ((0, 8 - conv_w.shape[0]), (0, 0)))
    full = lambda a: pl.BlockSpec(a.shape, lambda bi, si: (0, 0))
    row = pl.BlockSpec((ts, D_MODEL), lambda bi, si: (bi * n_s + si, 0))
    return pl.pallas_call(
        _odd_mixer_kernel,
        name="odd_mixer",
        grid=(b, n_s),
        in_specs=[row, full(gg), full(win), full(cw), full(cb), full(wout)],
        out_specs=row,
        out_shape=jax.ShapeDtypeStruct((t, D_MODEL), F32),
        scratch_shapes=[pltpu.VMEM((8, D_MODEL), F32)],
        compiler_params=_params("arbitrary", "arbitrary"),
    )(h, gg, win, cw, cb, wout)


def _store_slabs(ref, x, stride):
    rows = x.shape[0]
    for j in range(SLAB):
        ref[pl.ds(j, rows, stride=stride), :] = x[:, j * LANES:(j + 1) * LANES]


def _load_slabs(ref, rows, stride, offset=0):
    return jnp.concatenate([ref[pl.ds(offset + j, rows, stride=stride), :] for j in range(SLAB)], axis=-1)


def _router_kernel(h_ref, g_ref, wr_ref, br_ref, xn_ref, ids_ref, gates_ref):
    xn = _rms(h_ref[...], g_ref[...])
    _store_slabs(xn_ref, xn, SLAB)
    logits = lax.dot_general(wr_ref[...], xn, (((1,), (1,)), ((), ())), precision=HIGHEST,
                             preferred_element_type=F32) + br_ref[...]
    row = lambda i: logits[i:i + 1, :]

    def softmax(xs):
        m = functools.reduce(jnp.maximum, xs)
        es = [jnp.exp(x - m) for x in xs]
        tot = functools.reduce(lambda a, c: a + c, es)
        return [e / tot for e in es]

    def argmax(ps):
        best_p, best_i = ps[0], jnp.zeros_like(ps[0], dtype=I32)
        for i in range(1, len(ps)):
            upd = ps[i] > best_p
            best_p = jnp.where(upd, ps[i], best_p)
            best_i = jnp.where(upd, i, best_i)
        return best_p, best_i

    grp_p, grp_i = argmax(softmax([row(i) for i in range(N_GROUPS)]))
    in_grp = []
    for e in range(EXPERTS_PER_GROUP):
        x = row(N_GROUPS + e)
        for gidx in range(1, N_GROUPS):
            x = jnp.where(grp_i == gidx, row(N_GROUPS + gidx * EXPERTS_PER_GROUP + e), x)
        in_grp.append(x)
    pe = softmax(in_grp)
    p1, i1 = argmax(pe)
    p2, i2 = argmax([jnp.where(i1 == e, -1.0, pe[e]) for e in range(EXPERTS_PER_GROUP)])
    tot = p1 + p2
    ids_ref[...] = jnp.concatenate([grp_i * EXPERTS_PER_GROUP + i1, grp_i * EXPERTS_PER_GROUP + i2], axis=0)
    gates_ref[...] = jnp.concatenate([grp_p * p1 / tot, grp_p * p2 / tot], axis=0)


def _router(h, g, wg_group, bg_group, wg_expert, bg_expert):
    t = h.shape[0]
    tm = ROW_TILE
    n_logit = N_GROUPS + N_EXPERTS
    wr = jnp.pad(jnp.concatenate([wg_group, wg_expert], axis=1).T, ((0, ROUTER_ROWS - n_logit), (0, 0)))
    br = jnp.pad(jnp.concatenate([bg_group, bg_expert]), (0, ROUTER_ROWS - n_logit))
    br = jnp.broadcast_to(br[:, None], (ROUTER_ROWS, tm))
    gg = g[None, :]
    full = lambda a: pl.BlockSpec(a.shape, lambda i: (0, 0))
    return pl.pallas_call(
        _router_kernel,
        name="moe_router",
        grid=(t // tm,),
        in_specs=[pl.BlockSpec((tm, D_MODEL), lambda i: (i, 0)), full(gg), full(wr), full(br)],
        out_specs=[pl.BlockSpec((tm * SLAB, LANES), lambda i: (i, 0)),
                   pl.BlockSpec((TOP_K_IN_GROUP, tm), lambda i: (0, i)),
                   pl.BlockSpec((TOP_K_IN_GROUP, tm), lambda i: (0, i))],
        out_shape=[jax.ShapeDtypeStruct((t * SLAB, LANES), F32), jax.ShapeDtypeStruct((TOP_K_IN_GROUP, t), I32),
                   jax.ShapeDtypeStruct((TOP_K_IN_GROUP, t), F32)],
        compiler_params=_params("arbitrary"),
    )(h, gg, wr, br)


def _expert_kernel(blk_expert_ref, n_used_ref, src0_ref, src_next_ref, dst_prev_ref, x_hbm, wg_ref, wu_ref, wd_ref,
                   y_hbm, xbuf, ybuf, sem_in, sem_out):
    del blk_expert_ref
    i = pl.program_id(0)
    n_used = n_used_ref[0]
    slot = i % 2
    other = 1 - slot

    def row_copy(hbm, hbm_row, buf, r, sem, to_hbm):
        h = hbm.at[pl.ds(pl.multiple_of(hbm_row * SLAB, SLAB), SLAB)]
        v = buf.at[pl.ds(r * SLAB, SLAB)]
        return pltpu.make_async_copy(v, h, sem) if to_hbm else pltpu.make_async_copy(h, v, sem)

    def start_gather(idx_ref, s):
        for r in range(MOE_BLOCK):
            row_copy(x_hbm, idx_ref[0, r], xbuf.at[s], r, sem_in.at[s], False).start()

    def wait_gather(s):
        for r in range(MOE_BLOCK):
            row_copy(x_hbm, 0, xbuf.at[s], r, sem_in.at[s], False).wait()

    def wait_scatter(s):
        for r in range(MOE_BLOCK):
            row_copy(y_hbm, 0, ybuf.at[s], r, sem_out.at[s], True).wait()

    @pl.when(i == 0)
    def _():
        ybuf[...] = jnp.zeros_like(ybuf)
        start_gather(src0_ref, 0)

    @pl.when(i <= n_used)
    def _():
        wait_gather(slot)

        @pl.when(i >= 1)
        def _():
            wait_scatter(slot)

        for r in range(MOE_BLOCK):
            row_copy(x_hbm, src_next_ref[0, r], xbuf.at[other], r, sem_in.at[other], False).start(priority=0)
            row_copy(y_hbm, dst_prev_ref[0, r], ybuf.at[other], r, sem_out.at[other], True).start(priority=1)
        x = _load_slabs(xbuf.at[slot], MOE_BLOCK, SLAB).astype(BF16)
        a = jnp.dot(x, wg_ref[...].astype(BF16), preferred_element_type=F32)
        u = jnp.dot(x, wu_ref[...].astype(BF16), preferred_element_type=F32)
        hid = (a * _sigmoid(a) * u).astype(BF16)
        _store_slabs(ybuf.at[slot], jnp.dot(hid, wd_ref[...].astype(BF16), preferred_element_type=F32), SLAB)

        @pl.when(i == n_used)
        def _():
            wait_gather(other)
            wait_scatter(other)


def _moe_plan(ids, t):
    n_slot = t * TOP_K_IN_GROUP
    n_blocks = n_slot // MOE_BLOCK + N_EXPERTS
    flat_e = ids.T.reshape(-1)
    order = jnp.argsort(flat_e).astype(I32)
    experts = jnp.arange(N_EXPERTS, dtype=I32)
    counts = jnp.sum(flat_e[:, None] == experts[None, :], axis=0, dtype=I32)
    n_blk = (counts + MOE_BLOCK - 1) // MOE_BLOCK
    blk_end = jnp.cumsum(n_blk)
    start = jnp.cumsum(counts) - counts
    blocks = jnp.arange(n_blocks, dtype=I32)
    blk_expert = jnp.minimum(jnp.sum(blk_end[None, :] <= blocks[:, None], axis=1, dtype=I32), N_EXPERTS - 1)
    row0 = (blocks - (blk_end - n_blk)[blk_expert]) * MOE_BLOCK
    rows = row0[:, None] + jnp.arange(MOE_BLOCK, dtype=I32)[None, :]
    is_slot = (rows >= 0) & (rows < counts[blk_expert][:, None])
    slot = order[jnp.clip(start[blk_expert][:, None] + rows, 0, n_slot - 1)]
    trash = n_slot + jnp.arange(MOE_BLOCK, dtype=I32)[None, :]
    src = jnp.where(is_slot, slot // TOP_K_IN_GROUP, 0).astype(I32)[:, None, :]
    dst = jnp.where(is_slot, slot, trash).astype(I32)[:, None, :]
    src = jnp.concatenate([src, jnp.zeros((1, 1, MOE_BLOCK), I32)])
    dst_prev = jnp.concatenate([trash[None].astype(I32), dst])
    return blk_expert, blk_end[-1:].astype(I32), src, dst_prev


def _experts(xn_slabs, ids, w_gate, w_up, w_down, layer):
    t = xn_slabs.shape[0] // SLAB
    blk_expert, n_used, src, dst_prev = _moe_plan(ids, t)
    n_blocks = blk_expert.shape[0]
    idx_spec = lambda f: pl.BlockSpec((None, 1, MOE_BLOCK), lambda i, be, nu: (f(i), 0, 0), memory_space=pltpu.SMEM)
    w_spec = lambda a: pl.BlockSpec((None, None) + a.shape[2:], lambda i, be, nu: (layer, be[i], 0, 0))
    buf = pltpu.VMEM((2, MOE_BLOCK * SLAB, LANES), F32)
    return pl.pallas_call(
        _expert_kernel,
        name="moe_experts",
        grid_spec=pltpu.PrefetchScalarGridSpec(
            num_scalar_prefetch=2,
            grid=(n_blocks,),
            in_specs=[idx_spec(lambda i: 0), idx_spec(lambda i: i + 1), idx_spec(lambda i: i),
                      pl.BlockSpec(memory_space=pl.ANY), w_spec(w_gate), w_spec(w_up), w_spec(w_down)],
            out_specs=pl.BlockSpec(memory_space=pl.ANY),
            scratch_shapes=[buf, buf, pltpu.SemaphoreType.DMA((2,)), pltpu.SemaphoreType.DMA((2,))]),
        out_shape=jax.ShapeDtypeStruct(((t * TOP_K_IN_GROUP + MOE_BLOCK) * SLAB, LANES), F32),
        compiler_params=_params("arbitrary"),
    )(blk_expert, n_used, src, src, dst_prev, xn_slabs, w_gate, w_up, w_down)


def _combine_kernel(h_ref, y_ref, gate_ref, o_ref):
    gate = gate_ref[...]
    rows = h_ref.shape[0]
    y0 = _load_slabs(y_ref, rows, TOP_K_IN_GROUP * SLAB)
    y1 = _load_slabs(y_ref, rows, TOP_K_IN_GROUP * SLAB, SLAB)
    o_ref[...] = h_ref[...] + gate[:, 0:1] * y0 + gate[:, 1:2] * y1


def _combine(h, y_slabs, gates):
    t = h.shape[0]
    tm = ROW_TILE
    return pl.pallas_call(
        _combine_kernel,
        name="moe_combine",
        grid=(t // tm,),
        in_specs=[pl.BlockSpec((tm, D_MODEL), lambda i: (i, 0)),
                  pl.BlockSpec((tm * TOP_K_IN_GROUP * SLAB, LANES), lambda i: (i, 0)),
                  pl.BlockSpec((tm, TOP_K_IN_GROUP), lambda i: (i, 0))],
        out_specs=pl.BlockSpec((tm, D_MODEL), lambda i: (i, 0)),
        out_shape=jax.ShapeDtypeStruct((t, D_MODEL), F32),
        compiler_params=_params("arbitrary"),
    )(h, y_slabs, gates.T)


def _moe(h, g, wg_group, bg_group, wg_expert, bg_expert, w_gate, w_up, w_down, layer):
    xn_slabs, ids, gates = _router(h, g, wg_group, bg_group, wg_expert, bg_expert)
    y_slabs = _experts(xn_slabs, ids, w_gate, w_up, w_down, layer)
    return _combine(h, y_slabs, gates)


def _even_mixer(h, g, w_in, cmp_pe, w_cmp1, w_cmp2, q_norm_g, k_norm_g, ret_norm_g, w_out, b, s):
    ret, nq, kcv, ks_aug, vs, kw_aug, vw, gates = _even_inproj(h, g, w_in, q_norm_g, k_norm_g, s)
    y_ret = _retention(ret, ret_norm_g, b, s)
    cmp_kv = _compress(kcv[:, :NSA_KV_W], kcv[:, NSA_KV_W:], cmp_pe, w_cmp1, w_cmp2, k_norm_g, b, s)
    y_nsa = _nsa_attention(nq, cmp_kv, ks_aug, vs, kw_aug, vw, gates, b, s)
    return _even_outproj(h, y_ret, y_nsa, w_out)


def kernel(x, mix_norm_g, ffn_norm_g, ev_w_in, ev_cmp_pe, ev_w_cmp1, ev_w_cmp2, ev_q_norm_g, ev_k_norm_g, ev_ret_norm_g, ev_w_out, od_w_in, od_conv_w, od_conv_b, od_w_out, moe_wg_group, moe_bg_group, moe_wg_expert, moe_bg_expert, moe_w_gate, moe_w_up, moe_w_down):
    b, s, d = x.shape
    h = x.reshape(b * s, d)
    for layer in range(mix_norm_g.shape[0]):
        i = layer // 2
        if layer % 2 == 0:
            h = _even_mixer(h, mix_norm_g[layer], ev_w_in[i], ev_cmp_pe[i], ev_w_cmp1[i], ev_w_cmp2[i],
                            ev_q_norm_g[i], ev_k_norm_g[i], ev_ret_norm_g[i], ev_w_out[i], b, s)
        else:
            h = _odd_mixer(h, mix_norm_g[layer], od_w_in[i], od_conv_w[i], od_conv_b[i], od_w_out[i], b, s)
        h = _moe(h, ffn_norm_g[layer], moe_wg_group[layer], moe_bg_group[layer], moe_wg_expert[layer],
                 moe_bg_expert[layer], moe_w_gate, moe_w_up, moe_w_down, layer)
    return h.reshape(b, s, d)
```

```python
import functools

import jax
import jax.numpy as jnp
import numpy as np
from jax import lax
from jax.experimental import pallas as pl
from jax.experimental.pallas import tpu as pltpu

F32 = jnp.float32
BF16 = jnp.bfloat16
I32 = jnp.int32
HIGHEST = lax.Precision.HIGHEST

D_MODEL = 1024
HEAD_DIM = 64
RET_HEADS = 8
NSA_HEADS = 8
NSA_KV_GROUPS = 2
NSA_GROUP_SIZE = NSA_HEADS // NSA_KV_GROUPS
RET_W = RET_HEADS * HEAD_DIM
NSA_W = NSA_HEADS * HEAD_DIM
NSA_KV_W = NSA_KV_GROUPS * HEAD_DIM
N_BRANCHES = 3
RET_CHUNK = 128
CMP_STRIDE = 16
CMP_BLOCK = 2 * CMP_STRIDE
CMP_HIDDEN = 128
SEL_BLOCK = 64
TOP_N = 8
WINDOW = 256
N_GROUPS = 4
EXPERTS_PER_GROUP = 8
N_EXPERTS = N_GROUPS * EXPERTS_PER_GROUP
TOP_K_IN_GROUP = 2
D_EXPERT = 256
RMS_EPS = 1e-6
NEG_INF = -1e30
ATTN_SCALE = HEAD_DIM ** -0.5

LANES = 128
VMEM_LIMIT = 48 * 1024 * 1024
ROW_TILE = 512
NSA_Q_TILE = 256
NSA_KV_CHUNK = 512
MOE_BLOCK = 256
ROUTER_ROWS = 40
SLAB = D_MODEL // LANES
PLAN_SLOTS = 4


def _params(*sem):
    return pltpu.CompilerParams(dimension_semantics=sem, vmem_limit_bytes=VMEM_LIMIT)


def _rms(x, g):
    return x * lax.rsqrt(jnp.mean(x * x, axis=-1, keepdims=True) + RMS_EPS) * g


def _group_rms(x, bd, gain):
    sq = x * x
    hi = sq.astype(BF16)
    lo = (sq - hi.astype(F32)).astype(BF16)
    ms = jnp.dot(hi, bd, preferred_element_type=F32) + jnp.dot(lo, bd, preferred_element_type=F32)
    return x * lax.rsqrt(ms + RMS_EPS) * gain


def _sigmoid(x):
    return 1.0 / (1.0 + jnp.exp(-x))


def _block_diag_mean(n):
    idx = np.arange(n) // HEAD_DIM
    return jnp.asarray((idx[:, None] == idx[None, :]).astype(np.float32) / HEAD_DIM, BF16)


def _even_inproj_kernel(x_ref, g_ref, wret_ref, wnq_ref, wkv_ref, wng_ref, qgain_ref, kgain_ref, bd512_ref, bd128_ref,
                        ret_ref, nq_ref, kcv_ref, ks_ref, vs_ref, kw_ref, vw_ref, gate_ref, *, seq_tiles):
    tm = x_ref.shape[0]
    xn = _rms(x_ref[...], g_ref[...]).astype(BF16)
    ret_ref[...] = jnp.dot(xn, wret_ref[...], preferred_element_type=F32).astype(BF16)
    nq = jnp.dot(xn, wnq_ref[...], preferred_element_type=F32)
    nq_ref[...] = _group_rms(nq, bd512_ref[...], qgain_ref[...]).astype(BF16)
    kv = jnp.dot(xn, wkv_ref[...], preferred_element_type=F32)
    w = NSA_KV_W
    kcv_ref[...] = kv[:, 0:2 * w].astype(BF16)
    ks = _group_rms(kv[:, 2 * w:3 * w], bd128_ref[...], kgain_ref[0:1, :])
    kw = _group_rms(kv[:, 4 * w:5 * w], bd128_ref[...], kgain_ref[1:2, :])
    vs, vw = kv[:, 3 * w:4 * w], kv[:, 5 * w:6 * w]
    pos = (pl.program_id(0) % seq_tiles) * tm + lax.broadcasted_iota(I32, (tm, HEAD_DIM), 0)
    lane = lax.broadcasted_iota(I32, (tm, HEAD_DIM), 1)
    blk = lax.shift_right_arithmetic(pos, int(np.log2(SEL_BLOCK)))
    feat_w = jnp.where(lane == 0, blk, jnp.where(lane == 1, pos & (SEL_BLOCK - 1), 0)).astype(F32)
    feat_s = feat_w + jnp.where(lane == blk + 2, 1.0, 0.0)
    ones_col = jnp.where(lane == 0, 1.0, 0.0)
    for g in range(NSA_KV_GROUPS):
        sl = slice(g * HEAD_DIM, (g + 1) * HEAD_DIM)
        ks_ref[g] = jnp.concatenate([ks[:, sl], feat_s], axis=-1).astype(BF16)
        kw_ref[g] = jnp.concatenate([kw[:, sl], feat_w], axis=-1).astype(BF16)
        vs_ref[g] = jnp.concatenate([vs[:, sl], ones_col], axis=-1).astype(BF16)
        vw_ref[g] = jnp.concatenate([vw[:, sl], ones_col], axis=-1).astype(BF16)
    ng = jnp.dot(xn, wng_ref[...], preferred_element_type=F32)
    for g in range(NSA_KV_GROUPS):
        gate_ref[g] = ng[:, g * LANES:(g + 1) * LANES]


def _even_inproj(h, g, w_in, q_norm_g, k_norm_g, s):
    t = h.shape[0]
    tm = ROW_TILE
    c_ret, c_nq, c_kv = 4 * RET_W, 4 * RET_W + NSA_W, 4 * RET_W + NSA_W + 6 * NSA_KV_W
    n_gate = NSA_GROUP_SIZE * N_BRANCHES
    wb = w_in.astype(BF16)
    wret, wnq, wkv = wb[:, :c_ret], wb[:, c_ret:c_nq], wb[:, c_nq:c_kv]
    wng = jnp.concatenate([jnp.pad(wb[:, c_kv + gi * n_gate:c_kv + (gi + 1) * n_gate], ((0, 0), (0, LANES - n_gate)))
                           for gi in range(NSA_KV_GROUPS)], axis=1)
    qgain = jnp.tile(q_norm_g, NSA_HEADS)[None, :]
    kgain = jnp.stack([jnp.tile(k_norm_g[1], NSA_KV_GROUPS), jnp.tile(k_norm_g[2], NSA_KV_GROUPS)])
    full = lambda a: pl.BlockSpec(a.shape, lambda i: (0,) * a.ndim)
    row = lambda n: pl.BlockSpec((tm, n), lambda i: (i, 0))
    grp = lambda n: pl.BlockSpec((NSA_KV_GROUPS, tm, n), lambda i: (0, i, 0))
    grp_shape = lambda n, dt: jax.ShapeDtypeStruct((NSA_KV_GROUPS, t, n), dt)
    bd512, bd128 = _block_diag_mean(NSA_W), _block_diag_mean(NSA_KV_W)
    gg = g[None, :]
    return pl.pallas_call(
        functools.partial(_even_inproj_kernel, seq_tiles=s // tm),
        name="even_inproj",
        grid=(t // tm,),
        in_specs=[row(D_MODEL), full(gg), full(wret), full(wnq), full(wkv), full(wng), full(qgain),
                  full(kgain), full(bd512), full(bd128)],
        out_specs=[row(c_ret), row(NSA_W), row(2 * NSA_KV_W), grp(LANES), grp(LANES), grp(LANES), grp(LANES), grp(LANES)],
        out_shape=[jax.ShapeDtypeStruct((t, c_ret), BF16), jax.ShapeDtypeStruct((t, NSA_W), BF16),
                   jax.ShapeDtypeStruct((t, 2 * NSA_KV_W), BF16), grp_shape(LANES, BF16), grp_shape(LANES, BF16),
                   grp_shape(LANES, BF16), grp_shape(LANES, BF16), grp_shape(LANES, F32)],
        compiler_params=_params("arbitrary"),
    )(h, gg, wret, wnq, wkv, wng, qgain, kgain, bd512, bd128)


def _retention_kernel(q_ref, k_ref, v_ref, rg_ref, decay_ref, qd_ref, kd_ref, cd_ref, gain_ref, o_ref):
    n_chunks = q_ref.shape[0] // RET_CHUNK
    gain = gain_ref[...]
    nt = (((1,), (1,)), ((), ()))
    tn = (((0,), (0,)), ((), ()))

    def body(c, states):
        r0 = pl.multiple_of(c * RET_CHUNK, RET_CHUNK)
        rows = pl.ds(r0, RET_CHUNK)
        q2, k2, v2 = q_ref[rows, :], k_ref[rows, :], v_ref[rows, :]
        rg2 = rg_ref[rows, :].astype(F32)
        outs, new_states = [], []
        for hh in range(2):
            lanes = slice(hh * HEAD_DIM, (hh + 1) * HEAD_DIM)
            q, v = q2[:, lanes], v2[:, lanes]
            kf = k2[:, lanes].astype(F32) * ATTN_SCALE
            state = states[hh]
            s = lax.dot_general(q, kf.astype(BF16), nt, preferred_element_type=F32) * decay_ref[hh]
            y = jnp.dot(s.astype(BF16), v, preferred_element_type=F32)
            qd = (q.astype(F32) * qd_ref[hh]).astype(BF16)
            y = y + jnp.dot(qd, state.astype(BF16), preferred_element_type=F32)
            kd = (kf * kd_ref[hh]).astype(BF16)
            new_states.append(cd_ref[hh] * state + lax.dot_general(kd, v, tn, preferred_element_type=F32))
            rg = rg2[:, lanes]
            outs.append(_rms(y, gain) * (rg * _sigmoid(rg)))
        o_ref[rows, :] = jnp.concatenate(outs, axis=-1).astype(BF16)
        return tuple(new_states)

    zero = jnp.zeros((HEAD_DIM, HEAD_DIM), F32)
    lax.fori_loop(0, n_chunks, body, (zero, zero), unroll=True)


def _retention_tables():
    h = np.arange(RET_HEADS, dtype=np.float64)
    log_g = np.log(1.0 - 2.0 ** (-5.0 - h))
    pos = np.arange(RET_CHUNK, dtype=np.float64)
    diff = pos[:, None] - pos[None, :]
    decay = np.where(diff >= 0, np.exp(log_g[:, None, None] * np.maximum(diff, 0.0)), 0.0)
    qd = np.exp(log_g[:, None] * (pos + 1.0))[:, :, None] * np.ones((1, 1, HEAD_DIM))
    kd = np.exp(log_g[:, None] * (RET_CHUNK - 1 - pos))[:, :, None] * np.ones((1, 1, HEAD_DIM))
    cd = np.exp(log_g * RET_CHUNK)[:, None, None] * np.ones((1, HEAD_DIM, HEAD_DIM))
    return [jnp.asarray(a, F32) for a in (decay, qd, kd, cd)]


def _retention(ret, ret_norm_g, b, s):
    t = b * s
    n_pairs = RET_HEADS // 2
    decay, qd, kd, cd = _retention_tables()
    col = lambda off: pl.BlockSpec((s, LANES), lambda bi, p: (bi, off * n_pairs + p))
    tab = lambda a: pl.BlockSpec((2,) + a.shape[1:], lambda bi, p: (p, 0, 0))
    gain = ret_norm_g[None, :]
    return pl.pallas_call(
        _retention_kernel,
        name="retention",
        grid=(b, n_pairs),
        in_specs=[col(0), col(1), col(2), col(3), tab(decay), tab(qd), tab(kd), tab(cd),
                  pl.BlockSpec(gain.shape, lambda bi, p: (0, 0))],
        out_specs=pl.BlockSpec((s, LANES), lambda bi, p: (bi, p)),
        out_shape=jax.ShapeDtypeStruct((t, RET_W), BF16),
        compiler_params=_params("arbitrary", "arbitrary"),
    )(ret, ret, ret, ret, decay, qd, kd, cd, gain)


def _compress_kernel(x_ref, pe_ref, w1_ref, w2_ref, kgain_ref, o_ref):
    is_key = pl.program_id(0) == 0
    half = CMP_STRIDE * HEAD_DIM
    x = x_ref[...]
    n = x.shape[0]
    a = jnp.dot(x, w1_ref[0:half, :], preferred_element_type=F32)
    bm = jnp.dot(x, w1_ref[half:2 * half, :], preferred_element_type=F32)
    pew = jnp.dot(pe_ref[...], w1_ref[...], preferred_element_type=F32)[0:1, :]
    pre = a + pltpu.roll(bm, n - 1, axis=0) + pew
    hid = 0.5 * pre * (1.0 + jnp.tanh(np.sqrt(2.0 / np.pi) * (pre + 0.044715 * pre * pre * pre)))
    out = jnp.dot(hid.astype(BF16), w2_ref[...], preferred_element_type=F32)
    normed = _rms(out, kgain_ref[...])
    o_ref[...] = jnp.where(is_key, normed, out).astype(BF16)


def _compress(kc, vc, cmp_pe, w_cmp1, w_cmp2, k_norm_g, b, s):
    n = s // CMP_STRIDE
    g = NSA_KV_GROUPS
    flat = CMP_STRIDE * HEAD_DIM

    def chunks(a):
        return a.reshape(b, n, CMP_STRIDE, g, HEAD_DIM).transpose(0, 3, 1, 2, 4).reshape(b, g, n, flat)

    x = jnp.stack([chunks(kc), chunks(vc)])
    pe = jnp.broadcast_to(cmp_pe.reshape(2, 1, 2 * flat), (2, 8, 2 * flat)).astype(BF16)
    w1, w2 = w_cmp1.astype(BF16), w_cmp2.astype(BF16)
    kgain = k_norm_g[0][None, :]
    return pl.pallas_call(
        _compress_kernel,
        name="nsa_compress",
        grid=(2, b, g),
        in_specs=[pl.BlockSpec((None, None, None, n, flat), lambda kv, bi, gi: (kv, bi, gi, 0, 0)),
                  pl.BlockSpec((None, 8, 2 * flat), lambda kv, bi, gi: (kv, 0, 0)),
                  pl.BlockSpec((None, 2 * flat, CMP_HIDDEN), lambda kv, bi, gi: (kv, 0, 0)),
                  pl.BlockSpec((None, CMP_HIDDEN, HEAD_DIM), lambda kv, bi, gi: (kv, 0, 0)),
                  pl.BlockSpec(kgain.shape, lambda kv, bi, gi: (0, 0))],
        out_specs=pl.BlockSpec((None, None, None, n, HEAD_DIM), lambda kv, bi, gi: (kv, bi, gi, 0, 0)),
        out_shape=jax.ShapeDtypeStruct((2, b, g, n, HEAD_DIM), BF16),
        compiler_params=_params("arbitrary", "arbitrary", "arbitrary"),
    )(x, pe, w1, w2, kgain)


def _nsa_kernel(slopes_ref, q_ref, kcmp_ref, vcmp_ref, ks_ref, vs_ref, kw_ref, vw_ref, gate_ref, ovl_ref, place_ref,
                o_ref, *, n_sel, n_cmp):
    r_heads = NSA_GROUP_SIZE
    tq = q_ref.shape[0]
    gi = pl.program_id(1)
    t0 = pl.program_id(2) * tq
    nt = (((1,), (1,)), ((), ()))
    tn = (((0,), (0,)), ((), ()))

    q = q_ref[...]
    lane = lax.broadcasted_iota(I32, (tq, HEAD_DIM), 1)
    qs, feats = [], []
    for r in range(r_heads):
        slope = slopes_ref[gi * r_heads + r]
        qs.append((q[:, r * HEAD_DIM:(r + 1) * HEAD_DIM].astype(F32) * ATTN_SCALE).astype(BF16))
        feats.append(jnp.where(lane == 0, slope * SEL_BLOCK, jnp.where(lane == 1, slope, 0.0)))

    def stack_q(extra):
        return jnp.concatenate([jnp.concatenate([qs[r], (feats[r] + extra).astype(BF16)], axis=-1)
                                for r in range(r_heads)], axis=0)

    q4 = stack_q(0.0)
    t_col = t0 + lax.broadcasted_iota(I32, (tq, 1), 0)

    def head_rows(x):
        return [x[r * tq:(r + 1) * tq] for r in range(r_heads)]

    n_pad = kcmp_ref.shape[0]
    c_idx = lax.broadcasted_iota(I32, (tq, n_pad), 1)
    mask_c = (t_col >= c_idx * CMP_STRIDE + (CMP_BLOCK - 1)) & (c_idx < n_cmp)
    s4 = lax.dot_general(q4, kcmp_ref[...], nt, preferred_element_type=F32)
    ps = []
    for s in head_rows(s4):
        s = jnp.where(mask_c, s, NEG_INF)
        p = jnp.where(mask_c, jnp.exp(s - jnp.max(s, axis=-1, keepdims=True)), 0.0)
        ps.append(p * (1.0 / jnp.maximum(jnp.sum(p, axis=-1, keepdims=True), 1e-30)))
    p4 = jnp.concatenate(ps, axis=0).astype(BF16)
    o_cmp = head_rows(jnp.dot(p4, vcmp_ref[...], preferred_element_type=F32))

    n_rows = ovl_ref.shape[0]
    imp4 = lax.dot_general(ovl_ref[...], p4, nt, preferred_element_type=F32)
    imp = imp4[:, 0:tq]
    for r in range(1, r_heads):
        imp = imp + imp4[:, r * tq:(r + 1) * tq]
    j_idx = lax.broadcasted_iota(I32, (n_rows, tq), 0)
    t_row = t0 + lax.broadcasted_iota(I32, (n_rows, tq), 1)
    q_blk = lax.shift_right_arithmetic(t_row, int(np.log2(SEL_BLOCK)))
    valid = (j_idx * SEL_BLOCK <= t_row) & (j_idx < n_sel)
    forced = (j_idx == 0) | (j_idx == q_blk) | (j_idx == q_blk - 1)
    val = jnp.where(j_idx < n_sel, jnp.where(forced, imp + 1e3, jnp.where(valid, imp, -1e3)), -2e3)
    rank = jnp.zeros((n_rows, tq), F32)
    for k in range(n_sel):
        vk = val[k:k + 1, :]
        rank = rank + jnp.where(vk > val, 1.0, jnp.where(vk == val, jnp.where(j_idx > k, 1.0, 0.0), 0.0))
    unselected = jnp.where(valid, jnp.where(rank < TOP_N, 0.0, NEG_INF), NEG_INF).astype(BF16)
    q4_sel = stack_q(lax.dot_general(unselected, place_ref[...], tn, preferred_element_type=F32))

    ck = NSA_KV_CHUNK

    def sel_step(c, carry, causal):
        c0 = pl.multiple_of(c * ck, ck)
        s4 = lax.dot_general(q4_sel, ks_ref[pl.ds(c0, ck), :], nt, preferred_element_type=F32)
        if causal:
            visible = t_col >= c0 + lax.broadcasted_iota(I32, (tq, ck), 1)
        ps, new = [], []
        for r, s in enumerate(head_rows(s4)):
            if causal:
                s = jnp.where(visible, s, NEG_INF)
            m_old, acc_old = carry[r]
            m_new = jnp.maximum(m_old, jnp.max(s, axis=-1, keepdims=True))
            ps.append(jnp.exp(s - m_new))
            new.append((m_new, jnp.exp(m_old - m_new) * acc_old))
        pv = jnp.dot(jnp.concatenate(ps, axis=0).astype(BF16), vs_ref[pl.ds(c0, ck), :], preferred_element_type=F32)
        return tuple((m, acc + o) for (m, acc), o in zip(new, head_rows(pv)))

    def normalise(acc):
        return acc[:, 0:HEAD_DIM] * (1.0 / acc[:, HEAD_DIM:HEAD_DIM + 1])

    init = tuple((jnp.full((tq, 1), NEG_INF, F32), jnp.zeros((tq, LANES), F32)) for _ in range(r_heads))
    c_last = t0 // ck
    carry = lax.fori_loop(0, c_last, lambda c, carry: sel_step(c, carry, False), init)
    o_sel = [normalise(acc) for (_, acc) in sel_step(c_last, carry, True)]

    n_win = WINDOW + tq
    w0 = pl.multiple_of(jnp.maximum(t0 - WINDOW, 0), tq)
    dist_w = t_col - (w0 + lax.broadcasted_iota(I32, (tq, n_win), 1))
    mask_w = (dist_w >= 0) & (dist_w < WINDOW)
    s4 = lax.dot_general(q4, kw_ref[pl.ds(w0, n_win), :], nt, preferred_element_type=F32)
    ps = []
    for s in head_rows(s4):
        s = jnp.where(mask_w, s, NEG_INF)
        ps.append(jnp.exp(s - jnp.max(s, axis=-1, keepdims=True)))
    pv = jnp.dot(jnp.concatenate(ps, axis=0).astype(BF16), vw_ref[pl.ds(w0, n_win), :], preferred_element_type=F32)
    o_win = [normalise(o) for o in head_rows(pv)]

    gate = _sigmoid(gate_ref[...])
    outs = []
    for r in range(r_heads):
        c = N_BRANCHES * r
        outs.append(gate[:, c:c + 1] * o_cmp[r] + gate[:, c + 1:c + 2] * o_sel[r] + gate[:, c + 2:c + 3] * o_win[r])
    o_ref[...] = jnp.concatenate(outs, axis=-1).astype(BF16)


def _nsa_attention(nq, cmp_kv, ks_aug, vs, kw_aug, vw, gates, b, s):
    t = b * s
    g, r = NSA_KV_GROUPS, NSA_GROUP_SIZE
    tq = NSA_Q_TILE
    n_q = s // tq
    n_sel = s // SEL_BLOCK
    n_pad = s // CMP_STRIDE
    n_cmp = n_pad - 1
    n_rows = -(-n_sel // 8) * 8
    assert 2 + n_sel <= HEAD_DIM and s % NSA_KV_CHUNK == 0 and s >= WINDOW + tq and NSA_KV_CHUNK % tq == 0
    slopes = jnp.asarray(2.0 ** (-8.0 * np.arange(1, NSA_HEADS + 1) / NSA_HEADS), F32)
    cmp_start = np.arange(n_pad) * CMP_STRIDE
    cmp_end = cmp_start + CMP_BLOCK - 1
    feat = np.zeros((n_pad, HEAD_DIM), np.float32)
    feat[:, 0], feat[:, 1] = cmp_end // SEL_BLOCK, cmp_end % SEL_BLOCK
    kcmp_aug = jnp.concatenate([cmp_kv[0], jnp.broadcast_to(jnp.asarray(feat, BF16), cmp_kv[0].shape)], axis=-1)
    sel_start = np.arange(n_rows) * SEL_BLOCK
    ovl = ((cmp_start[None, :] < sel_start[:, None] + SEL_BLOCK) & (cmp_start[None, :] + CMP_BLOCK > sel_start[:, None])
           & (np.arange(n_pad)[None, :] < n_cmp) & (np.arange(n_rows)[:, None] < n_sel))
    ovl = jnp.asarray(ovl.astype(np.float32), BF16)
    place = jnp.asarray((np.arange(n_rows)[:, None] + 2 == np.arange(HEAD_DIM)[None, :]).astype(np.float32), BF16)
    seq = lambda w: pl.BlockSpec((None, s, w), lambda bi, gi, qi: (gi, bi, 0))
    cmp = lambda w: pl.BlockSpec((None, None, n_pad, w), lambda bi, gi, qi: (bi, gi, 0, 0))
    return pl.pallas_call(
        functools.partial(_nsa_kernel, n_sel=n_sel, n_cmp=n_cmp),
        name="nsa_attention",
        grid=(b, g, n_q),
        in_specs=[pl.BlockSpec(memory_space=pltpu.SMEM),
                  pl.BlockSpec((tq, r * HEAD_DIM), lambda bi, gi, qi: (bi * n_q + qi, gi)),
                  cmp(2 * HEAD_DIM), cmp(HEAD_DIM), seq(LANES), seq(LANES), seq(LANES), seq(LANES),
                  pl.BlockSpec((None, tq, LANES), lambda bi, gi, qi: (gi, bi * n_q + qi, 0)),
                  pl.BlockSpec(ovl.shape, lambda bi, gi, qi: (0, 0)),
                  pl.BlockSpec(place.shape, lambda bi, gi, qi: (0, 0))],
        out_specs=pl.BlockSpec((tq, r * HEAD_DIM), lambda bi, gi, qi: (bi * n_q + qi, gi)),
        out_shape=jax.ShapeDtypeStruct((t, NSA_W), BF16),
        compiler_params=_params("arbitrary", "arbitrary", "arbitrary"),
    )(slopes, nq, kcmp_aug, cmp_kv[1], ks_aug, vs, kw_aug, vw, gates, ovl, place)


def _even_outproj_kernel(h_ref, a_ref, b_ref, wa_ref, wb_ref, o_ref):
    y = jnp.dot(a_ref[...], wa_ref[...], preferred_element_type=F32)
    y = y + jnp.dot(b_ref[...], wb_ref[...], preferred_element_type=F32)
    o_ref[...] = h_ref[...] + y


def _even_outproj(h, y_ret, y_nsa, w_out):
    t = h.shape[0]
    tm = ROW_TILE
    wb = w_out.astype(BF16)
    wa, wbt = wb[:RET_W], wb[RET_W:]
    row = lambda n: pl.BlockSpec((tm, n), lambda i: (i, 0))
    full = lambda a: pl.BlockSpec(a.shape, lambda i: (0, 0))
    return pl.pallas_call(
        _even_outproj_kernel,
        name="even_outproj",
        grid=(t // tm,),
        in_specs=[row(D_MODEL), row(RET_W), row(NSA_W), full(wa), full(wbt)],
        out_specs=row(D_MODEL),
        out_shape=jax.ShapeDtypeStruct((t, D_MODEL), F32),
        compiler_params=_params("arbitrary"),
    )(h, y_ret, y_nsa, wa, wbt)


def _odd_mixer_kernel(h_ref, g_ref, win_ref, cw_ref, cb_ref, wout_ref, o_ref, tail_ref):
    @pl.when(pl.program_id(1) == 0)
    def _():
        tail_ref[...] = jnp.zeros_like(tail_ref)

    h = h_ref[...]
    ts = h.shape[0]
    xn = _rms(h, g_ref[...]).astype(BF16)
    proj = jnp.dot(xn, win_ref[...], preferred_element_type=F32)
    gate_b, gate_c, hid = proj[:, 0:D_MODEL], proj[:, D_MODEL:2 * D_MODEL], proj[:, 2 * D_MODEL:3 * D_MODEL]
    u = gate_c * hid
    row = lax.broadcasted_iota(I32, (ts, D_MODEL), 0)
    tail = tail_ref[...]
    prev1, prev2 = tail[7:8, :], tail[6:7, :]
    u1 = jnp.where(row >= 1, pltpu.roll(u, 1, axis=0), prev1)
    u2 = jnp.where(row >= 2, pltpu.roll(u, 2, axis=0), jnp.where(row == 1, prev1, prev2))
    cw = cw_ref[...]
    y = cw[0:1, :] * u2 + cw[1:2, :] * u1 + cw[2:3, :] * u + cb_ref[...]
    tail_ref[...] = u[ts - 8:ts, :]
    z = (gate_b * y).astype(BF16)
    o_ref[...] = h + jnp.dot(z, wout_ref[...], preferred_element_type=F32)


def _odd_mixer(h, g, w_in, conv_w, conv_b, w_out, b, s):
    t = b * s
    ts = ROW_TILE
    n_s = s // ts
    win, wout = w_in.astype(BF16), w_out.astype(BF16)
    gg, cb = g[None, :], conv_b[None, :]
    cw = jnp.pad(conv_w, ((0, 8 - conv_w.shape[0]), (0, 0)))
    full = lambda a: pl.BlockSpec(a.shape, lambda bi, si: (0, 0))
    row = pl.BlockSpec((ts, D_MODEL), lambda bi, si: (bi * n_s + si, 0))
    return pl.pallas_call(
        _odd_mixer_kernel,
        name="odd_mixer",
        grid=(b, n_s),
        in_specs=[row, full(gg), full(win), full(cw), full(cb), full(wout)],
        out_specs=row,
        out_shape=jax.ShapeDtypeStruct((t, D_MODEL), F32),
        scratch_shapes=[pltpu.VMEM((8, D_MODEL), F32)],
        compiler_params=_params("arbitrary", "arbitrary"),
    )(h, gg, win, cw, cb, wout)


def _store_slabs(ref, x, stride):
    rows = x.shape[0]
    for j in range(SLAB):
        ref[pl.ds(j, rows, stride=stride), :] = x[:, j * LANES:(j + 1) * LANES]


def _load_slabs(ref, rows, stride, offset=0):
    return jnp.concatenate([ref[pl.ds(offset + j, rows, stride=stride), :] for j in range(SLAB)], axis=-1)


def _router_kernel(h_ref, g_ref, wr_ref, br_ref, xn_ref, ids_ref, gates_ref):
    xn = _rms(h_ref[...], g_ref[...])
    _store_slabs(xn_ref, xn, SLAB)
    logits = lax.dot_general(wr_ref[...], xn, (((1,), (1,)), ((), ())), precision=HIGHEST,
                             preferred_element_type=F32) + br_ref[...]
    row = lambda i: logits[i:i + 1, :]

    def softmax(xs):
        m = functools.reduce(jnp.maximum, xs)
        es = [jnp.exp(x - m) for x in xs]
        tot = functools.reduce(lambda a, c: a + c, es)
        return [e / tot for e in es]

    def argmax(ps):
        best_p, best_i = ps[0], jnp.zeros_like(ps[0], dtype=I32)
        for i in range(1, len(ps)):
            upd = ps[i] > best_p
            best_p = jnp.where(upd, ps[i], best_p)
            best_i = jnp.where(upd, i, best_i)
        return best_p, best_i

    grp_p, grp_i = argmax(softmax([row(i) for i in range(N_GROUPS)]))
    in_grp = []
    for e in range(EXPERTS_PER_GROUP):
        x = row(N_GROUPS + e)
        for gidx in range(1, N_GROUPS):
            x = jnp.where(grp_i == gidx, row(N_GROUPS + gidx * EXPERTS_PER_GROUP + e), x)
        in_grp.append(x)
    pe = softmax(in_grp)
    p1, i1 = argmax(pe)
    p2, i2 = argmax([jnp.where(i1 == e, -1.0, pe[e]) for e in range(EXPERTS_PER_GROUP)])
    tot = p1 + p2
    ids_ref[...] = jnp.concatenate([grp_i * EXPERTS_PER_GROUP + i1, grp_i * EXPERTS_PER_GROUP + i2], axis=0)
    gates_ref[...] = jnp.concatenate([grp_p * p1 / tot, grp_p * p2 / tot], axis=0)


def _router(h, g, wg_group, bg_group, wg_expert, bg_expert):
    t = h.shape[0]
    tm = ROW_TILE
    n_logit = N_GROUPS + N_EXPERTS
    wr = jnp.pad(jnp.concatenate([wg_group, wg_expert], axis=1).T, ((0, ROUTER_ROWS - n_logit), (0, 0)))
    br = jnp.pad(jnp.concatenate([bg_group, bg_expert]), (0, ROUTER_ROWS - n_logit))
    br = jnp.broadcast_to(br[:, None], (ROUTER_ROWS, tm))
    gg = g[None, :]
    full = lambda a: pl.BlockSpec(a.shape, lambda i: (0, 0))
    return pl.pallas_call(
        _router_kernel,
        name="moe_router",
        grid=(t // tm,),
        in_specs=[pl.BlockSpec((tm, D_MODEL), lambda i: (i, 0)), full(gg), full(wr), full(br)],
        out_specs=[pl.BlockSpec((tm * SLAB, LANES), lambda i: (i, 0)),
                   pl.BlockSpec((TOP_K_IN_GROUP, tm), lambda i: (0, i)),
                   pl.BlockSpec((TOP_K_IN_GROUP, tm), lambda i: (0, i))],
        out_shape=[jax.ShapeDtypeStruct((t * SLAB, LANES), F32), jax.ShapeDtypeStruct((TOP_K_IN_GROUP, t), I32),
                   jax.ShapeDtypeStruct((TOP_K_IN_GROUP, t), F32)],
        compiler_params=_params("arbitrary"),
    )(h, gg, wr, br)


def _expert_kernel(blk_expert_ref, n_used_ref, plan_hbm, x_hbm, wg_ref, wu_ref, wd_ref, y_hbm,
                   plan, xbuf, ybuf, sem_plan, sem_in, sem_out):
    del blk_expert_ref
    i = pl.program_id(0)
    n_used = n_used_ref[0]
    slot = i % 2
    other = 1 - slot

    def plan_copy(k):
        return pltpu.make_async_copy(plan_hbm.at[k], plan.at[k % PLAN_SLOTS], sem_plan.at[k % PLAN_SLOTS])

    def row_copy(hbm, hbm_row, buf, r, sem, to_hbm):
        h = hbm.at[pl.ds(pl.multiple_of(hbm_row * SLAB, SLAB), SLAB)]
        v = buf.at[pl.ds(r * SLAB, SLAB)]
        return pltpu.make_async_copy(v, h, sem) if to_hbm else pltpu.make_async_copy(h, v, sem)

    def wait_gather(s):
        for r in range(MOE_BLOCK):
            row_copy(x_hbm, 0, xbuf.at[s], r, sem_in.at[s], False).wait()

    def wait_scatter(s):
        for r in range(MOE_BLOCK):
            row_copy(y_hbm, 0, ybuf.at[s], r, sem_out.at[s], True).wait()

    @pl.when(i == 0)
    def _():
        ybuf[...] = jnp.zeros_like(ybuf)
        for k in range(3):
            plan_copy(k).start()
        plan_copy(0).wait()
        plan_copy(1).wait()
        for r in range(MOE_BLOCK):
            row_copy(x_hbm, plan[1, 0, r], xbuf.at[0], r, sem_in.at[0], False).start()

    @pl.when(i <= n_used)
    def _():
        plan_copy(i + 2).wait()
        plan_copy(i + 3).start()
        wait_gather(slot)

        @pl.when(i >= 1)
        def _():
            wait_scatter(slot)

        nxt, prv = (i + 2) % PLAN_SLOTS, i % PLAN_SLOTS
        for r in range(MOE_BLOCK):
            row_copy(x_hbm, plan[nxt, 0, r], xbuf.at[other], r, sem_in.at[other], False).start(priority=0)
            row_copy(y_hbm, plan[prv, 0, MOE_BLOCK + r], ybuf.at[other], r, sem_out.at[other], True).start(priority=1)
        x = _load_slabs(xbuf.at[slot], MOE_BLOCK, SLAB).astype(BF16)
        a = jnp.dot(x, wg_ref[...].astype(BF16), preferred_element_type=F32)
        u = jnp.dot(x, wu_ref[...].astype(BF16), preferred_element_type=F32)
        hid = (a * _sigmoid(a) * u).astype(BF16)
        _store_slabs(ybuf.at[slot], jnp.dot(hid, wd_ref[...].astype(BF16), preferred_element_type=F32), SLAB)

        @pl.when(i == n_used)
        def _():
            plan_copy(i + 3).wait()
            wait_gather(other)
            wait_scatter(other)


def _moe_plan(ids, t):
    n_slot = t * TOP_K_IN_GROUP
    n_blocks = n_slot // MOE_BLOCK + N_EXPERTS
    flat_e = ids.T.reshape(-1)
    order = jnp.argsort(flat_e).astype(I32)
    experts = jnp.arange(N_EXPERTS, dtype=I32)
    counts = jnp.sum(flat_e[:, None] == experts[None, :], axis=0, dtype=I32)
    n_blk = (counts + MOE_BLOCK - 1) // MOE_BLOCK
    blk_end = jnp.cumsum(n_blk)
    start = jnp.cumsum(counts) - counts
    blocks = jnp.arange(n_blocks, dtype=I32)
    blk_expert = jnp.minimum(jnp.sum(blk_end[None, :] <= blocks[:, None], axis=1, dtype=I32), N_EXPERTS - 1)
    row0 = (blocks - (blk_end - n_blk)[blk_expert]) * MOE_BLOCK
    rows = row0[:, None] + jnp.arange(MOE_BLOCK, dtype=I32)[None, :]
    is_slot = (rows >= 0) & (rows < counts[blk_expert][:, None])
    slot = order[jnp.clip(start[blk_expert][:, None] + rows, 0, n_slot - 1)]
    trash = n_slot + jnp.arange(MOE_BLOCK, dtype=I32)[None, :]
    src = jnp.where(is_slot, slot // TOP_K_IN_GROUP, 0).astype(I32)
    dst = jnp.where(is_slot, slot, trash).astype(I32)
    pad = jnp.concatenate([jnp.zeros_like(trash), trash], axis=1).astype(I32)
    plan = jnp.concatenate([pad, jnp.concatenate([src, dst], axis=1)] + [pad] * (PLAN_SLOTS - 1))
    return blk_expert, blk_end[-1:].astype(I32), plan[:, None, :]


def _experts(xn_slabs, ids, w_gate, w_up, w_down, layer):
    t = xn_slabs.shape[0] // SLAB
    blk_expert, n_used, plan = _moe_plan(ids, t)
    n_blocks = blk_expert.shape[0]
    w_spec = lambda a: pl.BlockSpec((None, None) + a.shape[2:], lambda i, be, nu: (layer, be[i], 0, 0))
    buf = pltpu.VMEM((2, MOE_BLOCK * SLAB, LANES), F32)
    return pl.pallas_call(
        _expert_kernel,
        name="moe_experts",
        grid_spec=pltpu.PrefetchScalarGridSpec(
            num_scalar_prefetch=2,
            grid=(n_blocks,),
            in_specs=[pl.BlockSpec(memory_space=pl.ANY), pl.BlockSpec(memory_space=pl.ANY),
                      w_spec(w_gate), w_spec(w_up), w_spec(w_down)],
            out_specs=pl.BlockSpec(memory_space=pl.ANY),
            scratch_shapes=[pltpu.SMEM((PLAN_SLOTS, 1, 2 * MOE_BLOCK), I32), buf, buf,
                            pltpu.SemaphoreType.DMA((PLAN_SLOTS,)), pltpu.SemaphoreType.DMA((2,)),
                            pltpu.SemaphoreType.DMA((2,))]),
        out_shape=jax.ShapeDtypeStruct(((t * TOP_K_IN_GROUP + MOE_BLOCK) * SLAB, LANES), F32),
        compiler_params=_params("arbitrary"),
    )(blk_expert, n_used, plan, xn_slabs, w_gate, w_up, w_down)


def _combine_kernel(h_ref, y_ref, gate_ref, o_ref):
    gate = gate_ref[...]
    rows = h_ref.shape[0]
    y0 = _load_slabs(y_ref, rows, TOP_K_IN_GROUP * SLAB)
    y1 = _load_slabs(y_ref, rows, TOP_K_IN_GROUP * SLAB, SLAB)
    o_ref[...] = h_ref[...] + gate[:, 0:1] * y0 + gate[:, 1:2] * y1


def _combine(h, y_slabs, gates):
    t = h.shape[0]
    tm = ROW_TILE
    return pl.pallas_call(
        _combine_kernel,
        name="moe_combine",
        grid=(t // tm,),
        in_specs=[pl.BlockSpec((tm, D_MODEL), lambda i: (i, 0)),
                  pl.BlockSpec((tm * TOP_K_IN_GROUP * SLAB, LANES), lambda i: (i, 0)),
                  pl.BlockSpec((tm, TOP_K_IN_GROUP), lambda i: (i, 0))],
        out_specs=pl.BlockSpec((tm, D_MODEL), lambda i: (i, 0)),
        out_shape=jax.ShapeDtypeStruct((t, D_MODEL), F32),
        compiler_params=_params("arbitrary"),
    )(h, y_slabs, gates.T)


def _moe(h, g, wg_group, bg_group, wg_expert, bg_expert, w_gate, w_up, w_down, layer):
    xn_slabs, ids, gates = _router(h, g, wg_group, bg_group, wg_expert, bg_expert)
    y_slabs = _experts(xn_slabs, ids, w_gate, w_up, w_down, layer)
    return _combine(h, y_slabs, gates)


def _even_mixer(h, g, w_in, cmp_pe, w_cmp1, w_cmp2, q_norm_g, k_norm_g, ret_norm_g, w_out, b, s):
    ret, nq, kcv, ks_aug, vs, kw_aug, vw, gates = _even_inproj(h, g, w_in, q_norm_g, k_norm_g, s)
    y_ret = _retention(ret, ret_norm_g, b, s)
    cmp_kv = _compress(kcv[:, :NSA_KV_W], kcv[:, NSA_KV_W:], cmp_pe, w_cmp1, w_cmp2, k_norm_g, b, s)
    y_nsa = _nsa_attention(nq, cmp_kv, ks_aug, vs, kw_aug, vw, gates, b, s)
    return _even_outproj(h, y_ret, y_nsa, w_out)


def kernel(x, mix_norm_g, ffn_norm_g, ev_w_in, ev_cmp_pe, ev_w_cmp1, ev_w_cmp2, ev_q_norm_g, ev_k_norm_g, ev_ret_norm_g, ev_w_out, od_w_in, od_conv_w, od_conv_b, od_w_out, moe_wg_group, moe_bg_group, moe_wg_expert, moe_bg_expert, moe_w_gate, moe_w_up, moe_w_down):
    b, s, d = x.shape
    h = x.reshape(b * s, d)
    for layer in range(mix_norm_g.shape[0]):
        i = layer // 2
        if layer % 2 == 0:
            h = _even_mixer(h, mix_norm_g[layer], ev_w_in[i], ev_cmp_pe[i], ev_w_cmp1[i], ev_w_cmp2[i],
                            ev_q_norm_g[i], ev_k_norm_g[i], ev_ret_norm_g[i], ev_w_out[i], b, s)
        else:
            h = _odd_mixer(h, mix_norm_g[layer], od_w_in[i], od_conv_w[i], od_conv_b[i], od_w_out[i], b, s)
        h = _moe(h, ffn_norm_g[layer], moe_wg_group[layer], moe_bg_group[layer], moe_wg_expert[layer],
                 moe_bg_expert[layer], moe_w_gate, moe_w_up, moe_w_down, layer)
    return h.reshape(b, s, d)
```

```python
import functools

import jax
import jax.numpy as jnp
import numpy as np
from jax import lax
from jax.experimental import pallas as pl
from jax.experimental.pallas import tpu as pltpu

F32 = jnp.float32
BF16 = jnp.bfloat16
I32 = jnp.int32
HIGHEST = lax.Precision.HIGHEST

D_MODEL = 1024
HEAD_DIM = 64
RET_HEADS = 8
NSA_HEADS = 8
NSA_KV_GROUPS = 2
NSA_GROUP_SIZE = NSA_HEADS // NSA_KV_GROUPS
RET_W = RET_HEADS * HEAD_DIM
NSA_W = NSA_HEADS * HEAD_DIM
NSA_KV_W = NSA_KV_GROUPS * HEAD_DIM
N_BRANCHES = 3
RET_CHUNK = 128
CMP_STRIDE = 16
CMP_BLOCK = 2 * CMP_STRIDE
CMP_HIDDEN = 128
SEL_BLOCK = 64
TOP_N = 8
WINDOW = 256
N_GROUPS = 4
EXPERTS_PER_GROUP = 8
N_EXPERTS = N_GROUPS * EXPERTS_PER_GROUP
TOP_K_IN_GROUP = 2
D_EXPERT = 256
RMS_EPS = 1e-6
NEG_INF = -1e30
ATTN_SCALE = HEAD_DIM ** -0.5

LANES = 128
VMEM_LIMIT = 48 * 1024 * 1024
ROW_TILE = 512
NSA_Q_TILE = 256
NSA_KV_CHUNK = 512
MOE_BLOCK = 256
ROUTER_ROWS = 40
SLAB = D_MODEL // LANES
PLAN_SLOTS = 8
ROW_BUFS = 3


def _params(*sem):
    return pltpu.CompilerParams(dimension_semantics=sem, vmem_limit_bytes=VMEM_LIMIT)


def _rms(x, g):
    return x * lax.rsqrt(jnp.mean(x * x, axis=-1, keepdims=True) + RMS_EPS) * g


def _group_rms(x, bd, gain):
    sq = x * x
    hi = sq.astype(BF16)
    lo = (sq - hi.astype(F32)).astype(BF16)
    ms = jnp.dot(hi, bd, preferred_element_type=F32) + jnp.dot(lo, bd, preferred_element_type=F32)
    return x * lax.rsqrt(ms + RMS_EPS) * gain


def _sigmoid(x):
    return 1.0 / (1.0 + jnp.exp(-x))


def _block_diag_mean(n):
    idx = np.arange(n) // HEAD_DIM
    return jnp.asarray((idx[:, None] == idx[None, :]).astype(np.float32) / HEAD_DIM, BF16)


def _even_inproj_kernel(x_ref, g_ref, wret_ref, wnq_ref, wkv_ref, wng_ref, qgain_ref, kgain_ref, bd512_ref, bd128_ref,
                        ret_ref, nq_ref, kcv_ref, ks_ref, vs_ref, kw_ref, vw_ref, gate_ref, *, seq_tiles):
    tm = x_ref.shape[0]
    xn = _rms(x_ref[...], g_ref[...]).astype(BF16)
    ret_ref[...] = jnp.dot(xn, wret_ref[...], preferred_element_type=F32).astype(BF16)
    nq = jnp.dot(xn, wnq_ref[...], preferred_element_type=F32)
    nq_ref[...] = _group_rms(nq, bd512_ref[...], qgain_ref[...]).astype(BF16)
    kv = jnp.dot(xn, wkv_ref[...], preferred_element_type=F32)
    w = NSA_KV_W
    kcv_ref[...] = kv[:, 0:2 * w].astype(BF16)
    ks = _group_rms(kv[:, 2 * w:3 * w], bd128_ref[...], kgain_ref[0:1, :])
    kw = _group_rms(kv[:, 4 * w:5 * w], bd128_ref[...], kgain_ref[1:2, :])
    vs, vw = kv[:, 3 * w:4 * w], kv[:, 5 * w:6 * w]
    pos = (pl.program_id(0) % seq_tiles) * tm + lax.broadcasted_iota(I32, (tm, HEAD_DIM), 0)
    lane = lax.broadcasted_iota(I32, (tm, HEAD_DIM), 1)
    blk = lax.shift_right_arithmetic(pos, int(np.log2(SEL_BLOCK)))
    feat_w = jnp.where(lane == 0, blk, jnp.where(lane == 1, pos & (SEL_BLOCK - 1), 0)).astype(F32)
    feat_s = feat_w + jnp.where(lane == blk + 2, 1.0, 0.0)
    ones_col = jnp.where(lane == 0, 1.0, 0.0)
    for g in range(NSA_KV_GROUPS):
        sl = slice(g * HEAD_DIM, (g + 1) * HEAD_DIM)
        ks_ref[g] = jnp.concatenate([ks[:, sl], feat_s], axis=-1).astype(BF16)
        kw_ref[g] = jnp.concatenate([kw[:, sl], feat_w], axis=-1).astype(BF16)
        vs_ref[g] = jnp.concatenate([vs[:, sl], ones_col], axis=-1).astype(BF16)
        vw_ref[g] = jnp.concatenate([vw[:, sl], ones_col], axis=-1).astype(BF16)
    ng = jnp.dot(xn, wng_ref[...], preferred_element_type=F32)
    for g in range(NSA_KV_GROUPS):
        gate_ref[g] = ng[:, g * LANES:(g + 1) * LANES]


def _even_inproj(h, g, w_in, q_norm_g, k_norm_g, s):
    t = h.shape[0]
    tm = ROW_TILE
    c_ret, c_nq, c_kv = 4 * RET_W, 4 * RET_W + NSA_W, 4 * RET_W + NSA_W + 6 * NSA_KV_W
    n_gate = NSA_GROUP_SIZE * N_BRANCHES
    wb = w_in.astype(BF16)
    wret, wnq, wkv = wb[:, :c_ret], wb[:, c_ret:c_nq], wb[:, c_nq:c_kv]
    wng = jnp.concatenate([jnp.pad(wb[:, c_kv + gi * n_gate:c_kv + (gi + 1) * n_gate], ((0, 0), (0, LANES - n_gate)))
                           for gi in range(NSA_KV_GROUPS)], axis=1)
    qgain = jnp.tile(q_norm_g, NSA_HEADS)[None, :]
    kgain = jnp.stack([jnp.tile(k_norm_g[1], NSA_KV_GROUPS), jnp.tile(k_norm_g[2], NSA_KV_GROUPS)])
    full = lambda a: pl.BlockSpec(a.shape, lambda i: (0,) * a.ndim)
    row = lambda n: pl.BlockSpec((tm, n), lambda i: (i, 0))
    grp = lambda n: pl.BlockSpec((NSA_KV_GROUPS, tm, n), lambda i: (0, i, 0))
    grp_shape = lambda n, dt: jax.ShapeDtypeStruct((NSA_KV_GROUPS, t, n), dt)
    bd512, bd128 = _block_diag_mean(NSA_W), _block_diag_mean(NSA_KV_W)
    gg = g[None, :]
    return pl.pallas_call(
        functools.partial(_even_inproj_kernel, seq_tiles=s // tm),
        name="even_inproj",
        grid=(t // tm,),
        in_specs=[row(D_MODEL), full(gg), full(wret), full(wnq), full(wkv), full(wng), full(qgain),
                  full(kgain), full(bd512), full(bd128)],
        out_specs=[row(c_ret), row(NSA_W), row(2 * NSA_KV_W), grp(LANES), grp(LANES), grp(LANES), grp(LANES), grp(LANES)],
        out_shape=[jax.ShapeDtypeStruct((t, c_ret), BF16), jax.ShapeDtypeStruct((t, NSA_W), BF16),
                   jax.ShapeDtypeStruct((t, 2 * NSA_KV_W), BF16), grp_shape(LANES, BF16), grp_shape(LANES, BF16),
                   grp_shape(LANES, BF16), grp_shape(LANES, BF16), grp_shape(LANES, F32)],
        compiler_params=_params("arbitrary"),
    )(h, gg, wret, wnq, wkv, wng, qgain, kgain, bd512, bd128)


def _retention_kernel(q_ref, k_ref, v_ref, rg_ref, decay_ref, qd_ref, kd_ref, cd_ref, gain_ref, o_ref):
    n_chunks = q_ref.shape[0] // RET_CHUNK
    gain = gain_ref[...]
    nt = (((1,), (1,)), ((), ()))
    tn = (((0,), (0,)), ((), ()))

    def body(c, states):
        r0 = pl.multiple_of(c * RET_CHUNK, RET_CHUNK)
        rows = pl.ds(r0, RET_CHUNK)
        q2, k2, v2 = q_ref[rows, :], k_ref[rows, :], v_ref[rows, :]
        rg2 = rg_ref[rows, :].astype(F32)
        outs, new_states = [], []
        for hh in range(2):
            lanes = slice(hh * HEAD_DIM, (hh + 1) * HEAD_DIM)
            q, v = q2[:, lanes], v2[:, lanes]
            kf = k2[:, lanes].astype(F32) * ATTN_SCALE
            state = states[hh]
            s = lax.dot_general(q, kf.astype(BF16), nt, preferred_element_type=F32) * decay_ref[hh]
            y = jnp.dot(s.astype(BF16), v, preferred_element_type=F32)
            qd = (q.astype(F32) * qd_ref[hh]).astype(BF16)
            y = y + jnp.dot(qd, state.astype(BF16), preferred_element_type=F32)
            kd = (kf * kd_ref[hh]).astype(BF16)
            new_states.append(cd_ref[hh] * state + lax.dot_general(kd, v, tn, preferred_element_type=F32))
            rg = rg2[:, lanes]
            outs.append(_rms(y, gain) * (rg * _sigmoid(rg)))
        o_ref[rows, :] = jnp.concatenate(outs, axis=-1).astype(BF16)
        return tuple(new_states)

    zero = jnp.zeros((HEAD_DIM, HEAD_DIM), F32)
    lax.fori_loop(0, n_chunks, body, (zero, zero), unroll=True)


def _retention_tables():
    h = np.arange(RET_HEADS, dtype=np.float64)
    log_g = np.log(1.0 - 2.0 ** (-5.0 - h))
    pos = np.arange(RET_CHUNK, dtype=np.float64)
    diff = pos[:, None] - pos[None, :]
    decay = np.where(diff >= 0, np.exp(log_g[:, None, None] * np.maximum(diff, 0.0)), 0.0)
    qd = np.exp(log_g[:, None] * (pos + 1.0))[:, :, None] * np.ones((1, 1, HEAD_DIM))
    kd = np.exp(log_g[:, None] * (RET_CHUNK - 1 - pos))[:, :, None] * np.ones((1, 1, HEAD_DIM))
    cd = np.exp(log_g * RET_CHUNK)[:, None, None] * np.ones((1, HEAD_DIM, HEAD_DIM))
    return [jnp.asarray(a, F32) for a in (decay, qd, kd, cd)]


def _retention(ret, ret_norm_g, b, s):
    t = b * s
    n_pairs = RET_HEADS // 2
    decay, qd, kd, cd = _retention_tables()
    col = lambda off: pl.BlockSpec((s, LANES), lambda bi, p: (bi, off * n_pairs + p))
    tab = lambda a: pl.BlockSpec((2,) + a.shape[1:], lambda bi, p: (p, 0, 0))
    gain = ret_norm_g[None, :]
    return pl.pallas_call(
        _retention_kernel,
        name="retention",
        grid=(b, n_pairs),
        in_specs=[col(0), col(1), col(2), col(3), tab(decay), tab(qd), tab(kd), tab(cd),
                  pl.BlockSpec(gain.shape, lambda bi, p: (0, 0))],
        out_specs=pl.BlockSpec((s, LANES), lambda bi, p: (bi, p)),
        out_shape=jax.ShapeDtypeStruct((t, RET_W), BF16),
        compiler_params=_params("arbitrary", "arbitrary"),
    )(ret, ret, ret, ret, decay, qd, kd, cd, gain)


def _compress_kernel(x_ref, pe_ref, w1_ref, w2_ref, kgain_ref, o_ref):
    is_key = pl.program_id(0) == 0
    half = CMP_STRIDE * HEAD_DIM
    x = x_ref[...]
    n = x.shape[0]
    a = jnp.dot(x, w1_ref[0:half, :], preferred_element_type=F32)
    bm = jnp.dot(x, w1_ref[half:2 * half, :], preferred_element_type=F32)
    pew = jnp.dot(pe_ref[...], w1_ref[...], preferred_element_type=F32)[0:1, :]
    pre = a + pltpu.roll(bm, n - 1, axis=0) + pew
    hid = 0.5 * pre * (1.0 + jnp.tanh(np.sqrt(2.0 / np.pi) * (pre + 0.044715 * pre * pre * pre)))
    out = jnp.dot(hid.astype(BF16), w2_ref[...], preferred_element_type=F32)
    normed = _rms(out, kgain_ref[...])
    o_ref[...] = jnp.where(is_key, normed, out).astype(BF16)


def _compress(kc, vc, cmp_pe, w_cmp1, w_cmp2, k_norm_g, b, s):
    n = s // CMP_STRIDE
    g = NSA_KV_GROUPS
    flat = CMP_STRIDE * HEAD_DIM

    def chunks(a):
        return a.reshape(b, n, CMP_STRIDE, g, HEAD_DIM).transpose(0, 3, 1, 2, 4).reshape(b, g, n, flat)

    x = jnp.stack([chunks(kc), chunks(vc)])
    pe = jnp.broadcast_to(cmp_pe.reshape(2, 1, 2 * flat), (2, 8, 2 * flat)).astype(BF16)
    w1, w2 = w_cmp1.astype(BF16), w_cmp2.astype(BF16)
    kgain = k_norm_g[0][None, :]
    return pl.pallas_call(
        _compress_kernel,
        name="nsa_compress",
        grid=(2, b, g),
        in_specs=[pl.BlockSpec((None, None, None, n, flat), lambda kv, bi, gi: (kv, bi, gi, 0, 0)),
                  pl.BlockSpec((None, 8, 2 * flat), lambda kv, bi, gi: (kv, 0, 0)),
                  pl.BlockSpec((None, 2 * flat, CMP_HIDDEN), lambda kv, bi, gi: (kv, 0, 0)),
                  pl.BlockSpec((None, CMP_HIDDEN, HEAD_DIM), lambda kv, bi, gi: (kv, 0, 0)),
                  pl.BlockSpec(kgain.shape, lambda kv, bi, gi: (0, 0))],
        out_specs=pl.BlockSpec((None, None, None, n, HEAD_DIM), lambda kv, bi, gi: (kv, bi, gi, 0, 0)),
        out_shape=jax.ShapeDtypeStruct((2, b, g, n, HEAD_DIM), BF16),
        compiler_params=_params("arbitrary", "arbitrary", "arbitrary"),
    )(x, pe, w1, w2, kgain)


def _nsa_kernel(slopes_ref, q_ref, kcmp_ref, vcmp_ref, ks_ref, vs_ref, kw_ref, vw_ref, gate_ref, ovl_ref, place_ref,
                o_ref, *, n_sel, n_cmp):
    r_heads = NSA_GROUP_SIZE
    tq = q_ref.shape[0]
    gi = pl.program_id(1)
    t0 = pl.program_id(2) * tq
    nt = (((1,), (1,)), ((), ()))
    tn = (((0,), (0,)), ((), ()))

    q = q_ref[...]
    lane = lax.broadcasted_iota(I32, (tq, HEAD_DIM), 1)
    qs, feats = [], []
    for r in range(r_heads):
        slope = slopes_ref[gi * r_heads + r]
        qs.append((q[:, r * HEAD_DIM:(r + 1) * HEAD_DIM].astype(F32) * ATTN_SCALE).astype(BF16))
        feats.append(jnp.where(lane == 0, slope * SEL_BLOCK, jnp.where(lane == 1, slope, 0.0)))

    def stack_q(extra):
        return jnp.concatenate([jnp.concatenate([qs[r], (feats[r] + extra).astype(BF16)], axis=-1)
                                for r in range(r_heads)], axis=0)

    q4 = stack_q(0.0)
    t_col = t0 + lax.broadcasted_iota(I32, (tq, 1), 0)

    def head_rows(x):
        return [x[r * tq:(r + 1) * tq] for r in range(r_heads)]

    n_pad = kcmp_ref.shape[0]
    c_idx = lax.broadcasted_iota(I32, (tq, n_pad), 1)
    mask_c = (t_col >= c_idx * CMP_STRIDE + (CMP_BLOCK - 1)) & (c_idx < n_cmp)
    s4 = lax.dot_general(q4, kcmp_ref[...], nt, preferred_element_type=F32)
    ps = []
    for s in head_rows(s4):
        s = jnp.where(mask_c, s, NEG_INF)
        p = jnp.where(mask_c, jnp.exp(s - jnp.max(s, axis=-1, keepdims=True)), 0.0)
        ps.append(p * (1.0 / jnp.maximum(jnp.sum(p, axis=-1, keepdims=True), 1e-30)))
    p4 = jnp.concatenate(ps, axis=0).astype(BF16)
    o_cmp = head_rows(jnp.dot(p4, vcmp_ref[...], preferred_element_type=F32))

    n_rows = ovl_ref.shape[0]
    imp4 = lax.dot_general(ovl_ref[...], p4, nt, preferred_element_type=F32)
    imp = imp4[:, 0:tq]
    for r in range(1, r_heads):
        imp = imp + imp4[:, r * tq:(r + 1) * tq]
    j_idx = lax.broadcasted_iota(I32, (n_rows, tq), 0)
    t_row = t0 + lax.broadcasted_iota(I32, (n_rows, tq), 1)
    q_blk = lax.shift_right_arithmetic(t_row, int(np.log2(SEL_BLOCK)))
    valid = (j_idx * SEL_BLOCK <= t_row) & (j_idx < n_sel)
    forced = (j_idx == 0) | (j_idx == q_blk) | (j_idx == q_blk - 1)
    val = jnp.where(j_idx < n_sel, jnp.where(forced, imp + 1e3, jnp.where(valid, imp, -1e3)), -2e3)
    rank = jnp.zeros((n_rows, tq), F32)
    for k in range(n_sel):
        vk = val[k:k + 1, :]
        rank = rank + jnp.where(vk > val, 1.0, jnp.where(vk == val, jnp.where(j_idx > k, 1.0, 0.0), 0.0))
    unselected = jnp.where(valid, jnp.where(rank < TOP_N, 0.0, NEG_INF), NEG_INF).astype(BF16)
    q4_sel = stack_q(lax.dot_general(unselected, place_ref[...], tn, preferred_element_type=F32))

    ck = NSA_KV_CHUNK

    def sel_step(c, carry, causal):
        c0 = pl.multiple_of(c * ck, ck)
        s4 = lax.dot_general(q4_sel, ks_ref[pl.ds(c0, ck), :], nt, preferred_element_type=F32)
        if causal:
            visible = t_col >= c0 + lax.broadcasted_iota(I32, (tq, ck), 1)
        ps, new = [], []
        for r, s in enumerate(head_rows(s4)):
            if causal:
                s = jnp.where(visible, s, NEG_INF)
            m_old, acc_old = carry[r]
            m_new = jnp.maximum(m_old, jnp.max(s, axis=-1, keepdims=True))
            ps.append(jnp.exp(s - m_new))
            new.append((m_new, jnp.exp(m_old - m_new) * acc_old))
        pv = jnp.dot(jnp.concatenate(ps, axis=0).astype(BF16), vs_ref[pl.ds(c0, ck), :], preferred_element_type=F32)
        return tuple((m, acc + o) for (m, acc), o in zip(new, head_rows(pv)))

    def normalise(acc):
        return acc[:, 0:HEAD_DIM] * (1.0 / acc[:, HEAD_DIM:HEAD_DIM + 1])

    init = tuple((jnp.full((tq, 1), NEG_INF, F32), jnp.zeros((tq, LANES), F32)) for _ in range(r_heads))
    c_last = t0 // ck
    carry = lax.fori_loop(0, c_last, lambda c, carry: sel_step(c, carry, False), init)
    o_sel = [normalise(acc) for (_, acc) in sel_step(c_last, carry, True)]

    n_win = WINDOW + tq
    w0 = pl.multiple_of(jnp.maximum(t0 - WINDOW, 0), tq)
    dist_w = t_col - (w0 + lax.broadcasted_iota(I32, (tq, n_win), 1))
    mask_w = (dist_w >= 0) & (dist_w < WINDOW)
    s4 = lax.dot_general(q4, kw_ref[pl.ds(w0, n_win), :], nt, preferred_element_type=F32)
    ps = []
    for s in head_rows(s4):
        s = jnp.where(mask_w, s, NEG_INF)
        ps.append(jnp.exp(s - jnp.max(s, axis=-1, keepdims=True)))
    pv = jnp.dot(jnp.concatenate(ps, axis=0).astype(BF16), vw_ref[pl.ds(w0, n_win), :], preferred_element_type=F32)
    o_win = [normalise(o) for o in head_rows(pv)]

    gate = _sigmoid(gate_ref[...])
    outs = []
    for r in range(r_heads):
        c = N_BRANCHES * r
        outs.append(gate[:, c:c + 1] * o_cmp[r] + gate[:, c + 1:c + 2] * o_sel[r] + gate[:, c + 2:c + 3] * o_win[r])
    o_ref[...] = jnp.concatenate(outs, axis=-1).astype(BF16)


def _nsa_attention(nq, cmp_kv, ks_aug, vs, kw_aug, vw, gates, b, s):
    t = b * s
    g, r = NSA_KV_GROUPS, NSA_GROUP_SIZE
    tq = NSA_Q_TILE
    n_q = s // tq
    n_sel = s // SEL_BLOCK
    n_pad = s // CMP_STRIDE
    n_cmp = n_pad - 1
    n_rows = -(-n_sel // 8) * 8
    assert 2 + n_sel <= HEAD_DIM and s % NSA_KV_CHUNK == 0 and s >= WINDOW + tq and NSA_KV_CHUNK % tq == 0
    slopes = jnp.asarray(2.0 ** (-8.0 * np.arange(1, NSA_HEADS + 1) / NSA_HEADS), F32)
    cmp_start = np.arange(n_pad) * CMP_STRIDE
    cmp_end = cmp_start + CMP_BLOCK - 1
    feat = np.zeros((n_pad, HEAD_DIM), np.float32)
    feat[:, 0], feat[:, 1] = cmp_end // SEL_BLOCK, cmp_end % SEL_BLOCK
    kcmp_aug = jnp.concatenate([cmp_kv[0], jnp.broadcast_to(jnp.asarray(feat, BF16), cmp_kv[0].shape)], axis=-1)
    sel_start = np.arange(n_rows) * SEL_BLOCK
    ovl = ((cmp_start[None, :] < sel_start[:, None] + SEL_BLOCK) & (cmp_start[None, :] + CMP_BLOCK > sel_start[:, None])
           & (np.arange(n_pad)[None, :] < n_cmp) & (np.arange(n_rows)[:, None] < n_sel))
    ovl = jnp.asarray(ovl.astype(np.float32), BF16)
    place = jnp.asarray((np.arange(n_rows)[:, None] + 2 == np.arange(HEAD_DIM)[None, :]).astype(np.float32), BF16)
    seq = lambda w: pl.BlockSpec((None, s, w), lambda bi, gi, qi: (gi, bi, 0))
    cmp = lambda w: pl.BlockSpec((None, None, n_pad, w), lambda bi, gi, qi: (bi, gi, 0, 0))
    return pl.pallas_call(
        functools.partial(_nsa_kernel, n_sel=n_sel, n_cmp=n_cmp),
        name="nsa_attention",
        grid=(b, g, n_q),
        in_specs=[pl.BlockSpec(memory_space=pltpu.SMEM),
                  pl.BlockSpec((tq, r * HEAD_DIM), lambda bi, gi, qi: (bi * n_q + qi, gi)),
                  cmp(2 * HEAD_DIM), cmp(HEAD_DIM), seq(LANES), seq(LANES), seq(LANES), seq(LANES),
                  pl.BlockSpec((None, tq, LANES), lambda bi, gi, qi: (gi, bi * n_q + qi, 0)),
                  pl.BlockSpec(ovl.shape, lambda bi, gi, qi: (0, 0)),
                  pl.BlockSpec(place.shape, lambda bi, gi, qi: (0, 0))],
        out_specs=pl.BlockSpec((tq, r * HEAD_DIM), lambda bi, gi, qi: (bi * n_q + qi, gi)),
        out_shape=jax.ShapeDtypeStruct((t, NSA_W), BF16),
        compiler_params=_params("arbitrary", "arbitrary", "arbitrary"),
    )(slopes, nq, kcmp_aug, cmp_kv[1], ks_aug, vs, kw_aug, vw, gates, ovl, place)


def _even_outproj_kernel(h_ref, a_ref, b_ref, wa_ref, wb_ref, o_ref):
    y = jnp.dot(a_ref[...], wa_ref[...], preferred_element_type=F32)
    y = y + jnp.dot(b_ref[...], wb_ref[...], preferred_element_type=F32)
    o_ref[...] = h_ref[...] + y


def _even_outproj(h, y_ret, y_nsa, w_out):
    t = h.shape[0]
    tm = ROW_TILE
    wb = w_out.astype(BF16)
    wa, wbt = wb[:RET_W], wb[RET_W:]
    row = lambda n: pl.BlockSpec((tm, n), lambda i: (i, 0))
    full = lambda a: pl.BlockSpec(a.shape, lambda i: (0, 0))
    return pl.pallas_call(
        _even_outproj_kernel,
        name="even_outproj",
        grid=(t // tm,),
        in_specs=[row(D_MODEL), row(RET_W), row(NSA_W), full(wa), full(wbt)],
        out_specs=row(D_MODEL),
        out_shape=jax.ShapeDtypeStruct((t, D_MODEL), F32),
        compiler_params=_params("arbitrary"),
    )(h, y_ret, y_nsa, wa, wbt)


def _odd_mixer_kernel(h_ref, g_ref, win_ref, cw_ref, cb_ref, wout_ref, o_ref, tail_ref):
    @pl.when(pl.program_id(1) == 0)
    def _():
        tail_ref[...] = jnp.zeros_like(tail_ref)

    h = h_ref[...]
    ts = h.shape[0]
    xn = _rms(h, g_ref[...]).astype(BF16)
    proj = jnp.dot(xn, win_ref[...], preferred_element_type=F32)
    gate_b, gate_c, hid = proj[:, 0:D_MODEL], proj[:, D_MODEL:2 * D_MODEL], proj[:, 2 * D_MODEL:3 * D_MODEL]
    u = gate_c * hid
    row = lax.broadcasted_iota(I32, (ts, D_MODEL), 0)
    tail = tail_ref[...]
    prev1, prev2 = tail[7:8, :], tail[6:7, :]
    u1 = jnp.where(row >= 1, pltpu.roll(u, 1, axis=0), prev1)
    u2 = jnp.where(row >= 2, pltpu.roll(u, 2, axis=0), jnp.where(row == 1, prev1, prev2))
    cw = cw_ref[...]
    y = cw[0:1, :] * u2 + cw[1:2, :] * u1 + cw[2:3, :] * u + cb_ref[...]
    tail_ref[...] = u[ts - 8:ts, :]
    z = (gate_b * y).astype(BF16)
    o_ref[...] = h + jnp.dot(z, wout_ref[...], preferred_element_type=F32)


def _odd_mixer(h, g, w_in, conv_w, conv_b, w_out, b, s):
    t = b * s
    ts = ROW_TILE
    n_s = s // ts
    win, wout = w_in.astype(BF16), w_out.astype(BF16)
    gg, cb = g[None, :], conv_b[None, :]
    cw = jnp.pad(conv_w, ((0, 8 - conv_w.shape[0]), (0, 0)))
    full = lambda a: pl.BlockSpec(a.shape, lambda bi, si: (0, 0))
    row = pl.BlockSpec((ts, D_MODEL), lambda bi, si: (bi * n_s + si, 0))
    return pl.pallas_call(
        _odd_mixer_kernel,
        name="odd_mixer",
        grid=(b, n_s),
        in_specs=[row, full(gg), full(win), full(cw), full(cb), full(wout)],
        out_specs=row,
        out_shape=jax.ShapeDtypeStruct((t, D_MODEL), F32),
        scratch_shapes=[pltpu.VMEM((8, D_MODEL), F32)],
        compiler_params=_params("arbitrary", "arbitrary"),
    )(h, gg, win, cw, cb, wout)


def _store_slabs(ref, x, stride):
    rows = x.shape[0]
    for j in range(SLAB):
        ref[pl.ds(j, rows, stride=stride), :] = x[:, j * LANES:(j + 1) * LANES]


def _load_slabs(ref, rows, stride, offset=0):
    return jnp.concatenate([ref[pl.ds(offset + j, rows, stride=stride), :] for j in range(SLAB)], axis=-1)


def _router_kernel(h_ref, g_ref, wr_ref, br_ref, xn_ref, ids_ref, gates_ref):
    xn = _rms(h_ref[...], g_ref[...])
    _store_slabs(xn_ref, xn, SLAB)
    logits = lax.dot_general(wr_ref[...], xn, (((1,), (1,)), ((), ())), precision=HIGHEST,
                             preferred_element_type=F32) + br_ref[...]
    row = lambda i: logits[i:i + 1, :]

    def softmax(xs):
        m = functools.reduce(jnp.maximum, xs)
        es = [jnp.exp(x - m) for x in xs]
        tot = functools.reduce(lambda a, c: a + c, es)
        return [e / tot for e in es]

    def argmax(ps):
        best_p, best_i = ps[0], jnp.zeros_like(ps[0], dtype=I32)
        for i in range(1, len(ps)):
            upd = ps[i] > best_p
            best_p = jnp.where(upd, ps[i], best_p)
            best_i = jnp.where(upd, i, best_i)
        return best_p, best_i

    grp_p, grp_i = argmax(softmax([row(i) for i in range(N_GROUPS)]))
    in_grp = []
    for e in range(EXPERTS_PER_GROUP):
        x = row(N_GROUPS + e)
        for gidx in range(1, N_GROUPS):
            x = jnp.where(grp_i == gidx, row(N_GROUPS + gidx * EXPERTS_PER_GROUP + e), x)
        in_grp.append(x)
    pe = softmax(in_grp)
    p1, i1 = argmax(pe)
    p2, i2 = argmax([jnp.where(i1 == e, -1.0, pe[e]) for e in range(EXPERTS_PER_GROUP)])
    tot = p1 + p2
    ids_ref[...] = jnp.concatenate([grp_i * EXPERTS_PER_GROUP + i1, grp_i * EXPERTS_PER_GROUP + i2], axis=0)
    gates_ref[...] = jnp.concatenate([grp_p * p1 / tot, grp_p * p2 / tot], axis=0)


def _router(h, g, wg_group, bg_group, wg_expert, bg_expert):
    t = h.shape[0]
    tm = ROW_TILE
    n_logit = N_GROUPS + N_EXPERTS
    wr = jnp.pad(jnp.concatenate([wg_group, wg_expert], axis=1).T, ((0, ROUTER_ROWS - n_logit), (0, 0)))
    br = jnp.pad(jnp.concatenate([bg_group, bg_expert]), (0, ROUTER_ROWS - n_logit))
    br = jnp.broadcast_to(br[:, None], (ROUTER_ROWS, tm))
    gg = g[None, :]
    full = lambda a: pl.BlockSpec(a.shape, lambda i: (0, 0))
    return pl.pallas_call(
        _router_kernel,
        name="moe_router",
        grid=(t // tm,),
        in_specs=[pl.BlockSpec((tm, D_MODEL), lambda i: (i, 0)), full(gg), full(wr), full(br)],
        out_specs=[pl.BlockSpec((tm * SLAB, LANES), lambda i: (i, 0)),
                   pl.BlockSpec((TOP_K_IN_GROUP, tm), lambda i: (0, i)),
                   pl.BlockSpec((TOP_K_IN_GROUP, tm), lambda i: (0, i))],
        out_shape=[jax.ShapeDtypeStruct((t * SLAB, LANES), F32), jax.ShapeDtypeStruct((TOP_K_IN_GROUP, t), I32),
                   jax.ShapeDtypeStruct((TOP_K_IN_GROUP, t), F32)],
        compiler_params=_params("arbitrary"),
    )(h, gg, wr, br)


def _expert_kernel(blk_expert_ref, n_used_ref, plan_hbm, x_hbm, wg_ref, wu_ref, wd_ref, y_hbm,
                   plan, xbuf, ybuf, sem_plan, sem_in, sem_out):
    del blk_expert_ref
    i = pl.program_id(0)
    n_used = n_used_ref[0]
    cur, prev, ahead = i % ROW_BUFS, (i + ROW_BUFS - 1) % ROW_BUFS, (i + 2) % ROW_BUFS

    def plan_copy(k):
        return pltpu.make_async_copy(plan_hbm.at[k], plan.at[k % PLAN_SLOTS], sem_plan.at[k % PLAN_SLOTS])

    def row_copy(hbm, hbm_row, buf, r, sem, to_hbm):
        h = hbm.at[pl.ds(pl.multiple_of(hbm_row * SLAB, SLAB), SLAB)]
        v = buf.at[pl.ds(r * SLAB, SLAB)]
        return pltpu.make_async_copy(v, h, sem) if to_hbm else pltpu.make_async_copy(h, v, sem)

    def wait_gather(s):
        for r in range(MOE_BLOCK):
            row_copy(x_hbm, 0, xbuf.at[s], r, sem_in.at[s], False).wait()

    def wait_scatter(s):
        for r in range(MOE_BLOCK):
            row_copy(y_hbm, 0, ybuf.at[s], r, sem_out.at[s], True).wait()

    @pl.when(i == 0)
    def _():
        ybuf[...] = jnp.zeros_like(ybuf)
        n_slot = y_hbm.shape[0] // SLAB - ROW_BUFS * MOE_BLOCK
        for s in range(ROW_BUFS - 1):
            for r in range(MOE_BLOCK):
                row_copy(y_hbm, n_slot + s * MOE_BLOCK + r, ybuf.at[s], r, sem_out.at[s], True).start()
        for s in range(ROW_BUFS - 1):
            wait_scatter(s)
        for k in range(4):
            plan_copy(k).start()
        for k in range(4):
            plan_copy(k).wait()
        for blk in range(2):
            for r in range(MOE_BLOCK):
                row_copy(x_hbm, plan[blk + 1, 0, r], xbuf.at[blk], r, sem_in.at[blk], False).start()

    @pl.when(i <= n_used)
    def _():
        @pl.when(i >= 1)
        def _():
            plan_copy(i + 3).wait()

        plan_copy(i + 4).start()
        wait_gather(cur)

        @pl.when(i >= 2)
        def _():
            wait_scatter(cur)

        nxt, prv = (i + 3) % PLAN_SLOTS, i % PLAN_SLOTS
        for r in range(MOE_BLOCK):
            row_copy(x_hbm, plan[nxt, 0, r], xbuf.at[ahead], r, sem_in.at[ahead], False).start(priority=0)
            row_copy(y_hbm, plan[prv, 0, MOE_BLOCK + r], ybuf.at[prev], r, sem_out.at[prev], True).start(priority=1)
        x = _load_slabs(xbuf.at[cur], MOE_BLOCK, SLAB).astype(BF16)
        a = jnp.dot(x, wg_ref[...].astype(BF16), preferred_element_type=F32)
        u = jnp.dot(x, wu_ref[...].astype(BF16), preferred_element_type=F32)
        hid = (a * _sigmoid(a) * u).astype(BF16)
        _store_slabs(ybuf.at[cur], jnp.dot(hid, wd_ref[...].astype(BF16), preferred_element_type=F32), SLAB)

        @pl.when(i == n_used)
        def _():
            plan_copy(i + 4).wait()
            wait_gather((i + 1) % ROW_BUFS)
            wait_gather(ahead)
            wait_scatter(prev)

            @pl.when(i >= 1)
            def _():
                wait_scatter((i + 1) % ROW_BUFS)


def _moe_plan(ids, t):
    n_slot = t * TOP_K_IN_GROUP
    n_blocks = n_slot // MOE_BLOCK + N_EXPERTS
    flat_e = ids.T.reshape(-1)
    order = jnp.argsort(flat_e).astype(I32)
    experts = jnp.arange(N_EXPERTS, dtype=I32)
    counts = jnp.sum(flat_e[:, None] == experts[None, :], axis=0, dtype=I32)
    n_blk = (counts + MOE_BLOCK - 1) // MOE_BLOCK
    blk_end = jnp.cumsum(n_blk)
    start = jnp.cumsum(counts) - counts
    blocks = jnp.arange(n_blocks, dtype=I32)
    blk_expert = jnp.minimum(jnp.sum(blk_end[None, :] <= blocks[:, None], axis=1, dtype=I32), N_EXPERTS - 1)
    row0 = (blocks - (blk_end - n_blk)[blk_expert]) * MOE_BLOCK
    rows = row0[:, None] + jnp.arange(MOE_BLOCK, dtype=I32)[None, :]
    is_slot = (rows >= 0) & (rows < counts[blk_expert][:, None])
    slot = order[jnp.clip(start[blk_expert][:, None] + rows, 0, n_slot - 1)]
    src = jnp.where(is_slot, slot // TOP_K_IN_GROUP, 0).astype(I32)
    dst = jnp.where(is_slot, slot, 0).astype(I32)
    valid = is_slot.astype(I32)
    n_rows = n_blocks + 6
    lead = lambda a: jnp.concatenate([jnp.zeros((1, MOE_BLOCK), I32), a, jnp.zeros((5, MOE_BLOCK), I32)])
    trash = (n_slot + (jnp.arange(n_rows, dtype=I32)[:, None] + ROW_BUFS - 1) % ROW_BUFS * MOE_BLOCK
             + jnp.arange(MOE_BLOCK, dtype=I32)[None, :])
    dst_rows = jnp.where(lead(valid) > 0, lead(dst), trash)
    plan = jnp.concatenate([lead(src), dst_rows], axis=1)
    return blk_expert, blk_end[-1:].astype(I32), plan[:, None, :]


def _experts(xn_slabs, ids, w_gate, w_up, w_down, layer):
    t = xn_slabs.shape[0] // SLAB
    blk_expert, n_used, plan = _moe_plan(ids, t)
    n_blocks = blk_expert.shape[0]
    w_spec = lambda a: pl.BlockSpec((None, None) + a.shape[2:], lambda i, be, nu: (layer, be[i], 0, 0))
    buf = pltpu.VMEM((ROW_BUFS, MOE_BLOCK * SLAB, LANES), F32)
    return pl.pallas_call(
        _expert_kernel,
        name="moe_experts",
        grid_spec=pltpu.PrefetchScalarGridSpec(
            num_scalar_prefetch=2,
            grid=(n_blocks,),
            in_specs=[pl.BlockSpec(memory_space=pl.ANY), pl.BlockSpec(memory_space=pl.ANY),
                      w_spec(w_gate), w_spec(w_up), w_spec(w_down)],
            out_specs=pl.BlockSpec(memory_space=pl.ANY),
            scratch_shapes=[pltpu.SMEM((PLAN_SLOTS, 1, 2 * MOE_BLOCK), I32), buf, buf,
                            pltpu.SemaphoreType.DMA((PLAN_SLOTS,)), pltpu.SemaphoreType.DMA((ROW_BUFS,)),
                            pltpu.SemaphoreType.DMA((ROW_BUFS,))]),
        out_shape=jax.ShapeDtypeStruct(((t * TOP_K_IN_GROUP + ROW_BUFS * MOE_BLOCK) * SLAB, LANES), F32),
        compiler_params=_params("arbitrary"),
    )(blk_expert, n_used, plan, xn_slabs, w_gate, w_up, w_down)


def _combine_kernel(h_ref, y_ref, gate_ref, o_ref):
    gate = gate_ref[...]
    rows = h_ref.shape[0]
    y0 = _load_slabs(y_ref, rows, TOP_K_IN_GROUP * SLAB)
    y1 = _load_slabs(y_ref, rows, TOP_K_IN_GROUP * SLAB, SLAB)
    o_ref[...] = h_ref[...] + gate[:, 0:1] * y0 + gate[:, 1:2] * y1


def _combine(h, y_slabs, gates):
    t = h.shape[0]
    tm = ROW_TILE
    return pl.pallas_call(
        _combine_kernel,
        name="moe_combine",
        grid=(t // tm,),
        in_specs=[pl.BlockSpec((tm, D_MODEL), lambda i: (i, 0)),
                  pl.BlockSpec((tm * TOP_K_IN_GROUP * SLAB, LANES), lambda i: (i, 0)),
                  pl.BlockSpec((tm, TOP_K_IN_GROUP), lambda i: (i, 0))],
        out_specs=pl.BlockSpec((tm, D_MODEL), lambda i: (i, 0)),
        out_shape=jax.ShapeDtypeStruct((t, D_MODEL), F32),
        compiler_params=_params("arbitrary"),
    )(h, y_slabs, gates.T)


def _moe(h, g, wg_group, bg_group, wg_expert, bg_expert, w_gate, w_up, w_down, layer):
    xn_slabs, ids, gates = _router(h, g, wg_group, bg_group, wg_expert, bg_expert)
    y_slabs = _experts(xn_slabs, ids, w_gate, w_up, w_down, layer)
    return _combine(h, y_slabs, gates)


def _even_mixer(h, g, w_in, cmp_pe, w_cmp1, w_cmp2, q_norm_g, k_norm_g, ret_norm_g, w_out, b, s):
    ret, nq, kcv, ks_aug, vs, kw_aug, vw, gates = _even_inproj(h, g, w_in, q_norm_g, k_norm_g, s)
    y_ret = _retention(ret, ret_norm_g, b, s)
    cmp_kv = _compress(kcv[:, :NSA_KV_W], kcv[:, NSA_KV_W:], cmp_pe, w_cmp1, w_cmp2, k_norm_g, b, s)
    y_nsa = _nsa_attention(nq, cmp_kv, ks_aug, vs, kw_aug, vw, gates, b, s)
    return _even_outproj(h, y_ret, y_nsa, w_out)


def kernel(x, mix_norm_g, ffn_norm_g, ev_w_in, ev_cmp_pe, ev_w_cmp1, ev_w_cmp2, ev_q_norm_g, ev_k_norm_g, ev_ret_norm_g, ev_w_out, od_w_in, od_conv_w, od_conv_b, od_w_out, moe_wg_group, moe_bg_group, moe_wg_expert, moe_bg_expert, moe_w_gate, moe_w_up, moe_w_down):
    b, s, d = x.shape
    h = x.reshape(b * s, d)
    for layer in range(mix_norm_g.shape[0]):
        i = layer // 2
        if layer % 2 == 0:
            h = _even_mixer(h, mix_norm_g[layer], ev_w_in[i], ev_cmp_pe[i], ev_w_cmp1[i], ev_w_cmp2[i],
                            ev_q_norm_g[i], ev_k_norm_g[i], ev_ret_norm_g[i], ev_w_out[i], b, s)
        else:
            h = _odd_mixer(h, mix_norm_g[layer], od_w_in[i], od_conv_w[i], od_conv_b[i], od_w_out[i], b, s)
        h = _moe(h, ffn_norm_g[layer], moe_wg_group[layer], moe_bg_group[layer], moe_wg_expert[layer],
                 moe_bg_expert[layer], moe_w_gate, moe_w_up, moe_w_down, layer)
    return h.reshape(b, s, d)
```

```python
import functools

import jax
import jax.numpy as jnp
import numpy as np
from jax import lax
from jax.experimental import pallas as pl
from jax.experimental.pallas import tpu as pltpu

F32 = jnp.float32
BF16 = jnp.bfloat16
I32 = jnp.int32
HIGHEST = lax.Precision.HIGHEST

D_MODEL = 1024
HEAD_DIM = 64
RET_HEADS = 8
NSA_HEADS = 8
NSA_KV_GROUPS = 2
NSA_GROUP_SIZE = NSA_HEADS // NSA_KV_GROUPS
RET_W = RET_HEADS * HEAD_DIM
NSA_W = NSA_HEADS * HEAD_DIM
NSA_KV_W = NSA_KV_GROUPS * HEAD_DIM
N_BRANCHES = 3
RET_CHUNK = 128
CMP_STRIDE = 16
CMP_BLOCK = 2 * CMP_STRIDE
CMP_HIDDEN = 128
SEL_BLOCK = 64
TOP_N = 8
WINDOW = 256
N_GROUPS = 4
EXPERTS_PER_GROUP = 8
N_EXPERTS = N_GROUPS * EXPERTS_PER_GROUP
TOP_K_IN_GROUP = 2
D_EXPERT = 256
RMS_EPS = 1e-6
NEG_INF = -1e30
ATTN_SCALE = HEAD_DIM ** -0.5

LANES = 128
VMEM_LIMIT = 48 * 1024 * 1024
ROW_TILE = 512
NSA_Q_TILE = 256
NSA_KV_CHUNK = 512
MOE_BLOCK = 256
ROUTER_ROWS = 40
SLAB = D_MODEL // LANES
PLAN_SLOTS = 8
ROW_BUFS = 4


def _params(*sem):
    return pltpu.CompilerParams(dimension_semantics=sem, vmem_limit_bytes=VMEM_LIMIT)


def _rms(x, g):
    return x * lax.rsqrt(jnp.mean(x * x, axis=-1, keepdims=True) + RMS_EPS) * g


def _group_rms(x, bd, gain):
    sq = x * x
    hi = sq.astype(BF16)
    lo = (sq - hi.astype(F32)).astype(BF16)
    ms = jnp.dot(hi, bd, preferred_element_type=F32) + jnp.dot(lo, bd, preferred_element_type=F32)
    return x * lax.rsqrt(ms + RMS_EPS) * gain


def _sigmoid(x):
    return 1.0 / (1.0 + jnp.exp(-x))


def _block_diag_mean(n):
    idx = np.arange(n) // HEAD_DIM
    return jnp.asarray((idx[:, None] == idx[None, :]).astype(np.float32) / HEAD_DIM, BF16)


def _even_inproj_kernel(x_ref, g_ref, wret_ref, wnq_ref, wkv_ref, wng_ref, qgain_ref, kgain_ref, bd512_ref, bd128_ref,
                        ret_ref, nq_ref, chunk_ref, ks_ref, vs_ref, kw_ref, vw_ref, gate_ref, kcv_ref, *, seq_tiles):
    tm = x_ref.shape[0]
    xn = _rms(x_ref[...], g_ref[...]).astype(BF16)
    ret_ref[...] = jnp.dot(xn, wret_ref[...], preferred_element_type=F32).astype(BF16)
    nq = jnp.dot(xn, wnq_ref[...], preferred_element_type=F32)
    nq_ref[...] = _group_rms(nq, bd512_ref[...], qgain_ref[...]).astype(BF16)
    kv = jnp.dot(xn, wkv_ref[...], preferred_element_type=F32)
    w = NSA_KV_W
    n_chunk = tm // CMP_STRIDE
    for kv_i in range(2):
        kcv_ref[kv_i] = kv[:, kv_i * w:(kv_i + 1) * w]
        for j in range(CMP_STRIDE // 2):
            even = kcv_ref[kv_i, pl.ds(2 * j, n_chunk, stride=CMP_STRIDE), :]
            odd = kcv_ref[kv_i, pl.ds(2 * j + 1, n_chunk, stride=CMP_STRIDE), :]
            for g in range(NSA_KV_GROUPS):
                sl = slice(g * HEAD_DIM, (g + 1) * HEAD_DIM)
                pair = jnp.concatenate([even[:, sl], odd[:, sl]], axis=-1).astype(BF16)
                chunk_ref[kv_i, g, :, j * LANES:(j + 1) * LANES] = pair
    ks = _group_rms(kv[:, 2 * w:3 * w], bd128_ref[...], kgain_ref[0:1, :])
    kw = _group_rms(kv[:, 4 * w:5 * w], bd128_ref[...], kgain_ref[1:2, :])
    vs, vw = kv[:, 3 * w:4 * w], kv[:, 5 * w:6 * w]
    pos = (pl.program_id(0) % seq_tiles) * tm + lax.broadcasted_iota(I32, (tm, HEAD_DIM), 0)
    lane = lax.broadcasted_iota(I32, (tm, HEAD_DIM), 1)
    blk = lax.shift_right_arithmetic(pos, int(np.log2(SEL_BLOCK)))
    feat_w = jnp.where(lane == 0, blk, jnp.where(lane == 1, pos & (SEL_BLOCK - 1), 0)).astype(F32)
    feat_s = feat_w + jnp.where(lane == blk + 2, 1.0, 0.0)
    ones_col = jnp.where(lane == 0, 1.0, 0.0)
    for g in range(NSA_KV_GROUPS):
        sl = slice(g * HEAD_DIM, (g + 1) * HEAD_DIM)
        ks_ref[g] = jnp.concatenate([ks[:, sl], feat_s], axis=-1).astype(BF16)
        kw_ref[g] = jnp.concatenate([kw[:, sl], feat_w], axis=-1).astype(BF16)
        vs_ref[g] = jnp.concatenate([vs[:, sl], ones_col], axis=-1).astype(BF16)
        vw_ref[g] = jnp.concatenate([vw[:, sl], ones_col], axis=-1).astype(BF16)
    ng = jnp.dot(xn, wng_ref[...], preferred_element_type=F32)
    for g in range(NSA_KV_GROUPS):
        gate_ref[g] = ng[:, g * LANES:(g + 1) * LANES]


def _even_inproj(h, g, w_in, q_norm_g, k_norm_g, s):
    t = h.shape[0]
    tm = ROW_TILE
    c_ret, c_nq, c_kv = 4 * RET_W, 4 * RET_W + NSA_W, 4 * RET_W + NSA_W + 6 * NSA_KV_W
    n_gate = NSA_GROUP_SIZE * N_BRANCHES
    flat = CMP_STRIDE * HEAD_DIM
    wb = w_in.astype(BF16)
    wret, wnq, wkv = wb[:, :c_ret], wb[:, c_ret:c_nq], wb[:, c_nq:c_kv]
    wng = jnp.concatenate([jnp.pad(wb[:, c_kv + gi * n_gate:c_kv + (gi + 1) * n_gate], ((0, 0), (0, LANES - n_gate)))
                           for gi in range(NSA_KV_GROUPS)], axis=1)
    qgain = jnp.tile(q_norm_g, NSA_HEADS)[None, :]
    kgain = jnp.stack([jnp.tile(k_norm_g[1], NSA_KV_GROUPS), jnp.tile(k_norm_g[2], NSA_KV_GROUPS)])
    full = lambda a: pl.BlockSpec(a.shape, lambda i: (0,) * a.ndim)
    row = lambda n: pl.BlockSpec((tm, n), lambda i: (i, 0))
    grp = lambda n: pl.BlockSpec((NSA_KV_GROUPS, tm, n), lambda i: (0, i, 0))
    grp_shape = lambda n, dt: jax.ShapeDtypeStruct((NSA_KV_GROUPS, t, n), dt)
    bd512, bd128 = _block_diag_mean(NSA_W), _block_diag_mean(NSA_KV_W)
    gg = g[None, :]
    return pl.pallas_call(
        functools.partial(_even_inproj_kernel, seq_tiles=s // tm),
        name="even_inproj",
        grid=(t // tm,),
        in_specs=[row(D_MODEL), full(gg), full(wret), full(wnq), full(wkv), full(wng), full(qgain),
                  full(kgain), full(bd512), full(bd128)],
        out_specs=[row(c_ret), row(NSA_W),
                   pl.BlockSpec((2, NSA_KV_GROUPS, tm // CMP_STRIDE, flat), lambda i: (0, 0, i, 0)),
                   grp(LANES), grp(LANES), grp(LANES), grp(LANES), grp(LANES)],
        out_shape=[jax.ShapeDtypeStruct((t, c_ret), BF16), jax.ShapeDtypeStruct((t, NSA_W), BF16),
                   jax.ShapeDtypeStruct((2, NSA_KV_GROUPS, t // CMP_STRIDE, flat), BF16), grp_shape(LANES, BF16),
                   grp_shape(LANES, BF16), grp_shape(LANES, BF16), grp_shape(LANES, BF16), grp_shape(LANES, F32)],
        scratch_shapes=[pltpu.VMEM((2, tm, NSA_KV_W), F32)],
        compiler_params=_params("arbitrary"),
    )(h, gg, wret, wnq, wkv, wng, qgain, kgain, bd512, bd128)


def _retention_kernel(q_ref, k_ref, v_ref, rg_ref, decay_ref, qd_ref, kd_ref, cd_ref, gain_ref, o_ref):
    n_chunks = q_ref.shape[0] // RET_CHUNK
    gain = gain_ref[...]
    nt = (((1,), (1,)), ((), ()))
    tn = (((0,), (0,)), ((), ()))

    def body(c, states):
        r0 = pl.multiple_of(c * RET_CHUNK, RET_CHUNK)
        rows = pl.ds(r0, RET_CHUNK)
        q2, k2, v2 = q_ref[rows, :], k_ref[rows, :], v_ref[rows, :]
        rg2 = rg_ref[rows, :].astype(F32)
        outs, new_states = [], []
        for hh in range(2):
            lanes = slice(hh * HEAD_DIM, (hh + 1) * HEAD_DIM)
            q, v = q2[:, lanes], v2[:, lanes]
            kf = k2[:, lanes].astype(F32) * ATTN_SCALE
            state = states[hh]
            s = lax.dot_general(q, kf.astype(BF16), nt, preferred_element_type=F32) * decay_ref[hh]
            y = jnp.dot(s.astype(BF16), v, preferred_element_type=F32)
            qd = (q.astype(F32) * qd_ref[hh]).astype(BF16)
            y = y + jnp.dot(qd, state.astype(BF16), preferred_element_type=F32)
            kd = (kf * kd_ref[hh]).astype(BF16)
            new_states.append(cd_ref[hh] * state + lax.dot_general(kd, v, tn, preferred_element_type=F32))
            rg = rg2[:, lanes]
            outs.append(_rms(y, gain) * (rg * _sigmoid(rg)))
        o_ref[rows, :] = jnp.concatenate(outs, axis=-1).astype(BF16)
        return tuple(new_states)

    zero = jnp.zeros((HEAD_DIM, HEAD_DIM), F32)
    lax.fori_loop(0, n_chunks, body, (zero, zero), unroll=True)


def _retention_tables():
    h = np.arange(RET_HEADS, dtype=np.float64)
    log_g = np.log(1.0 - 2.0 ** (-5.0 - h))
    pos = np.arange(RET_CHUNK, dtype=np.float64)
    diff = pos[:, None] - pos[None, :]
    decay = np.where(diff >= 0, np.exp(log_g[:, None, None] * np.maximum(diff, 0.0)), 0.0)
    qd = np.exp(log_g[:, None] * (pos + 1.0))[:, :, None] * np.ones((1, 1, HEAD_DIM))
    kd = np.exp(log_g[:, None] * (RET_CHUNK - 1 - pos))[:, :, None] * np.ones((1, 1, HEAD_DIM))
    cd = np.exp(log_g * RET_CHUNK)[:, None, None] * np.ones((1, HEAD_DIM, HEAD_DIM))
    return [jnp.asarray(a, F32) for a in (decay, qd, kd, cd)]


def _retention(ret, ret_norm_g, b, s):
    t = b * s
    n_pairs = RET_HEADS // 2
    decay, qd, kd, cd = _retention_tables()
    col = lambda off: pl.BlockSpec((s, LANES), lambda bi, p: (bi, off * n_pairs + p))
    tab = lambda a: pl.BlockSpec((2,) + a.shape[1:], lambda bi, p: (p, 0, 0))
    gain = ret_norm_g[None, :]
    return pl.pallas_call(
        _retention_kernel,
        name="retention",
        grid=(b, n_pairs),
        in_specs=[col(0), col(1), col(2), col(3), tab(decay), tab(qd), tab(kd), tab(cd),
                  pl.BlockSpec(gain.shape, lambda bi, p: (0, 0))],
        out_specs=pl.BlockSpec((s, LANES), lambda bi, p: (bi, p)),
        out_shape=jax.ShapeDtypeStruct((t, RET_W), BF16),
        compiler_params=_params("arbitrary", "arbitrary"),
    )(ret, ret, ret, ret, decay, qd, kd, cd, gain)


def _compress_kernel(x_ref, pe_ref, w1_ref, w2_ref, kgain_ref, o_ref):
    is_key = pl.program_id(0) == 0
    half = CMP_STRIDE * HEAD_DIM
    x = x_ref[...]
    n = x.shape[0]
    a = jnp.dot(x, w1_ref[0:half, :], preferred_element_type=F32)
    bm = jnp.dot(x, w1_ref[half:2 * half, :], preferred_element_type=F32)
    pew = jnp.dot(pe_ref[...], w1_ref[...], preferred_element_type=F32)[0:1, :]
    pre = a + pltpu.roll(bm, n - 1, axis=0) + pew
    hid = 0.5 * pre * (1.0 + jnp.tanh(np.sqrt(2.0 / np.pi) * (pre + 0.044715 * pre * pre * pre)))
    out = jnp.dot(hid.astype(BF16), w2_ref[...], preferred_element_type=F32)
    normed = _rms(out, kgain_ref[...])
    o_ref[...] = jnp.where(is_key, normed, out).astype(BF16)


def _compress(chunks, cmp_pe, w_cmp1, w_cmp2, k_norm_g, b, s):
    n = s // CMP_STRIDE
    g = NSA_KV_GROUPS
    flat = CMP_STRIDE * HEAD_DIM
    pe = jnp.broadcast_to(cmp_pe.reshape(2, 1, 2 * flat), (2, 8, 2 * flat)).astype(BF16)
    w1, w2 = w_cmp1.astype(BF16), w_cmp2.astype(BF16)
    kgain = k_norm_g[0][None, :]
    return pl.pallas_call(
        _compress_kernel,
        name="nsa_compress",
        grid=(2, b, g),
        in_specs=[pl.BlockSpec((None, None, n, flat), lambda kv, bi, gi: (kv, gi, bi, 0)),
                  pl.BlockSpec((None, 8, 2 * flat), lambda kv, bi, gi: (kv, 0, 0)),
                  pl.BlockSpec((None, 2 * flat, CMP_HIDDEN), lambda kv, bi, gi: (kv, 0, 0)),
                  pl.BlockSpec((None, CMP_HIDDEN, HEAD_DIM), lambda kv, bi, gi: (kv, 0, 0)),
                  pl.BlockSpec(kgain.shape, lambda kv, bi, gi: (0, 0))],
        out_specs=pl.BlockSpec((None, None, None, n, HEAD_DIM), lambda kv, bi, gi: (kv, bi, gi, 0, 0)),
        out_shape=jax.ShapeDtypeStruct((2, b, g, n, HEAD_DIM), BF16),
        compiler_params=_params("arbitrary", "arbitrary", "arbitrary"),
    )(chunks, pe, w1, w2, kgain)


def _nsa_kernel(slopes_ref, q_ref, kcmp_ref, vcmp_ref, ks_ref, vs_ref, kw_ref, vw_ref, gate_ref, ovl_ref, place_ref,
                o_ref, *, n_sel, n_cmp):
    r_heads = NSA_GROUP_SIZE
    tq = q_ref.shape[0]
    gi = pl.program_id(1)
    t0 = pl.program_id(2) * tq
    nt = (((1,), (1,)), ((), ()))
    tn = (((0,), (0,)), ((), ()))

    q = q_ref[...]
    lane = lax.broadcasted_iota(I32, (tq, HEAD_DIM), 1)
    qs, feats = [], []
    for r in range(r_heads):
        slope = slopes_ref[gi * r_heads + r]
        qs.append((q[:, r * HEAD_DIM:(r + 1) * HEAD_DIM].astype(F32) * ATTN_SCALE).astype(BF16))
        feats.append(jnp.where(lane == 0, slope * SEL_BLOCK, jnp.where(lane == 1, slope, 0.0)))

    def stack_q(extra):
        return jnp.concatenate([jnp.concatenate([qs[r], (feats[r] + extra).astype(BF16)], axis=-1)
                                for r in range(r_heads)], axis=0)

    q4 = stack_q(0.0)
    t_col = t0 + lax.broadcasted_iota(I32, (tq, 1), 0)

    def head_rows(x):
        return [x[r * tq:(r + 1) * tq] for r in range(r_heads)]

    n_pad = kcmp_ref.shape[0]
    c_idx = lax.broadcasted_iota(I32, (tq, n_pad), 1)
    mask_c = (t_col >= c_idx * CMP_STRIDE + (CMP_BLOCK - 1)) & (c_idx < n_cmp)
    s4 = lax.dot_general(q4, kcmp_ref[...], nt, preferred_element_type=F32)
    ps = []
    for s in head_rows(s4):
        s = jnp.where(mask_c, s, NEG_INF)
        p = jnp.where(mask_c, jnp.exp(s - jnp.max(s, axis=-1, keepdims=True)), 0.0)
        ps.append(p * (1.0 / jnp.maximum(jnp.sum(p, axis=-1, keepdims=True), 1e-30)))
    p4 = jnp.concatenate(ps, axis=0).astype(BF16)
    o_cmp = head_rows(jnp.dot(p4, vcmp_ref[...], preferred_element_type=F32))

    n_rows = ovl_ref.shape[0]
    imp4 = lax.dot_general(ovl_ref[...], p4, nt, preferred_element_type=F32)
    imp = imp4[:, 0:tq]
    for r in range(1, r_heads):
        imp = imp + imp4[:, r * tq:(r + 1) * tq]
    j_idx = lax.broadcasted_iota(I32, (n_rows, tq), 0)
    t_row = t0 + lax.broadcasted_iota(I32, (n_rows, tq), 1)
    q_blk = lax.shift_right_arithmetic(t_row, int(np.log2(SEL_BLOCK)))
    valid = (j_idx * SEL_BLOCK <= t_row) & (j_idx < n_sel)
    forced = (j_idx == 0) | (j_idx == q_blk) | (j_idx == q_blk - 1)
    val = jnp.where(j_idx < n_sel, jnp.where(forced, imp + 1e3, jnp.where(valid, imp, -1e3)), -2e3)
    rank = jnp.zeros((n_rows, tq), F32)
    for k in range(n_sel):
        vk = val[k:k + 1, :]
        rank = rank + jnp.where(vk > val, 1.0, jnp.where(vk == val, jnp.where(j_idx > k, 1.0, 0.0), 0.0))
    unselected = jnp.where(valid, jnp.where(rank < TOP_N, 0.0, NEG_INF), NEG_INF).astype(BF16)
    q4_sel = stack_q(lax.dot_general(unselected, place_ref[...], tn, preferred_element_type=F32))

    ck = NSA_KV_CHUNK

    def sel_step(c, carry, causal):
        c0 = pl.multiple_of(c * ck, ck)
        s4 = lax.dot_general(q4_sel, ks_ref[pl.ds(c0, ck), :], nt, preferred_element_type=F32)
        if causal:
            visible = t_col >= c0 + lax.broadcasted_iota(I32, (tq, ck), 1)
        ps, new = [], []
        for r, s in enumerate(head_rows(s4)):
            if causal:
                s = jnp.where(visible, s, NEG_INF)
            m_old, acc_old = carry[r]
            m_new = jnp.maximum(m_old, jnp.max(s, axis=-1, keepdims=True))
            ps.append(jnp.exp(s - m_new))
            new.append((m_new, jnp.exp(m_old - m_new) * acc_old))
        pv = jnp.dot(jnp.concatenate(ps, axis=0).astype(BF16), vs_ref[pl.ds(c0, ck), :], preferred_element_type=F32)
        return tuple((m, acc + o) for (m, acc), o in zip(new, head_rows(pv)))

    def normalise(acc):
        return acc[:, 0:HEAD_DIM] * (1.0 / acc[:, HEAD_DIM:HEAD_DIM + 1])

    init = tuple((jnp.full((tq, 1), NEG_INF, F32), jnp.zeros((tq, LANES), F32)) for _ in range(r_heads))
    c_last = t0 // ck
    carry = lax.fori_loop(0, c_last, lambda c, carry: sel_step(c, carry, False), init)
    o_sel = [normalise(acc) for (_, acc) in sel_step(c_last, carry, True)]

    n_win = WINDOW + tq
    w0 = pl.multiple_of(jnp.maximum(t0 - WINDOW, 0), tq)
    dist_w = t_col - (w0 + lax.broadcasted_iota(I32, (tq, n_win), 1))
    mask_w = (dist_w >= 0) & (dist_w < WINDOW)
    s4 = lax.dot_general(q4, kw_ref[pl.ds(w0, n_win), :], nt, preferred_element_type=F32)
    ps = []
    for s in head_rows(s4):
        s = jnp.where(mask_w, s, NEG_INF)
        ps.append(jnp.exp(s - jnp.max(s, axis=-1, keepdims=True)))
    pv = jnp.dot(jnp.concatenate(ps, axis=0).astype(BF16), vw_ref[pl.ds(w0, n_win), :], preferred_element_type=F32)
    o_win = [normalise(o) for o in head_rows(pv)]

    gate = _sigmoid(gate_ref[...])
    outs = []
    for r in range(r_heads):
        c = N_BRANCHES * r
        outs.append(gate[:, c:c + 1] * o_cmp[r] + gate[:, c + 1:c + 2] * o_sel[r] + gate[:, c + 2:c + 3] * o_win[r])
    o_ref[...] = jnp.concatenate(outs, axis=-1).astype(BF16)


def _nsa_attention(nq, cmp_kv, ks_aug, vs, kw_aug, vw, gates, b, s):
    t = b * s
    g, r = NSA_KV_GROUPS, NSA_GROUP_SIZE
    tq = NSA_Q_TILE
    n_q = s // tq
    n_sel = s // SEL_BLOCK
    n_pad = s // CMP_STRIDE
    n_cmp = n_pad - 1
    n_rows = -(-n_sel // 8) * 8
    assert 2 + n_sel <= HEAD_DIM and s % NSA_KV_CHUNK == 0 and s >= WINDOW + tq and NSA_KV_CHUNK % tq == 0
    slopes = jnp.asarray(2.0 ** (-8.0 * np.arange(1, NSA_HEADS + 1) / NSA_HEADS), F32)
    cmp_start = np.arange(n_pad) * CMP_STRIDE
    cmp_end = cmp_start + CMP_BLOCK - 1
    feat = np.zeros((n_pad, HEAD_DIM), np.float32)
    feat[:, 0], feat[:, 1] = cmp_end // SEL_BLOCK, cmp_end % SEL_BLOCK
    kcmp_aug = jnp.concatenate([cmp_kv[0], jnp.broadcast_to(jnp.asarray(feat, BF16), cmp_kv[0].shape)], axis=-1)
    sel_start = np.arange(n_rows) * SEL_BLOCK
    ovl = ((cmp_start[None, :] < sel_start[:, None] + SEL_BLOCK) & (cmp_start[None, :] + CMP_BLOCK > sel_start[:, None])
           & (np.arange(n_pad)[None, :] < n_cmp) & (np.arange(n_rows)[:, None] < n_sel))
    ovl = jnp.asarray(ovl.astype(np.float32), BF16)
    place = jnp.asarray((np.arange(n_rows)[:, None] + 2 == np.arange(HEAD_DIM)[None, :]).astype(np.float32), BF16)
    seq = lambda w: pl.BlockSpec((None, s, w), lambda bi, gi, qi: (gi, bi, 0))
    cmp = lambda w: pl.BlockSpec((None, None, n_pad, w), lambda bi, gi, qi: (bi, gi, 0, 0))
    return pl.pallas_call(
        functools.partial(_nsa_kernel, n_sel=n_sel, n_cmp=n_cmp),
        name="nsa_attention",
        grid=(b, g, n_q),
        in_specs=[pl.BlockSpec(memory_space=pltpu.SMEM),
                  pl.BlockSpec((tq, r * HEAD_DIM), lambda bi, gi, qi: (bi * n_q + qi, gi)),
                  cmp(2 * HEAD_DIM), cmp(HEAD_DIM), seq(LANES), seq(LANES), seq(LANES), seq(LANES),
                  pl.BlockSpec((None, tq, LANES), lambda bi, gi, qi: (gi, bi * n_q + qi, 0)),
                  pl.BlockSpec(ovl.shape, lambda bi, gi, qi: (0, 0)),
                  pl.BlockSpec(place.shape, lambda bi, gi, qi: (0, 0))],
        out_specs=pl.BlockSpec((tq, r * HEAD_DIM), lambda bi, gi, qi: (bi * n_q + qi, gi)),
        out_shape=jax.ShapeDtypeStruct((t, NSA_W), BF16),
        compiler_params=_params("arbitrary", "arbitrary", "arbitrary"),
    )(slopes, nq, kcmp_aug, cmp_kv[1], ks_aug, vs, kw_aug, vw, gates, ovl, place)


def _even_outproj_kernel(h_ref, a_ref, b_ref, wa_ref, wb_ref, o_ref):
    y = jnp.dot(a_ref[...], wa_ref[...], preferred_element_type=F32)
    y = y + jnp.dot(b_ref[...], wb_ref[...], preferred_element_type=F32)
    o_ref[...] = h_ref[...] + y


def _even_outproj(h, y_ret, y_nsa, w_out):
    t = h.shape[0]
    tm = ROW_TILE
    wb = w_out.astype(BF16)
    wa, wbt = wb[:RET_W], wb[RET_W:]
    row = lambda n: pl.BlockSpec((tm, n), lambda i: (i, 0))
    full = lambda a: pl.BlockSpec(a.shape, lambda i: (0, 0))
    return pl.pallas_call(
        _even_outproj_kernel,
        name="even_outproj",
        grid=(t // tm,),
        in_specs=[row(D_MODEL), row(RET_W), row(NSA_W), full(wa), full(wbt)],
        out_specs=row(D_MODEL),
        out_shape=jax.ShapeDtypeStruct((t, D_MODEL), F32),
        compiler_params=_params("arbitrary"),
    )(h, y_ret, y_nsa, wa, wbt)


def _odd_mixer_kernel(h_ref, g_ref, win_ref, cw_ref, cb_ref, wout_ref, o_ref, tail_ref):
    @pl.when(pl.program_id(1) == 0)
    def _():
        tail_ref[...] = jnp.zeros_like(tail_ref)

    h = h_ref[...]
    ts = h.shape[0]
    xn = _rms(h, g_ref[...]).astype(BF16)
    proj = jnp.dot(xn, win_ref[...], preferred_element_type=F32)
    gate_b, gate_c, hid = proj[:, 0:D_MODEL], proj[:, D_MODEL:2 * D_MODEL], proj[:, 2 * D_MODEL:3 * D_MODEL]
    u = gate_c * hid
    row = lax.broadcasted_iota(I32, (ts, D_MODEL), 0)
    tail = tail_ref[...]
    prev1, prev2 = tail[7:8, :], tail[6:7, :]
    u1 = jnp.where(row >= 1, pltpu.roll(u, 1, axis=0), prev1)
    u2 = jnp.where(row >= 2, pltpu.roll(u, 2, axis=0), jnp.where(row == 1, prev1, prev2))
    cw = cw_ref[...]
    y = cw[0:1, :] * u2 + cw[1:2, :] * u1 + cw[2:3, :] * u + cb_ref[...]
    tail_ref[...] = u[ts - 8:ts, :]
    z = (gate_b * y).astype(BF16)
    o_ref[...] = h + jnp.dot(z, wout_ref[...], preferred_element_type=F32)


def _odd_mixer(h, g, w_in, conv_w, conv_b, w_out, b, s):
    t = b * s
    ts = ROW_TILE
    n_s = s // ts
    win, wout = w_in.astype(BF16), w_out.astype(BF16)
    gg, cb = g[None, :], conv_b[None, :]
    cw = jnp.pad(conv_w, ((0, 8 - conv_w.shape[0]), (0, 0)))
    full = lambda a: pl.BlockSpec(a.shape, lambda bi, si: (0, 0))
    row = pl.BlockSpec((ts, D_MODEL), lambda bi, si: (bi * n_s + si, 0))
    return pl.pallas_call(
        _odd_mixer_kernel,
        name="odd_mixer",
        grid=(b, n_s),
        in_specs=[row, full(gg), full(win), full(cw), full(cb), full(wout)],
        out_specs=row,
        out_shape=jax.ShapeDtypeStruct((t, D_MODEL), F32),
        scratch_shapes=[pltpu.VMEM((8, D_MODEL), F32)],
        compiler_params=_params("arbitrary", "arbitrary"),
    )(h, gg, win, cw, cb, wout)


def _store_slabs(ref, x, stride):
    rows = x.shape[0]
    for j in range(SLAB):
        ref[pl.ds(j, rows, stride=stride), :] = x[:, j * LANES:(j + 1) * LANES]


def _load_slabs(ref, rows, stride, offset=0):
    return jnp.concatenate([ref[pl.ds(offset + j, rows, stride=stride), :] for j in range(SLAB)], axis=-1)


def _router_kernel(h_ref, g_ref, wr_ref, br_ref, xn_ref, ids_ref, gates_ref):
    xn = _rms(h_ref[...], g_ref[...])
    _store_slabs(xn_ref, xn, SLAB)
    logits = lax.dot_general(wr_ref[...], xn, (((1,), (1,)), ((), ())), precision=HIGHEST,
                             preferred_element_type=F32) + br_ref[...]
    row = lambda i: logits[i:i + 1, :]

    def softmax(xs):
        m = functools.reduce(jnp.maximum, xs)
        es = [jnp.exp(x - m) for x in xs]
        tot = functools.reduce(lambda a, c: a + c, es)
        return [e / tot for e in es]

    def argmax(ps):
        best_p, best_i = ps[0], jnp.zeros_like(ps[0], dtype=I32)
        for i in range(1, len(ps)):
            upd = ps[i] > best_p
            best_p = jnp.where(upd, ps[i], best_p)
            best_i = jnp.where(upd, i, best_i)
        return best_p, best_i

    grp_p, grp_i = argmax(softmax([row(i) for i in range(N_GROUPS)]))
    in_grp = []
    for e in range(EXPERTS_PER_GROUP):
        x = row(N_GROUPS + e)
        for gidx in range(1, N_GROUPS):
            x = jnp.where(grp_i == gidx, row(N_GROUPS + gidx * EXPERTS_PER_GROUP + e), x)
        in_grp.append(x)
    pe = softmax(in_grp)
    p1, i1 = argmax(pe)
    p2, i2 = argmax([jnp.where(i1 == e, -1.0, pe[e]) for e in range(EXPERTS_PER_GROUP)])
    tot = p1 + p2
    ids_ref[...] = jnp.concatenate([grp_i * EXPERTS_PER_GROUP + i1, grp_i * EXPERTS_PER_GROUP + i2], axis=0)
    gates_ref[...] = jnp.concatenate([grp_p * p1 / tot, grp_p * p2 / tot], axis=0)


def _router(h, g, wg_group, bg_group, wg_expert, bg_expert):
    t = h.shape[0]
    tm = ROW_TILE
    n_logit = N_GROUPS + N_EXPERTS
    wr = jnp.pad(jnp.concatenate([wg_group, wg_expert], axis=1).T, ((0, ROUTER_ROWS - n_logit), (0, 0)))
    br = jnp.pad(jnp.concatenate([bg_group, bg_expert]), (0, ROUTER_ROWS - n_logit))
    br = jnp.broadcast_to(br[:, None], (ROUTER_ROWS, tm))
    gg = g[None, :]
    full = lambda a: pl.BlockSpec(a.shape, lambda i: (0, 0))
    return pl.pallas_call(
        _router_kernel,
        name="moe_router",
        grid=(t // tm,),
        in_specs=[pl.BlockSpec((tm, D_MODEL), lambda i: (i, 0)), full(gg), full(wr), full(br)],
        out_specs=[pl.BlockSpec((tm * SLAB, LANES), lambda i: (i, 0)),
                   pl.BlockSpec((TOP_K_IN_GROUP, tm), lambda i: (0, i)),
                   pl.BlockSpec((TOP_K_IN_GROUP, tm), lambda i: (0, i))],
        out_shape=[jax.ShapeDtypeStruct((t * SLAB, LANES), F32), jax.ShapeDtypeStruct((TOP_K_IN_GROUP, t), I32),
                   jax.ShapeDtypeStruct((TOP_K_IN_GROUP, t), F32)],
        compiler_params=_params("arbitrary"),
    )(h, gg, wr, br)


def _expert_kernel(blk_expert_ref, n_used_ref, plan_hbm, x_hbm, wg_ref, wu_ref, wd_ref, y_hbm,
                   plan, xbuf, ybuf, sem_plan, sem_in, sem_out):
    del blk_expert_ref
    depth = ROW_BUFS
    i = pl.program_id(0)
    n_used = n_used_ref[0]
    cur, prev, ahead = i % depth, (i + depth - 1) % depth, (i + depth - 1) % depth

    def plan_copy(k):
        return pltpu.make_async_copy(plan_hbm.at[k], plan.at[k % PLAN_SLOTS], sem_plan.at[k % PLAN_SLOTS])

    def row_copy(hbm, hbm_row, buf, r, sem, to_hbm):
        h = hbm.at[pl.ds(pl.multiple_of(hbm_row * SLAB, SLAB), SLAB)]
        v = buf.at[pl.ds(r * SLAB, SLAB)]
        return pltpu.make_async_copy(v, h, sem) if to_hbm else pltpu.make_async_copy(h, v, sem)

    def wait_gather(s):
        for r in range(MOE_BLOCK):
            row_copy(x_hbm, 0, xbuf.at[s], r, sem_in.at[s], False).wait()

    def wait_scatter(s):
        for r in range(MOE_BLOCK):
            row_copy(y_hbm, 0, ybuf.at[s], r, sem_out.at[s], True).wait()

    @pl.when(i == 0)
    def _():
        ybuf[...] = jnp.zeros_like(ybuf)
        n_slot = y_hbm.shape[0] // SLAB - depth * MOE_BLOCK
        for s in range(depth - 1):
            for r in range(MOE_BLOCK):
                row_copy(y_hbm, n_slot + s * MOE_BLOCK + r, ybuf.at[s], r, sem_out.at[s], True).start()
        for s in range(depth - 1):
            wait_scatter(s)
        for k in range(depth + 1):
            plan_copy(k).start()
        for k in range(depth + 1):
            plan_copy(k).wait()
        for blk in range(depth - 1):
            for r in range(MOE_BLOCK):
                row_copy(x_hbm, plan[blk + 1, 0, r], xbuf.at[blk], r, sem_in.at[blk], False).start()

    @pl.when(i <= n_used)
    def _():
        @pl.when(i >= 1)
        def _():
            plan_copy(i + depth).wait()

        plan_copy(i + depth + 1).start()
        wait_gather(cur)

        @pl.when(i >= depth - 1)
        def _():
            wait_scatter(cur)

        nxt, prv = (i + depth) % PLAN_SLOTS, i % PLAN_SLOTS
        for r in range(MOE_BLOCK):
            row_copy(x_hbm, plan[nxt, 0, r], xbuf.at[ahead], r, sem_in.at[ahead], False).start(priority=0)
            row_copy(y_hbm, plan[prv, 0, MOE_BLOCK + r], ybuf.at[prev], r, sem_out.at[prev], True).start(priority=1)
        x = _load_slabs(xbuf.at[cur], MOE_BLOCK, SLAB).astype(BF16)
        a = jnp.dot(x, wg_ref[...].astype(BF16), preferred_element_type=F32)
        u = jnp.dot(x, wu_ref[...].astype(BF16), preferred_element_type=F32)
        hid = (a * _sigmoid(a) * u).astype(BF16)
        _store_slabs(ybuf.at[cur], jnp.dot(hid, wd_ref[...].astype(BF16), preferred_element_type=F32), SLAB)

        @pl.when(i == n_used)
        def _():
            plan_copy(i + depth + 1).wait()
            for k in range(1, depth):
                wait_gather((i + k) % depth)

                @pl.when(i >= k - 1)
                def _():
                    wait_scatter((i + depth - k) % depth)


def _moe_plan(ids, t):
    n_slot = t * TOP_K_IN_GROUP
    n_blocks = n_slot // MOE_BLOCK + N_EXPERTS
    flat_e = ids.T.reshape(-1)
    order = jnp.argsort(flat_e).astype(I32)
    experts = jnp.arange(N_EXPERTS, dtype=I32)
    counts = jnp.sum(flat_e[:, None] == experts[None, :], axis=0, dtype=I32)
    n_blk = (counts + MOE_BLOCK - 1) // MOE_BLOCK
    blk_end = jnp.cumsum(n_blk)
    start = jnp.cumsum(counts) - counts
    blocks = jnp.arange(n_blocks, dtype=I32)
    blk_expert = jnp.minimum(jnp.sum(blk_end[None, :] <= blocks[:, None], axis=1, dtype=I32), N_EXPERTS - 1)
    row0 = (blocks - (blk_end - n_blk)[blk_expert]) * MOE_BLOCK
    rows = row0[:, None] + jnp.arange(MOE_BLOCK, dtype=I32)[None, :]
    is_slot = (rows >= 0) & (rows < counts[blk_expert][:, None])
    slot = order[jnp.clip(start[blk_expert][:, None] + rows, 0, n_slot - 1)]
    src = jnp.where(is_slot, slot // TOP_K_IN_GROUP, 0).astype(I32)
    dst = jnp.where(is_slot, slot, 0).astype(I32)
    valid = is_slot.astype(I32)
    tail = ROW_BUFS + 2
    n_rows = n_blocks + 1 + tail
    lead = lambda a: jnp.concatenate([jnp.zeros((1, MOE_BLOCK), I32), a, jnp.zeros((tail, MOE_BLOCK), I32)])
    trash = (n_slot + (jnp.arange(n_rows, dtype=I32)[:, None] + ROW_BUFS - 1) % ROW_BUFS * MOE_BLOCK
             + jnp.arange(MOE_BLOCK, dtype=I32)[None, :])
    dst_rows = jnp.where(lead(valid) > 0, lead(dst), trash)
    plan = jnp.concatenate([lead(src), dst_rows], axis=1)
    return blk_expert, blk_end[-1:].astype(I32), plan[:, None, :]


def _experts(xn_slabs, ids, w_gate, w_up, w_down, layer):
    t = xn_slabs.shape[0] // SLAB
    blk_expert, n_used, plan = _moe_plan(ids, t)
    n_blocks = blk_expert.shape[0]
    w_spec = lambda a: pl.BlockSpec((None, None) + a.shape[2:], lambda i, be, nu: (layer, be[i], 0, 0))
    buf = pltpu.VMEM((ROW_BUFS, MOE_BLOCK * SLAB, LANES), F32)
    return pl.pallas_call(
        _expert_kernel,
        name="moe_experts",
        grid_spec=pltpu.PrefetchScalarGridSpec(
            num_scalar_prefetch=2,
            grid=(n_blocks,),
            in_specs=[pl.BlockSpec(memory_space=pl.ANY), pl.BlockSpec(memory_space=pl.ANY),
                      w_spec(w_gate), w_spec(w_up), w_spec(w_down)],
            out_specs=pl.BlockSpec(memory_space=pl.ANY),
            scratch_shapes=[pltpu.SMEM((PLAN_SLOTS, 1, 2 * MOE_BLOCK), I32), buf, buf,
                            pltpu.SemaphoreType.DMA((PLAN_SLOTS,)), pltpu.SemaphoreType.DMA((ROW_BUFS,)),
                            pltpu.SemaphoreType.DMA((ROW_BUFS,))]),
        out_shape=jax.ShapeDtypeStruct(((t * TOP_K_IN_GROUP + ROW_BUFS * MOE_BLOCK) * SLAB, LANES), F32),
        compiler_params=_params("arbitrary"),
    )(blk_expert, n_used, plan, xn_slabs, w_gate, w_up, w_down)


def _combine_kernel(h_ref, y_ref, gate_ref, o_ref):
    gate = gate_ref[...]
    rows = h_ref.shape[0]
    y0 = _load_slabs(y_ref, rows, TOP_K_IN_GROUP * SLAB)
    y1 = _load_slabs(y_ref, rows, TOP_K_IN_GROUP * SLAB, SLAB)
    o_ref[...] = h_ref[...] + gate[:, 0:1] * y0 + gate[:, 1:2] * y1


def _combine(h, y_slabs, gates):
    t = h.shape[0]
    tm = ROW_TILE
    return pl.pallas_call(
        _combine_kernel,
        name="moe_combine",
        grid=(t // tm,),
        in_specs=[pl.BlockSpec((tm, D_MODEL), lambda i: (i, 0)),
                  pl.BlockSpec((tm * TOP_K_IN_GROUP * SLAB, LANES), lambda i: (i, 0)),
                  pl.BlockSpec((tm, TOP_K_IN_GROUP), lambda i: (i, 0))],
        out_specs=pl.BlockSpec((tm, D_MODEL), lambda i: (i, 0)),
        out_shape=jax.ShapeDtypeStruct((t, D_MODEL), F32),
        compiler_params=_params("arbitrary"),
    )(h, y_slabs, gates.T)


def _moe(h, g, wg_group, bg_group, wg_expert, bg_expert, w_gate, w_up, w_down, layer):
    xn_slabs, ids, gates = _router(h, g, wg_group, bg_group, wg_expert, bg_expert)
    y_slabs = _experts(xn_slabs, ids, w_gate, w_up, w_down, layer)
    return _combine(h, y_slabs, gates)


def _even_mixer(h, g, w_in, cmp_pe, w_cmp1, w_cmp2, q_norm_g, k_norm_g, ret_norm_g, w_out, b, s):
    ret, nq, chunks, ks_aug, vs, kw_aug, vw, gates = _even_inproj(h, g, w_in, q_norm_g, k_norm_g, s)
    y_ret = _retention(ret, ret_norm_g, b, s)
    cmp_kv = _compress(chunks, cmp_pe, w_cmp1, w_cmp2, k_norm_g, b, s)
    y_nsa = _nsa_attention(nq, cmp_kv, ks_aug, vs, kw_aug, vw, gates, b, s)
    return _even_outproj(h, y_ret, y_nsa, w_out)


def kernel(x, mix_norm_g, ffn_norm_g, ev_w_in, ev_cmp_pe, ev_w_cmp1, ev_w_cmp2, ev_q_norm_g, ev_k_norm_g, ev_ret_norm_g, ev_w_out, od_w_in, od_conv_w, od_conv_b, od_w_out, moe_wg_group, moe_bg_group, moe_wg_expert, moe_bg_expert, moe_w_gate, moe_w_up, moe_w_down):
    b, s, d = x.shape
    h = x.reshape(b * s, d)
    for layer in range(mix_norm_g.shape[0]):
        i = layer // 2
        if layer % 2 == 0:
            h = _even_mixer(h, mix_norm_g[layer], ev_w_in[i], ev_cmp_pe[i], ev_w_cmp1[i], ev_w_cmp2[i],
                            ev_q_norm_g[i], ev_k_norm_g[i], ev_ret_norm_g[i], ev_w_out[i], b, s)
        else:
            h = _odd_mixer(h, mix_norm_g[layer], od_w_in[i], od_conv_w[i], od_conv_b[i], od_w_out[i], b, s)
        h = _moe(h, ffn_norm_g[layer], moe_wg_group[layer], moe_bg_group[layer], moe_wg_expert[layer],
                 moe_bg_expert[layer], moe_w_gate, moe_w_up, moe_w_down, layer)
    return h.reshape(b, s, d)
```

```python
import functools

import jax
import jax.numpy as jnp
import numpy as np
from jax import lax
from jax.experimental import pallas as pl
from jax.experimental.pallas import tpu as pltpu

F32 = jnp.float32
BF16 = jnp.bfloat16
I32 = jnp.int32
HIGHEST = lax.Precision.HIGHEST

D_MODEL = 1024
HEAD_DIM = 64
RET_HEADS = 8
NSA_HEADS = 8
NSA_KV_GROUPS = 2
NSA_GROUP_SIZE = NSA_HEADS // NSA_KV_GROUPS
RET_W = RET_HEADS * HEAD_DIM
NSA_W = NSA_HEADS * HEAD_DIM
NSA_KV_W = NSA_KV_GROUPS * HEAD_DIM
N_BRANCHES = 3
RET_CHUNK = 128
CMP_STRIDE = 16
CMP_BLOCK = 2 * CMP_STRIDE
CMP_HIDDEN = 128
SEL_BLOCK = 64
TOP_N = 8
WINDOW = 256
N_GROUPS = 4
EXPERTS_PER_GROUP = 8
N_EXPERTS = N_GROUPS * EXPERTS_PER_GROUP
TOP_K_IN_GROUP = 2
D_EXPERT = 256
RMS_EPS = 1e-6
NEG_INF = -1e30
ATTN_SCALE = HEAD_DIM ** -0.5

LANES = 128
VMEM_LIMIT = 48 * 1024 * 1024
ROW_TILE = 512
NSA_Q_TILE = 256
NSA_KV_CHUNK = 512
MOE_BLOCK = 256
ROUTER_ROWS = 40
SLAB = D_MODEL // LANES
PLAN_SLOTS = 8
ROW_BUFS = 4


def _params(*sem):
    return pltpu.CompilerParams(dimension_semantics=sem, vmem_limit_bytes=VMEM_LIMIT)


def _rms(x, g):
    return x * lax.rsqrt(jnp.mean(x * x, axis=-1, keepdims=True) + RMS_EPS) * g


def _group_rms(x, bd, gain):
    sq = x * x
    hi = sq.astype(BF16)
    lo = (sq - hi.astype(F32)).astype(BF16)
    ms = jnp.dot(hi, bd, preferred_element_type=F32) + jnp.dot(lo, bd, preferred_element_type=F32)
    return x * lax.rsqrt(ms + RMS_EPS) * gain


def _sigmoid(x):
    return 1.0 / (1.0 + jnp.exp(-x))


def _block_diag_mean(n):
    idx = np.arange(n) // HEAD_DIM
    return jnp.asarray((idx[:, None] == idx[None, :]).astype(np.float32) / HEAD_DIM, BF16)


def _even_inproj_kernel(x_ref, g_ref, wret_ref, wkt_ref, wnq_ref, wkv_ref, wng_ref, qgain_ref, kgain_ref, bd512_ref,
                        bd128_ref, ret_ref, kt_ref, nq_ref, chunk_ref, ks_ref, vs_ref, kw_ref, vw_ref, gate_ref, kcv_ref, *, seq_tiles):
    tm = x_ref.shape[0]
    xn = _rms(x_ref[...], g_ref[...]).astype(BF16)
    ret_ref[...] = jnp.dot(xn, wret_ref[...], preferred_element_type=F32).astype(BF16)
    kt_ref[...] = lax.dot_general(wkt_ref[...], xn, (((1,), (1,)), ((), ())), preferred_element_type=F32).astype(BF16)
    nq = jnp.dot(xn, wnq_ref[...], preferred_element_type=F32)
    nq_ref[...] = _group_rms(nq, bd512_ref[...], qgain_ref[...]).astype(BF16)
    kv = jnp.dot(xn, wkv_ref[...], preferred_element_type=F32)
    w = NSA_KV_W
    n_chunk = tm // CMP_STRIDE
    for kv_i in range(2):
        kcv_ref[kv_i] = kv[:, kv_i * w:(kv_i + 1) * w]
        for j in range(CMP_STRIDE // 2):
            even = kcv_ref[kv_i, pl.ds(2 * j, n_chunk, stride=CMP_STRIDE), :]
            odd = kcv_ref[kv_i, pl.ds(2 * j + 1, n_chunk, stride=CMP_STRIDE), :]
            for g in range(NSA_KV_GROUPS):
                sl = slice(g * HEAD_DIM, (g + 1) * HEAD_DIM)
                pair = jnp.concatenate([even[:, sl], odd[:, sl]], axis=-1).astype(BF16)
                chunk_ref[kv_i, g, :, j * LANES:(j + 1) * LANES] = pair
    ks = _group_rms(kv[:, 2 * w:3 * w], bd128_ref[...], kgain_ref[0:1, :])
    kw = _group_rms(kv[:, 4 * w:5 * w], bd128_ref[...], kgain_ref[1:2, :])
    vs, vw = kv[:, 3 * w:4 * w], kv[:, 5 * w:6 * w]
    pos = (pl.program_id(0) % seq_tiles) * tm + lax.broadcasted_iota(I32, (tm, HEAD_DIM), 0)
    lane = lax.broadcasted_iota(I32, (tm, HEAD_DIM), 1)
    blk = lax.shift_right_arithmetic(pos, int(np.log2(SEL_BLOCK)))
    feat_w = jnp.where(lane == 0, blk, jnp.where(lane == 1, pos & (SEL_BLOCK - 1), 0)).astype(F32)
    feat_s = feat_w + jnp.where(lane == blk + 2, 1.0, 0.0)
    ones_col = jnp.where(lane == 0, 1.0, 0.0)
    for g in range(NSA_KV_GROUPS):
        sl = slice(g * HEAD_DIM, (g + 1) * HEAD_DIM)
        ks_ref[g] = jnp.concatenate([ks[:, sl], feat_s], axis=-1).astype(BF16)
        kw_ref[g] = jnp.concatenate([kw[:, sl], feat_w], axis=-1).astype(BF16)
        vs_ref[g] = jnp.concatenate([vs[:, sl], ones_col], axis=-1).astype(BF16)
        vw_ref[g] = jnp.concatenate([vw[:, sl], ones_col], axis=-1).astype(BF16)
    ng = jnp.dot(xn, wng_ref[...], preferred_element_type=F32)
    for g in range(NSA_KV_GROUPS):
        gate_ref[g] = ng[:, g * LANES:(g + 1) * LANES]


def _even_inproj(h, g, w_in, q_norm_g, k_norm_g, s):
    t = h.shape[0]
    tm = ROW_TILE
    c_ret, c_nq, c_kv = 4 * RET_W, 4 * RET_W + NSA_W, 4 * RET_W + NSA_W + 6 * NSA_KV_W
    n_gate = NSA_GROUP_SIZE * N_BRANCHES
    flat = CMP_STRIDE * HEAD_DIM
    wb = w_in.astype(BF16)
    wret = jnp.concatenate([wb[:, :RET_W], wb[:, 2 * RET_W:c_ret]], axis=1)
    wkt = wb[:, RET_W:2 * RET_W].T
    wnq, wkv = wb[:, c_ret:c_nq], wb[:, c_nq:c_kv]
    wng = jnp.concatenate([jnp.pad(wb[:, c_kv + gi * n_gate:c_kv + (gi + 1) * n_gate], ((0, 0), (0, LANES - n_gate)))
                           for gi in range(NSA_KV_GROUPS)], axis=1)
    qgain = jnp.tile(q_norm_g, NSA_HEADS)[None, :]
    kgain = jnp.stack([jnp.tile(k_norm_g[1], NSA_KV_GROUPS), jnp.tile(k_norm_g[2], NSA_KV_GROUPS)])
    full = lambda a: pl.BlockSpec(a.shape, lambda i: (0,) * a.ndim)
    row = lambda n: pl.BlockSpec((tm, n), lambda i: (i, 0))
    grp = lambda n: pl.BlockSpec((NSA_KV_GROUPS, tm, n), lambda i: (0, i, 0))
    grp_shape = lambda n, dt: jax.ShapeDtypeStruct((NSA_KV_GROUPS, t, n), dt)
    bd512, bd128 = _block_diag_mean(NSA_W), _block_diag_mean(NSA_KV_W)
    gg = g[None, :]
    return pl.pallas_call(
        functools.partial(_even_inproj_kernel, seq_tiles=s // tm),
        name="even_inproj",
        grid=(t // tm,),
        in_specs=[row(D_MODEL), full(gg), full(wret), full(wkt), full(wnq), full(wkv), full(wng), full(qgain),
                  full(kgain), full(bd512), full(bd128)],
        out_specs=[row(3 * RET_W), pl.BlockSpec((RET_W, tm), lambda i: (0, i)), row(NSA_W),
                   pl.BlockSpec((2, NSA_KV_GROUPS, tm // CMP_STRIDE, flat), lambda i: (0, 0, i, 0)),
                   grp(LANES), grp(LANES), grp(LANES), grp(LANES), grp(LANES)],
        out_shape=[jax.ShapeDtypeStruct((t, 3 * RET_W), BF16), jax.ShapeDtypeStruct((RET_W, t), BF16),
                   jax.ShapeDtypeStruct((t, NSA_W), BF16), jax.ShapeDtypeStruct((2, NSA_KV_GROUPS, t // CMP_STRIDE, flat), BF16), grp_shape(LANES, BF16),
                   grp_shape(LANES, BF16), grp_shape(LANES, BF16), grp_shape(LANES, BF16), grp_shape(LANES, F32)],
        scratch_shapes=[pltpu.VMEM((2, tm, NSA_KV_W), F32)],
        compiler_params=_params("arbitrary"),
    )(h, gg, wret, wkt, wnq, wkv, wng, qgain, kgain, bd512, bd128)


def _retention_kernel(q_ref, kt_ref, v_ref, rg_ref, decay_ref, qd_ref, kd_ref, cd_ref, diag_ref, bd_ref, gain_ref, o_ref):
    n_chunks = q_ref.shape[0] // RET_CHUNK
    gain, diag, bd = gain_ref[...], diag_ref[...], bd_ref[...]
    qd, kd, cd = qd_ref[0], kd_ref[0], cd_ref[0]
    first = lax.broadcasted_iota(I32, (RET_CHUNK, LANES), 1) < HEAD_DIM
    zero = jnp.zeros((RET_CHUNK, LANES), BF16)

    def body(c, state):
        r0 = pl.multiple_of(c * RET_CHUNK, RET_CHUNK)
        rows = pl.ds(r0, RET_CHUNK)
        q2, v2 = q_ref[rows, :], v_ref[rows, :]
        kt = kt_ref[:, rows].astype(F32) * ATTN_SCALE
        kt_b = kt.astype(BF16)
        s0 = jnp.dot(jnp.where(first, q2, zero), kt_b, preferred_element_type=F32) * decay_ref[0]
        s1 = jnp.dot(jnp.where(first, zero, q2), kt_b, preferred_element_type=F32) * decay_ref[1]
        y = jnp.where(first, jnp.dot(s0.astype(BF16), v2, preferred_element_type=F32),
                      jnp.dot(s1.astype(BF16), v2, preferred_element_type=F32))
        y = y + jnp.dot((q2.astype(F32) * qd).astype(BF16), state.astype(BF16), preferred_element_type=F32)
        kv = jnp.dot((kt * kd).astype(BF16), v2, preferred_element_type=F32)
        rg = rg_ref[rows, :].astype(F32)
        o_ref[rows, :] = (_group_rms(y, bd, gain) * (rg * _sigmoid(rg))).astype(BF16)
        return cd * state + diag * kv

    lax.fori_loop(0, n_chunks, body, jnp.zeros((LANES, LANES), F32), unroll=True)


def _retention_tables():
    h = np.arange(RET_HEADS, dtype=np.float64)
    log_g = np.log(1.0 - 2.0 ** (-5.0 - h))
    pos = np.arange(RET_CHUNK, dtype=np.float64)
    diff = pos[:, None] - pos[None, :]
    decay = np.where(diff >= 0, np.exp(log_g[:, None, None] * np.maximum(diff, 0.0)), 0.0)
    qd = np.exp(log_g[:, None] * (pos + 1.0))
    kd = np.exp(log_g[:, None] * (RET_CHUNK - 1 - pos))
    cd = np.exp(log_g * RET_CHUNK)
    pairs = RET_HEADS // 2
    rep = lambda a: np.repeat(a, HEAD_DIM, axis=1)
    qd2 = rep(qd.reshape(pairs, 2, RET_CHUNK)).transpose(0, 2, 1)
    kd2 = rep(kd.reshape(pairs, 2, RET_CHUNK))
    cd2 = rep(cd.reshape(pairs, 2, 1)) * np.ones((1, 1, LANES))
    half = np.arange(LANES) // HEAD_DIM
    diag = (half[:, None] == half[None, :]).astype(np.float64)
    return [jnp.asarray(a, F32) for a in (decay.reshape(pairs, 2, RET_CHUNK, RET_CHUNK), qd2, kd2, cd2, diag)]


def _retention(ret, ret_kt, ret_norm_g, b, s):
    t = b * s
    n_pairs = RET_HEADS // 2
    decay, qd, kd, cd, diag = _retention_tables()
    bd = _block_diag_mean(LANES)
    col = lambda off: pl.BlockSpec((s, LANES), lambda bi, p: (bi, off * n_pairs + p))
    tab = lambda a: pl.BlockSpec((1,) + a.shape[1:], lambda bi, p: (p,) + (0,) * (a.ndim - 1))
    full = lambda a: pl.BlockSpec(a.shape, lambda bi, p: (0, 0))
    gain = jnp.tile(ret_norm_g, 2)[None, :]
    return pl.pallas_call(
        _retention_kernel,
        name="retention",
        grid=(b, n_pairs),
        in_specs=[col(0), pl.BlockSpec((LANES, s), lambda bi, p: (p, bi)), col(1), col(2),
                  pl.BlockSpec((None, 2, RET_CHUNK, RET_CHUNK), lambda bi, p: (p, 0, 0, 0)), tab(qd), tab(kd), tab(cd),
                  full(diag), full(bd), full(gain)],
        out_specs=pl.BlockSpec((s, LANES), lambda bi, p: (bi, p)),
        out_shape=jax.ShapeDtypeStruct((t, RET_W), BF16),
        compiler_params=_params("arbitrary", "arbitrary"),
    )(ret, ret_kt, ret, ret, decay, qd, kd, cd, diag, bd, gain)


def _compress_kernel(x_ref, pe_ref, w1_ref, w2_ref, kgain_ref, o_ref):
    is_key = pl.program_id(0) == 0
    half = CMP_STRIDE * HEAD_DIM
    x = x_ref[...]
    n = x.shape[0]
    a = jnp.dot(x, w1_ref[0:half, :], preferred_element_type=F32)
    bm = jnp.dot(x, w1_ref[half:2 * half, :], preferred_element_type=F32)
    pew = jnp.dot(pe_ref[...], w1_ref[...], preferred_element_type=F32)[0:1, :]
    pre = a + pltpu.roll(bm, n - 1, axis=0) + pew
    hid = 0.5 * pre * (1.0 + jnp.tanh(np.sqrt(2.0 / np.pi) * (pre + 0.044715 * pre * pre * pre)))
    out = jnp.dot(hid.astype(BF16), w2_ref[...], preferred_element_type=F32)
    normed = _rms(out, kgain_ref[...])
    o_ref[...] = jnp.where(is_key, normed, out).astype(BF16)


def _compress(chunks, cmp_pe, w_cmp1, w_cmp2, k_norm_g, b, s):
    n = s // CMP_STRIDE
    g = NSA_KV_GROUPS
    flat = CMP_STRIDE * HEAD_DIM
    pe = jnp.broadcast_to(cmp_pe.reshape(2, 1, 2 * flat), (2, 8, 2 * flat)).astype(BF16)
    w1, w2 = w_cmp1.astype(BF16), w_cmp2.astype(BF16)
    kgain = k_norm_g[0][None, :]
    return pl.pallas_call(
        _compress_kernel,
        name="nsa_compress",
        grid=(2, b, g),
        in_specs=[pl.BlockSpec((None, None, n, flat), lambda kv, bi, gi: (kv, gi, bi, 0)),
                  pl.BlockSpec((None, 8, 2 * flat), lambda kv, bi, gi: (kv, 0, 0)),
                  pl.BlockSpec((None, 2 * flat, CMP_HIDDEN), lambda kv, bi, gi: (kv, 0, 0)),
                  pl.BlockSpec((None, CMP_HIDDEN, HEAD_DIM), lambda kv, bi, gi: (kv, 0, 0)),
                  pl.BlockSpec(kgain.shape, lambda kv, bi, gi: (0, 0))],
        out_specs=pl.BlockSpec((None, None, None, n, HEAD_DIM), lambda kv, bi, gi: (kv, bi, gi, 0, 0)),
        out_shape=jax.ShapeDtypeStruct((2, b, g, n, HEAD_DIM), BF16),
        compiler_params=_params("arbitrary", "arbitrary", "arbitrary"),
    )(chunks, pe, w1, w2, kgain)


def _nsa_kernel(slopes_ref, q_ref, kcmp_ref, vcmp_ref, ks_ref, vs_ref, kw_ref, vw_ref, gate_ref, ovl_ref, place_ref,
                o_ref, *, n_sel, n_cmp):
    r_heads = NSA_GROUP_SIZE
    tq = q_ref.shape[0]
    gi = pl.program_id(1)
    t0 = pl.program_id(2) * tq
    nt = (((1,), (1,)), ((), ()))
    tn = (((0,), (0,)), ((), ()))

    q = q_ref[...]
    lane = lax.broadcasted_iota(I32, (tq, HEAD_DIM), 1)
    qs, feats = [], []
    for r in range(r_heads):
        slope = slopes_ref[gi * r_heads + r]
        qs.append((q[:, r * HEAD_DIM:(r + 1) * HEAD_DIM].astype(F32) * ATTN_SCALE).astype(BF16))
        feats.append(jnp.where(lane == 0, slope * SEL_BLOCK, jnp.where(lane == 1, slope, 0.0)))

    def stack_q(extra):
        return jnp.concatenate([jnp.concatenate([qs[r], (feats[r] + extra).astype(BF16)], axis=-1)
                                for r in range(r_heads)], axis=0)

    q4 = stack_q(0.0)
    t_col = t0 + lax.broadcasted_iota(I32, (tq, 1), 0)

    def head_rows(x):
        return [x[r * tq:(r + 1) * tq] for r in range(r_heads)]

    n_pad = kcmp_ref.shape[0]
    c_idx = lax.broadcasted_iota(I32, (tq, n_pad), 1)
    mask_c = (t_col >= c_idx * CMP_STRIDE + (CMP_BLOCK - 1)) & (c_idx < n_cmp)
    s4 = lax.dot_general(q4, kcmp_ref[...], nt, preferred_element_type=F32)
    ps = []
    for s in head_rows(s4):
        s = jnp.where(mask_c, s, NEG_INF)
        p = jnp.where(mask_c, jnp.exp(s - jnp.max(s, axis=-1, keepdims=True)), 0.0)
        ps.append(p * (1.0 / jnp.maximum(jnp.sum(p, axis=-1, keepdims=True), 1e-30)))
    p4 = jnp.concatenate(ps, axis=0).astype(BF16)
    o_cmp = head_rows(jnp.dot(p4, vcmp_ref[...], preferred_element_type=F32))

    n_rows = ovl_ref.shape[0]
    imp4 = lax.dot_general(ovl_ref[...], p4, nt, preferred_element_type=F32)
    imp = imp4[:, 0:tq]
    for r in range(1, r_heads):
        imp = imp + imp4[:, r * tq:(r + 1) * tq]
    j_idx = lax.broadcasted_iota(I32, (n_rows, tq), 0)
    t_row = t0 + lax.broadcasted_iota(I32, (n_rows, tq), 1)
    q_blk = lax.shift_right_arithmetic(t_row, int(np.log2(SEL_BLOCK)))
    valid = (j_idx * SEL_BLOCK <= t_row) & (j_idx < n_sel)
    forced = (j_idx == 0) | (j_idx == q_blk) | (j_idx == q_blk - 1)
    val = jnp.where(j_idx < n_sel, jnp.where(forced, imp + 1e3, jnp.where(valid, imp, -1e3)), -2e3)
    rank = jnp.zeros((n_rows, tq), F32)
    for k in range(n_sel):
        vk = val[k:k + 1, :]
        rank = rank + jnp.where(vk > val, 1.0, jnp.where(vk == val, jnp.where(j_idx > k, 1.0, 0.0), 0.0))
    unselected = jnp.where(valid, jnp.where(rank < TOP_N, 0.0, NEG_INF), NEG_INF).astype(BF16)
    q4_sel = stack_q(lax.dot_general(unselected, place_ref[...], tn, preferred_element_type=F32))

    ck = NSA_KV_CHUNK

    def sel_step(c, carry, causal):
        c0 = pl.multiple_of(c * ck, ck)
        s4 = lax.dot_general(q4_sel, ks_ref[pl.ds(c0, ck), :], nt, preferred_element_type=F32)
        if causal:
            visible = t_col >= c0 + lax.broadcasted_iota(I32, (tq, ck), 1)
        ps, new = [], []
        for r, s in enumerate(head_rows(s4)):
            if causal:
                s = jnp.where(visible, s, NEG_INF)
            m_old, acc_old = carry[r]
            m_new = jnp.maximum(m_old, jnp.max(s, axis=-1, keepdims=True))
            ps.append(jnp.exp(s - m_new))
            new.append((m_new, jnp.exp(m_old - m_new) * acc_old))
        pv = jnp.dot(jnp.concatenate(ps, axis=0).astype(BF16), vs_ref[pl.ds(c0, ck), :], preferred_element_type=F32)
        return tuple((m, acc + o) for (m, acc), o in zip(new, head_rows(pv)))

    def normalise(acc):
        return acc[:, 0:HEAD_DIM] * (1.0 / acc[:, HEAD_DIM:HEAD_DIM + 1])

    init = tuple((jnp.full((tq, 1), NEG_INF, F32), jnp.zeros((tq, LANES), F32)) for _ in range(r_heads))
    c_last = t0 // ck
    carry = lax.fori_loop(0, c_last, lambda c, carry: sel_step(c, carry, False), init)
    o_sel = [normalise(acc) for (_, acc) in sel_step(c_last, carry, True)]

    n_win = WINDOW + tq
    w0 = pl.multiple_of(jnp.maximum(t0 - WINDOW, 0), tq)
    dist_w = t_col - (w0 + lax.broadcasted_iota(I32, (tq, n_win), 1))
    mask_w = (dist_w >= 0) & (dist_w < WINDOW)
    s4 = lax.dot_general(q4, kw_ref[pl.ds(w0, n_win), :], nt, preferred_element_type=F32)
    ps = []
    for s in head_rows(s4):
        s = jnp.where(mask_w, s, NEG_INF)
        ps.append(jnp.exp(s - jnp.max(s, axis=-1, keepdims=True)))
    pv = jnp.dot(jnp.concatenate(ps, axis=0).astype(BF16), vw_ref[pl.ds(w0, n_win), :], preferred_element_type=F32)
    o_win = [normalise(o) for o in head_rows(pv)]

    gate = _sigmoid(gate_ref[...])
    outs = []
    for r in range(r_heads):
        c = N_BRANCHES * r
        outs.append(gate[:, c:c + 1] * o_cmp[r] + gate[:, c + 1:c + 2] * o_sel[r] + gate[:, c + 2:c + 3] * o_win[r])
    o_ref[...] = jnp.concatenate(outs, axis=-1).astype(BF16)


def _nsa_attention(nq, cmp_kv, ks_aug, vs, kw_aug, vw, gates, b, s):
    t = b * s
    g, r = NSA_KV_GROUPS, NSA_GROUP_SIZE
    tq = NSA_Q_TILE
    n_q = s // tq
    n_sel = s // SEL_BLOCK
    n_pad = s // CMP_STRIDE
    n_cmp = n_pad - 1
    n_rows = -(-n_sel // 8) * 8
    assert 2 + n_sel <= HEAD_DIM and s % NSA_KV_CHUNK == 0 and s >= WINDOW + tq and NSA_KV_CHUNK % tq == 0
    slopes = jnp.asarray(2.0 ** (-8.0 * np.arange(1, NSA_HEADS + 1) / NSA_HEADS), F32)
    cmp_start = np.arange(n_pad) * CMP_STRIDE
    cmp_end = cmp_start + CMP_BLOCK - 1
    feat = np.zeros((n_pad, HEAD_DIM), np.float32)
    feat[:, 0], feat[:, 1] = cmp_end // SEL_BLOCK, cmp_end % SEL_BLOCK
    kcmp_aug = jnp.concatenate([cmp_kv[0], jnp.broadcast_to(jnp.asarray(feat, BF16), cmp_kv[0].shape)], axis=-1)
    sel_start = np.arange(n_rows) * SEL_BLOCK
    ovl = ((cmp_start[None, :] < sel_start[:, None] + SEL_BLOCK) & (cmp_start[None, :] + CMP_BLOCK > sel_start[:, None])
           & (np.arange(n_pad)[None, :] < n_cmp) & (np.arange(n_rows)[:, None] < n_sel))
    ovl = jnp.asarray(ovl.astype(np.float32), BF16)
    place = jnp.asarray((np.arange(n_rows)[:, None] + 2 == np.arange(HEAD_DIM)[None, :]).astype(np.float32), BF16)
    seq = lambda w: pl.BlockSpec((None, s, w), lambda bi, gi, qi: (gi, bi, 0))
    cmp = lambda w: pl.BlockSpec((None, None, n_pad, w), lambda bi, gi, qi: (bi, gi, 0, 0))
    return pl.pallas_call(
        functools.partial(_nsa_kernel, n_sel=n_sel, n_cmp=n_cmp),
        name="nsa_attention",
        grid=(b, g, n_q),
        in_specs=[pl.BlockSpec(memory_space=pltpu.SMEM),
                  pl.BlockSpec((tq, r * HEAD_DIM), lambda bi, gi, qi: (bi * n_q + qi, gi)),
                  cmp(2 * HEAD_DIM), cmp(HEAD_DIM), seq(LANES), seq(LANES), seq(LANES), seq(LANES),
                  pl.BlockSpec((None, tq, LANES), lambda bi, gi, qi: (gi, bi * n_q + qi, 0)),
                  pl.BlockSpec(ovl.shape, lambda bi, gi, qi: (0, 0)),
                  pl.BlockSpec(place.shape, lambda bi, gi, qi: (0, 0))],
        out_specs=pl.BlockSpec((tq, r * HEAD_DIM), lambda bi, gi, qi: (bi * n_q + qi, gi)),
        out_shape=jax.ShapeDtypeStruct((t, NSA_W), BF16),
        compiler_params=_params("arbitrary", "arbitrary", "arbitrary"),
    )(slopes, nq, kcmp_aug, cmp_kv[1], ks_aug, vs, kw_aug, vw, gates, ovl, place)


def _even_outproj_kernel(h_ref, a_ref, b_ref, wa_ref, wb_ref, o_ref):
    y = jnp.dot(a_ref[...], wa_ref[...], preferred_element_type=F32)
    y = y + jnp.dot(b_ref[...], wb_ref[...], preferred_element_type=F32)
    o_ref[...] = h_ref[...] + y


def _even_outproj(h, y_ret, y_nsa, w_out):
    t = h.shape[0]
    tm = ROW_TILE
    wb = w_out.astype(BF16)
    wa, wbt = wb[:RET_W], wb[RET_W:]
    row = lambda n: pl.BlockSpec((tm, n), lambda i: (i, 0))
    full = lambda a: pl.BlockSpec(a.shape, lambda i: (0, 0))
    return pl.pallas_call(
        _even_outproj_kernel,
        name="even_outproj",
        grid=(t // tm,),
        in_specs=[row(D_MODEL), row(RET_W), row(NSA_W), full(wa), full(wbt)],
        out_specs=row(D_MODEL),
        out_shape=jax.ShapeDtypeStruct((t, D_MODEL), F32),
        compiler_params=_params("arbitrary"),
    )(h, y_ret, y_nsa, wa, wbt)


def _odd_mixer_kernel(h_ref, g_ref, win_ref, cw_ref, cb_ref, wout_ref, o_ref, tail_ref):
    @pl.when(pl.program_id(1) == 0)
    def _():
        tail_ref[...] = jnp.zeros_like(tail_ref)

    h = h_ref[...]
    ts = h.shape[0]
    xn = _rms(h, g_ref[...]).astype(BF16)
    proj = jnp.dot(xn, win_ref[...], preferred_element_type=F32)
    gate_b, gate_c, hid = proj[:, 0:D_MODEL], proj[:, D_MODEL:2 * D_MODEL], proj[:, 2 * D_MODEL:3 * D_MODEL]
    u = gate_c * hid
    row = lax.broadcasted_iota(I32, (ts, D_MODEL), 0)
    tail = tail_ref[...]
    prev1, prev2 = tail[7:8, :], tail[6:7, :]
    u1 = jnp.where(row >= 1, pltpu.roll(u, 1, axis=0), prev1)
    u2 = jnp.where(row >= 2, pltpu.roll(u, 2, axis=0), jnp.where(row == 1, prev1, prev2))
    cw = cw_ref[...]
    y = cw[0:1, :] * u2 + cw[1:2, :] * u1 + cw[2:3, :] * u + cb_ref[...]
    tail_ref[...] = u[ts - 8:ts, :]
    z = (gate_b * y).astype(BF16)
    o_ref[...] = h + jnp.dot(z, wout_ref[...], preferred_element_type=F32)


def _odd_mixer(h, g, w_in, conv_w, conv_b, w_out, b, s):
    t = b * s
    ts = ROW_TILE
    n_s = s // ts
    win, wout = w_in.astype(BF16), w_out.astype(BF16)
    gg, cb = g[None, :], conv_b[None, :]
    cw = jnp.pad(conv_w, ((0, 8 - conv_w.shape[0]), (0, 0)))
    full = lambda a: pl.BlockSpec(a.shape, lambda bi, si: (0, 0))
    row = pl.BlockSpec((ts, D_MODEL), lambda bi, si: (bi * n_s + si, 0))
    return pl.pallas_call(
        _odd_mixer_kernel,
        name="odd_mixer",
        grid=(b, n_s),
        in_specs=[row, full(gg), full(win), full(cw), full(cb), full(wout)],
        out_specs=row,
        out_shape=jax.ShapeDtypeStruct((t, D_MODEL), F32),
        scratch_shapes=[pltpu.VMEM((8, D_MODEL), F32)],
        compiler_params=_params("arbitrary", "arbitrary"),
    )(h, gg, win, cw, cb, wout)


def _store_slabs(ref, x, stride):
    rows = x.shape[0]
    for j in range(SLAB):
        ref[pl.ds(j, rows, stride=stride), :] = x[:, j * LANES:(j + 1) * LANES]


def _load_slabs(ref, rows, stride, offset=0):
    return jnp.concatenate([ref[pl.ds(offset + j, rows, stride=stride), :] for j in range(SLAB)], axis=-1)


def _router_kernel(h_ref, g_ref, wr_ref, br_ref, xn_ref, ids_ref, gates_ref):
    xn = _rms(h_ref[...], g_ref[...])
    _store_slabs(xn_ref, xn, SLAB)
    logits = lax.dot_general(wr_ref[...], xn, (((1,), (1,)), ((), ())), precision=HIGHEST,
                             preferred_element_type=F32) + br_ref[...]
    row = lambda i: logits[i:i + 1, :]

    def softmax(xs):
        m = functools.reduce(jnp.maximum, xs)
        es = [jnp.exp(x - m) for x in xs]
        tot = functools.reduce(lambda a, c: a + c, es)
        return [e / tot for e in es]

    def argmax(ps):
        best_p, best_i = ps[0], jnp.zeros_like(ps[0], dtype=I32)
        for i in range(1, len(ps)):
            upd = ps[i] > best_p
            best_p = jnp.where(upd, ps[i], best_p)
            best_i = jnp.where(upd, i, best_i)
        return best_p, best_i

    grp_p, grp_i = argmax(softmax([row(i) for i in range(N_GROUPS)]))
    in_grp = []
    for e in range(EXPERTS_PER_GROUP):
        x = row(N_GROUPS + e)
        for gidx in range(1, N_GROUPS):
            x = jnp.where(grp_i == gidx, row(N_GROUPS + gidx * EXPERTS_PER_GROUP + e), x)
        in_grp.append(x)
    pe = softmax(in_grp)
    p1, i1 = argmax(pe)
    p2, i2 = argmax([jnp.where(i1 == e, -1.0, pe[e]) for e in range(EXPERTS_PER_GROUP)])
    tot = p1 + p2
    ids_ref[...] = jnp.concatenate([grp_i * EXPERTS_PER_GROUP + i1, grp_i * EXPERTS_PER_GROUP + i2], axis=0)
    gates_ref[...] = jnp.concatenate([grp_p * p1 / tot, grp_p * p2 / tot], axis=0)


def _router(h, g, wg_group, bg_group, wg_expert, bg_expert):
    t = h.shape[0]
    tm = ROW_TILE
    n_logit = N_GROUPS + N_EXPERTS
    wr = jnp.pad(jnp.concatenate([wg_group, wg_expert], axis=1).T, ((0, ROUTER_ROWS - n_logit), (0, 0)))
    br = jnp.pad(jnp.concatenate([bg_group, bg_expert]), (0, ROUTER_ROWS - n_logit))
    br = jnp.broadcast_to(br[:, None], (ROUTER_ROWS, tm))
    gg = g[None, :]
    full = lambda a: pl.BlockSpec(a.shape, lambda i: (0, 0))
    return pl.pallas_call(
        _router_kernel,
        name="moe_router",
        grid=(t // tm,),
        in_specs=[pl.BlockSpec((tm, D_MODEL), lambda i: (i, 0)), full(gg), full(wr), full(br)],
        out_specs=[pl.BlockSpec((tm * SLAB, LANES), lambda i: (i, 0)),
                   pl.BlockSpec((TOP_K_IN_GROUP, tm), lambda i: (0, i)),
                   pl.BlockSpec((TOP_K_IN_GROUP, tm), lambda i: (0, i))],
        out_shape=[jax.ShapeDtypeStruct((t * SLAB, LANES), F32), jax.ShapeDtypeStruct((TOP_K_IN_GROUP, t), I32),
                   jax.ShapeDtypeStruct((TOP_K_IN_GROUP, t), F32)],
        compiler_params=_params("arbitrary"),
    )(h, gg, wr, br)


def _expert_kernel(blk_expert_ref, n_used_ref, plan_hbm, x_hbm, wg_ref, wu_ref, wd_ref, y_hbm,
                   plan, xbuf, ybuf, sem_plan, sem_in, sem_out):
    del blk_expert_ref
    depth = ROW_BUFS
    i = pl.program_id(0)
    n_used = n_used_ref[0]
    cur, prev, ahead = i % depth, (i + depth - 1) % depth, (i + depth - 1) % depth

    def plan_copy(k):
        return pltpu.make_async_copy(plan_hbm.at[k], plan.at[k % PLAN_SLOTS], sem_plan.at[k % PLAN_SLOTS])

    def row_copy(hbm, hbm_row, buf, r, sem, to_hbm):
        h = hbm.at[pl.ds(pl.multiple_of(hbm_row * SLAB, SLAB), SLAB)]
        v = buf.at[pl.ds(r * SLAB, SLAB)]
        return pltpu.make_async_copy(v, h, sem) if to_hbm else pltpu.make_async_copy(h, v, sem)

    def wait_gather(s):
        for r in range(MOE_BLOCK):
            row_copy(x_hbm, 0, xbuf.at[s], r, sem_in.at[s], False).wait()

    def wait_scatter(s):
        for r in range(MOE_BLOCK):
            row_copy(y_hbm, 0, ybuf.at[s], r, sem_out.at[s], True).wait()

    @pl.when(i == 0)
    def _():
        ybuf[...] = jnp.zeros_like(ybuf)
        n_slot = y_hbm.shape[0] // SLAB - depth * MOE_BLOCK
        for s in range(depth - 1):
            for r in range(MOE_BLOCK):
                row_copy(y_hbm, n_slot + s * MOE_BLOCK + r, ybuf.at[s], r, sem_out.at[s], True).start()
        for s in range(depth - 1):
            wait_scatter(s)
        for k in range(depth + 1):
            plan_copy(k).start()
        for k in range(depth + 1):
            plan_copy(k).wait()
        for blk in range(depth - 1):
            for r in range(MOE_BLOCK):
                row_copy(x_hbm, plan[blk + 1, 0, r], xbuf.at[blk], r, sem_in.at[blk], False).start()

    @pl.when(i <= n_used)
    def _():
        @pl.when(i >= 1)
        def _():
            plan_copy(i + depth).wait()

        plan_copy(i + depth + 1).start()
        wait_gather(cur)

        @pl.when(i >= depth - 1)
        def _():
            wait_scatter(cur)

        nxt, prv = (i + depth) % PLAN_SLOTS, i % PLAN_SLOTS
        for r in range(MOE_BLOCK):
            row_copy(x_hbm, plan[nxt, 0, r], xbuf.at[ahead], r, sem_in.at[ahead], False).start(priority=r % 2)
            row_copy(y_hbm, plan[prv, 0, MOE_BLOCK + r], ybuf.at[prev], r, sem_out.at[prev], True).start(priority=(r + 1) % 2)
        x = _load_slabs(xbuf.at[cur], MOE_BLOCK, SLAB).astype(BF16)
        a = jnp.dot(x, wg_ref[...].astype(BF16), preferred_element_type=F32)
        u = jnp.dot(x, wu_ref[...].astype(BF16), preferred_element_type=F32)
        hid = (a * _sigmoid(a) * u).astype(BF16)
        _store_slabs(ybuf.at[cur], jnp.dot(hid, wd_ref[...].astype(BF16), preferred_element_type=F32), SLAB)

        @pl.when(i == n_used)
        def _():
            plan_copy(i + depth + 1).wait()
            for k in range(1, depth):
                wait_gather((i + k) % depth)

                @pl.when(i >= k - 1)
                def _():
                    wait_scatter((i + depth - k) % depth)


def _moe_plan(ids, t):
    n_slot = t * TOP_K_IN_GROUP
    n_blocks = n_slot // MOE_BLOCK + N_EXPERTS
    flat_e = ids.T.reshape(-1)
    order = jnp.argsort(flat_e).astype(I32)
    experts = jnp.arange(N_EXPERTS, dtype=I32)
    counts = jnp.sum(flat_e[:, None] == experts[None, :], axis=0, dtype=I32)
    n_blk = (counts + MOE_BLOCK - 1) // MOE_BLOCK
    blk_end = jnp.cumsum(n_blk)
    start = jnp.cumsum(counts) - counts
    blocks = jnp.arange(n_blocks, dtype=I32)
    blk_expert = jnp.minimum(jnp.sum(blk_end[None, :] <= blocks[:, None], axis=1, dtype=I32), N_EXPERTS - 1)
    row0 = (blocks - (blk_end - n_blk)[blk_expert]) * MOE_BLOCK
    rows = row0[:, None] + jnp.arange(MOE_BLOCK, dtype=I32)[None, :]
    is_slot = (rows >= 0) & (rows < counts[blk_expert][:, None])
    slot = order[jnp.clip(start[blk_expert][:, None] + rows, 0, n_slot - 1)]
    src = jnp.where(is_slot, slot // TOP_K_IN_GROUP, 0).astype(I32)
    dst = jnp.where(is_slot, slot, 0).astype(I32)
    valid = is_slot.astype(I32)
    tail = ROW_BUFS + 2
    n_rows = n_blocks + 1 + tail
    lead = lambda a: jnp.concatenate([jnp.zeros((1, MOE_BLOCK), I32), a, jnp.zeros((tail, MOE_BLOCK), I32)])
    trash = (n_slot + (jnp.arange(n_rows, dtype=I32)[:, None] + ROW_BUFS - 1) % ROW_BUFS * MOE_BLOCK
             + jnp.arange(MOE_BLOCK, dtype=I32)[None, :])
    dst_rows = jnp.where(lead(valid) > 0, lead(dst), trash)
    plan = jnp.concatenate([lead(src), dst_rows], axis=1)
    return blk_expert, blk_end[-1:].astype(I32), plan[:, None, :]


def _experts(xn_slabs, ids, w_gate, w_up, w_down, layer):
    t = xn_slabs.shape[0] // SLAB
    blk_expert, n_used, plan = _moe_plan(ids, t)
    n_blocks = blk_expert.shape[0]
    w_spec = lambda a: pl.BlockSpec((None, None) + a.shape[2:], lambda i, be, nu: (layer, be[i], 0, 0))
    buf = pltpu.VMEM((ROW_BUFS, MOE_BLOCK * SLAB, LANES), F32)
    return pl.pallas_call(
        _expert_kernel,
        name="moe_experts",
        grid_spec=pltpu.PrefetchScalarGridSpec(
            num_scalar_prefetch=2,
            grid=(n_blocks,),
            in_specs=[pl.BlockSpec(memory_space=pl.ANY), pl.BlockSpec(memory_space=pl.ANY),
                      w_spec(w_gate), w_spec(w_up), w_spec(w_down)],
            out_specs=pl.BlockSpec(memory_space=pl.ANY),
            scratch_shapes=[pltpu.SMEM((PLAN_SLOTS, 1, 2 * MOE_BLOCK), I32), buf, buf,
                            pltpu.SemaphoreType.DMA((PLAN_SLOTS,)), pltpu.SemaphoreType.DMA((ROW_BUFS,)),
                            pltpu.SemaphoreType.DMA((ROW_BUFS,))]),
        out_shape=jax.ShapeDtypeStruct(((t * TOP_K_IN_GROUP + ROW_BUFS * MOE_BLOCK) * SLAB, LANES), F32),
        compiler_params=_params("arbitrary"),
    )(blk_expert, n_used, plan, xn_slabs, w_gate, w_up, w_down)


def _combine_kernel(h_ref, y_ref, gate_ref, o_ref):
    gate = gate_ref[...]
    rows = h_ref.shape[0]
    y0 = _load_slabs(y_ref, rows, TOP_K_IN_GROUP * SLAB)
    y1 = _load_slabs(y_ref, rows, TOP_K_IN_GROUP * SLAB, SLAB)
    o_ref[...] = h_ref[...] + gate[:, 0:1] * y0 + gate[:, 1:2] * y1


def _combine(h, y_slabs, gates):
    t = h.shape[0]
    tm = ROW_TILE
    return pl.pallas_call(
        _combine_kernel,
        name="moe_combine",
        grid=(t // tm,),
        in_specs=[pl.BlockSpec((tm, D_MODEL), lambda i: (i, 0)),
                  pl.BlockSpec((tm * TOP_K_IN_GROUP * SLAB, LANES), lambda i: (i, 0)),
                  pl.BlockSpec((tm, TOP_K_IN_GROUP), lambda i: (i, 0))],
        out_specs=pl.BlockSpec((tm, D_MODEL), lambda i: (i, 0)),
        out_shape=jax.ShapeDtypeStruct((t, D_MODEL), F32),
        compiler_params=_params("arbitrary"),
    )(h, y_slabs, gates.T)


def _moe(h, g, wg_group, bg_group, wg_expert, bg_expert, w_gate, w_up, w_down, layer):
    xn_slabs, ids, gates = _router(h, g, wg_group, bg_group, wg_expert, bg_expert)
    y_slabs = _experts(xn_slabs, ids, w_gate, w_up, w_down, layer)
    return _combine(h, y_slabs, gates)


def _even_mixer(h, g, w_in, cmp_pe, w_cmp1, w_cmp2, q_norm_g, k_norm_g, ret_norm_g, w_out, b, s):
    ret, ret_kt, nq, chunks, ks_aug, vs, kw_aug, vw, gates = _even_inproj(h, g, w_in, q_norm_g, k_norm_g, s)
    y_ret = _retention(ret, ret_kt, ret_norm_g, b, s)
    cmp_kv = _compress(chunks, cmp_pe, w_cmp1, w_cmp2, k_norm_g, b, s)
    y_nsa = _nsa_attention(nq, cmp_kv, ks_aug, vs, kw_aug, vw, gates, b, s)
    return _even_outproj(h, y_ret, y_nsa, w_out)


def kernel(x, mix_norm_g, ffn_norm_g, ev_w_in, ev_cmp_pe, ev_w_cmp1, ev_w_cmp2, ev_q_norm_g, ev_k_norm_g, ev_ret_norm_g, ev_w_out, od_w_in, od_conv_w, od_conv_b, od_w_out, moe_wg_group, moe_bg_group, moe_wg_expert, moe_bg_expert, moe_w_gate, moe_w_up, moe_w_down):
    b, s, d = x.shape
    h = x.reshape(b * s, d)
    for layer in range(mix_norm_g.shape[0]):
        i = layer // 2
        if layer % 2 == 0:
            h = _even_mixer(h, mix_norm_g[layer], ev_w_in[i], ev_cmp_pe[i], ev_w_cmp1[i], ev_w_cmp2[i],
                            ev_q_norm_g[i], ev_k_norm_g[i], ev_ret_norm_g[i], ev_w_out[i], b, s)
        else:
            h = _odd_mixer(h, mix_norm_g[layer], od_w_in[i], od_conv_w[i], od_conv_b[i], od_w_out[i], b, s)
        h = _moe(h, ffn_norm_g[layer], moe_wg_group[layer], moe_bg_group[layer], moe_wg_expert[layer],
                 moe_bg_expert[layer], moe_w_gate, moe_w_up, moe_w_down, layer)
    return h.reshape(b, s, d)
```

```python
import functools

import jax
import jax.numpy as jnp
import numpy as np
from jax import lax
from jax.experimental import pallas as pl
from jax.experimental.pallas import tpu as pltpu

F32 = jnp.float32
BF16 = jnp.bfloat16
I32 = jnp.int32
HIGHEST = lax.Precision.HIGHEST

D_MODEL = 1024
HEAD_DIM = 64
RET_HEADS = 8
NSA_HEADS = 8
NSA_KV_GROUPS = 2
NSA_GROUP_SIZE = NSA_HEADS // NSA_KV_GROUPS
RET_W = RET_HEADS * HEAD_DIM
NSA_W = NSA_HEADS * HEAD_DIM
NSA_KV_W = NSA_KV_GROUPS * HEAD_DIM
N_BRANCHES = 3
RET_CHUNK = 128
CMP_STRIDE = 16
CMP_BLOCK = 2 * CMP_STRIDE
CMP_HIDDEN = 128
SEL_BLOCK = 64
TOP_N = 8
WINDOW = 256
N_GROUPS = 4
EXPERTS_PER_GROUP = 8
N_EXPERTS = N_GROUPS * EXPERTS_PER_GROUP
TOP_K_IN_GROUP = 2
D_EXPERT = 256
RMS_EPS = 1e-6
NEG_INF = -1e30
ATTN_SCALE = HEAD_DIM ** -0.5

LANES = 128
VMEM_LIMIT = 48 * 1024 * 1024
ROW_TILE = 512
NSA_Q_TILE = 256
NSA_KV_CHUNK = 512
MOE_BLOCK = 128
ROUTER_ROWS = 40
SLAB = D_MODEL // LANES
RECORD = 2 * SLAB
PLAN_SLOTS = 8
ROW_BUFS = 4


def _params(*sem):
    return pltpu.CompilerParams(dimension_semantics=sem, vmem_limit_bytes=VMEM_LIMIT)


def _rms(x, g):
    return x * lax.rsqrt(jnp.mean(x * x, axis=-1, keepdims=True) + RMS_EPS) * g


def _group_rms(x, bd, gain):
    sq = x * x
    hi = sq.astype(BF16)
    lo = (sq - hi.astype(F32)).astype(BF16)
    ms = jnp.dot(hi, bd, preferred_element_type=F32) + jnp.dot(lo, bd, preferred_element_type=F32)
    return x * lax.rsqrt(ms + RMS_EPS) * gain


def _sigmoid(x):
    return 1.0 / (1.0 + jnp.exp(-x))


def _block_diag_mean(n):
    idx = np.arange(n) // HEAD_DIM
    return jnp.asarray((idx[:, None] == idx[None, :]).astype(np.float32) / HEAD_DIM, BF16)


def _even_inproj_kernel(x_ref, g_ref, wret_ref, wkt_ref, wnq_ref, wkv_ref, wng_ref, qgain_ref, kgain_ref, bd512_ref,
                        bd128_ref, ret_ref, kt_ref, nq_ref, chunk_ref, ks_ref, vs_ref, kw_ref, vw_ref, gate_ref, kcv_ref, *, seq_tiles):
    tm = x_ref.shape[0]
    xn = _rms(x_ref[...], g_ref[...]).astype(BF16)
    ret_ref[...] = jnp.dot(xn, wret_ref[...], preferred_element_type=F32).astype(BF16)
    kt_ref[...] = lax.dot_general(wkt_ref[...], xn, (((1,), (1,)), ((), ())), preferred_element_type=F32).astype(BF16)
    nq = jnp.dot(xn, wnq_ref[...], preferred_element_type=F32)
    nq_ref[...] = _group_rms(nq, bd512_ref[...], qgain_ref[...]).astype(BF16)
    kv = jnp.dot(xn, wkv_ref[...], preferred_element_type=F32)
    w = NSA_KV_W
    n_chunk = tm // CMP_STRIDE
    for kv_i in range(2):
        kcv_ref[kv_i] = kv[:, kv_i * w:(kv_i + 1) * w]
        for j in range(CMP_STRIDE // 2):
            even = kcv_ref[kv_i, pl.ds(2 * j, n_chunk, stride=CMP_STRIDE), :]
            odd = kcv_ref[kv_i, pl.ds(2 * j + 1, n_chunk, stride=CMP_STRIDE), :]
            for g in range(NSA_KV_GROUPS):
                sl = slice(g * HEAD_DIM, (g + 1) * HEAD_DIM)
                pair = jnp.concatenate([even[:, sl], odd[:, sl]], axis=-1).astype(BF16)
                chunk_ref[kv_i, g, :, j * LANES:(j + 1) * LANES] = pair
    ks = _group_rms(kv[:, 2 * w:3 * w], bd128_ref[...], kgain_ref[0:1, :])
    kw = _group_rms(kv[:, 4 * w:5 * w], bd128_ref[...], kgain_ref[1:2, :])
    vs, vw = kv[:, 3 * w:4 * w], kv[:, 5 * w:6 * w]
    pos = (pl.program_id(0) % seq_tiles) * tm + lax.broadcasted_iota(I32, (tm, HEAD_DIM), 0)
    lane = lax.broadcasted_iota(I32, (tm, HEAD_DIM), 1)
    blk = lax.shift_right_arithmetic(pos, int(np.log2(SEL_BLOCK)))
    feat_w = jnp.where(lane == 0, blk, jnp.where(lane == 1, pos & (SEL_BLOCK - 1), 0)).astype(F32)
    feat_s = feat_w + jnp.where(lane == blk + 2, 1.0, 0.0)
    ones_col = jnp.where(lane == 0, 1.0, 0.0)
    for g in range(NSA_KV_GROUPS):
        sl = slice(g * HEAD_DIM, (g + 1) * HEAD_DIM)
        ks_ref[g] = jnp.concatenate([ks[:, sl], feat_s], axis=-1).astype(BF16)
        kw_ref[g] = jnp.concatenate([kw[:, sl], feat_w], axis=-1).astype(BF16)
        vs_ref[g] = jnp.concatenate([vs[:, sl], ones_col], axis=-1).astype(BF16)
        vw_ref[g] = jnp.concatenate([vw[:, sl], ones_col], axis=-1).astype(BF16)
    ng = jnp.dot(xn, wng_ref[...], preferred_element_type=F32)
    for g in range(NSA_KV_GROUPS):
        gate_ref[g] = ng[:, g * LANES:(g + 1) * LANES]


def _even_inproj(h, g, w_in, q_norm_g, k_norm_g, s):
    t = h.shape[0]
    tm = ROW_TILE
    c_ret, c_nq, c_kv = 4 * RET_W, 4 * RET_W + NSA_W, 4 * RET_W + NSA_W + 6 * NSA_KV_W
    n_gate = NSA_GROUP_SIZE * N_BRANCHES
    flat = CMP_STRIDE * HEAD_DIM
    wb = w_in.astype(BF16)
    wret = jnp.concatenate([wb[:, :RET_W], wb[:, 2 * RET_W:c_ret]], axis=1)
    wkt = wb[:, RET_W:2 * RET_W].T
    wnq, wkv = wb[:, c_ret:c_nq], wb[:, c_nq:c_kv]
    wng = jnp.concatenate([jnp.pad(wb[:, c_kv + gi * n_gate:c_kv + (gi + 1) * n_gate], ((0, 0), (0, LANES - n_gate)))
                           for gi in range(NSA_KV_GROUPS)], axis=1)
    qgain = jnp.tile(q_norm_g, NSA_HEADS)[None, :]
    kgain = jnp.stack([jnp.tile(k_norm_g[1], NSA_KV_GROUPS), jnp.tile(k_norm_g[2], NSA_KV_GROUPS)])
    full = lambda a: pl.BlockSpec(a.shape, lambda i: (0,) * a.ndim)
    row = lambda n: pl.BlockSpec((tm, n), lambda i: (i, 0))
    grp = lambda n: pl.BlockSpec((NSA_KV_GROUPS, tm, n), lambda i: (0, i, 0))
    grp_shape = lambda n, dt: jax.ShapeDtypeStruct((NSA_KV_GROUPS, t, n), dt)
    bd512, bd128 = _block_diag_mean(NSA_W), _block_diag_mean(NSA_KV_W)
    gg = g[None, :]
    return pl.pallas_call(
        functools.partial(_even_inproj_kernel, seq_tiles=s // tm),
        name="even_inproj",
        grid=(t // tm,),
        in_specs=[row(D_MODEL), full(gg), full(wret), full(wkt), full(wnq), full(wkv), full(wng), full(qgain),
                  full(kgain), full(bd512), full(bd128)],
        out_specs=[row(3 * RET_W), pl.BlockSpec((RET_W, tm), lambda i: (0, i)), row(NSA_W),
                   pl.BlockSpec((2, NSA_KV_GROUPS, tm // CMP_STRIDE, flat), lambda i: (0, 0, i, 0)),
                   grp(LANES), grp(LANES), grp(LANES), grp(LANES), grp(LANES)],
        out_shape=[jax.ShapeDtypeStruct((t, 3 * RET_W), BF16), jax.ShapeDtypeStruct((RET_W, t), BF16),
                   jax.ShapeDtypeStruct((t, NSA_W), BF16), jax.ShapeDtypeStruct((2, NSA_KV_GROUPS, t // CMP_STRIDE, flat), BF16), grp_shape(LANES, BF16),
                   grp_shape(LANES, BF16), grp_shape(LANES, BF16), grp_shape(LANES, BF16), grp_shape(LANES, F32)],
        scratch_shapes=[pltpu.VMEM((2, tm, NSA_KV_W), F32)],
        compiler_params=_params("arbitrary"),
    )(h, gg, wret, wkt, wnq, wkv, wng, qgain, kgain, bd512, bd128)


def _retention_kernel(q_ref, kt_ref, v_ref, rg_ref, decay_ref, qd_ref, kd_ref, cd_ref, diag_ref, bd_ref, gain_ref, o_ref):
    n_chunks = q_ref.shape[0] // RET_CHUNK
    gain, diag, bd = gain_ref[...], diag_ref[...], bd_ref[...]
    qd, kd, cd = qd_ref[0], kd_ref[0], cd_ref[0]
    first = lax.broadcasted_iota(I32, (RET_CHUNK, LANES), 1) < HEAD_DIM
    zero = jnp.zeros((RET_CHUNK, LANES), BF16)

    def body(c, state):
        r0 = pl.multiple_of(c * RET_CHUNK, RET_CHUNK)
        rows = pl.ds(r0, RET_CHUNK)
        q2, v2 = q_ref[rows, :], v_ref[rows, :]
        kt = kt_ref[:, rows].astype(F32) * ATTN_SCALE
        kt_b = kt.astype(BF16)
        s0 = jnp.dot(jnp.where(first, q2, zero), kt_b, preferred_element_type=F32) * decay_ref[0]
        s1 = jnp.dot(jnp.where(first, zero, q2), kt_b, preferred_element_type=F32) * decay_ref[1]
        y = jnp.where(first, jnp.dot(s0.astype(BF16), v2, preferred_element_type=F32),
                      jnp.dot(s1.astype(BF16), v2, preferred_element_type=F32))
        y = y + jnp.dot((q2.astype(F32) * qd).astype(BF16), state.astype(BF16), preferred_element_type=F32)
        kv = jnp.dot((kt * kd).astype(BF16), v2, preferred_element_type=F32)
        rg = rg_ref[rows, :].astype(F32)
        o_ref[rows, :] = (_group_rms(y, bd, gain) * (rg * _sigmoid(rg))).astype(BF16)
        return cd * state + diag * kv

    lax.fori_loop(0, n_chunks, body, jnp.zeros((LANES, LANES), F32), unroll=True)


def _retention_tables():
    h = np.arange(RET_HEADS, dtype=np.float64)
    log_g = np.log(1.0 - 2.0 ** (-5.0 - h))
    pos = np.arange(RET_CHUNK, dtype=np.float64)
    diff = pos[:, None] - pos[None, :]
    decay = np.where(diff >= 0, np.exp(log_g[:, None, None] * np.maximum(diff, 0.0)), 0.0)
    qd = np.exp(log_g[:, None] * (pos + 1.0))
    kd = np.exp(log_g[:, None] * (RET_CHUNK - 1 - pos))
    cd = np.exp(log_g * RET_CHUNK)
    pairs = RET_HEADS // 2
    rep = lambda a: np.repeat(a, HEAD_DIM, axis=1)
    qd2 = rep(qd.reshape(pairs, 2, RET_CHUNK)).transpose(0, 2, 1)
    kd2 = rep(kd.reshape(pairs, 2, RET_CHUNK))
    cd2 = rep(cd.reshape(pairs, 2, 1)) * np.ones((1, 1, LANES))
    half = np.arange(LANES) // HEAD_DIM
    diag = (half[:, None] == half[None, :]).astype(np.float64)
    return [jnp.asarray(a, F32) for a in (decay.reshape(pairs, 2, RET_CHUNK, RET_CHUNK), qd2, kd2, cd2, diag)]


def _retention(ret, ret_kt, ret_norm_g, b, s):
    t = b * s
    n_pairs = RET_HEADS // 2
    decay, qd, kd, cd, diag = _retention_tables()
    bd = _block_diag_mean(LANES)
    col = lambda off: pl.BlockSpec((s, LANES), lambda bi, p: (bi, off * n_pairs + p))
    tab = lambda a: pl.BlockSpec((1,) + a.shape[1:], lambda bi, p: (p,) + (0,) * (a.ndim - 1))
    full = lambda a: pl.BlockSpec(a.shape, lambda bi, p: (0, 0))
    gain = jnp.tile(ret_norm_g, 2)[None, :]
    return pl.pallas_call(
        _retention_kernel,
        name="retention",
        grid=(b, n_pairs),
        in_specs=[col(0), pl.BlockSpec((LANES, s), lambda bi, p: (p, bi)), col(1), col(2),
                  pl.BlockSpec((None, 2, RET_CHUNK, RET_CHUNK), lambda bi, p: (p, 0, 0, 0)), tab(qd), tab(kd), tab(cd),
                  full(diag), full(bd), full(gain)],
        out_specs=pl.BlockSpec((s, LANES), lambda bi, p: (bi, p)),
        out_shape=jax.ShapeDtypeStruct((t, RET_W), BF16),
        compiler_params=_params("arbitrary", "arbitrary"),
    )(ret, ret_kt, ret, ret, decay, qd, kd, cd, diag, bd, gain)


def _compress_kernel(x_ref, pe_ref, w1_ref, w2_ref, kgain_ref, o_ref):
    is_key = pl.program_id(0) == 0
    half = CMP_STRIDE * HEAD_DIM
    x = x_ref[...]
    n = x.shape[0]
    a = jnp.dot(x, w1_ref[0:half, :], preferred_element_type=F32)
    bm = jnp.dot(x, w1_ref[half:2 * half, :], preferred_element_type=F32)
    pew = jnp.dot(pe_ref[...], w1_ref[...], preferred_element_type=F32)[0:1, :]
    pre = a + pltpu.roll(bm, n - 1, axis=0) + pew
    hid = 0.5 * pre * (1.0 + jnp.tanh(np.sqrt(2.0 / np.pi) * (pre + 0.044715 * pre * pre * pre)))
    out = jnp.dot(hid.astype(BF16), w2_ref[...], preferred_element_type=F32)
    normed = _rms(out, kgain_ref[...])
    o_ref[...] = jnp.where(is_key, normed, out).astype(BF16)


def _compress(chunks, cmp_pe, w_cmp1, w_cmp2, k_norm_g, b, s):
    n = s // CMP_STRIDE
    g = NSA_KV_GROUPS
    flat = CMP_STRIDE * HEAD_DIM
    pe = jnp.broadcast_to(cmp_pe.reshape(2, 1, 2 * flat), (2, 8, 2 * flat)).astype(BF16)
    w1, w2 = w_cmp1.astype(BF16), w_cmp2.astype(BF16)
    kgain = k_norm_g[0][None, :]
    return pl.pallas_call(
        _compress_kernel,
        name="nsa_compress",
        grid=(2, b, g),
        in_specs=[pl.BlockSpec((None, None, n, flat), lambda kv, bi, gi: (kv, gi, bi, 0)),
                  pl.BlockSpec((None, 8, 2 * flat), lambda kv, bi, gi: (kv, 0, 0)),
                  pl.BlockSpec((None, 2 * flat, CMP_HIDDEN), lambda kv, bi, gi: (kv, 0, 0)),
                  pl.BlockSpec((None, CMP_HIDDEN, HEAD_DIM), lambda kv, bi, gi: (kv, 0, 0)),
                  pl.BlockSpec(kgain.shape, lambda kv, bi, gi: (0, 0))],
        out_specs=pl.BlockSpec((None, None, None, n, HEAD_DIM), lambda kv, bi, gi: (kv, bi, gi, 0, 0)),
        out_shape=jax.ShapeDtypeStruct((2, b, g, n, HEAD_DIM), BF16),
        compiler_params=_params("arbitrary", "arbitrary", "arbitrary"),
    )(chunks, pe, w1, w2, kgain)


def _nsa_kernel(slopes_ref, q_ref, kcmp_ref, vcmp_ref, ks_ref, vs_ref, kw_ref, vw_ref, gate_ref, ovl_ref, place_ref,
                o_ref, *, n_sel, n_cmp):
    r_heads = NSA_GROUP_SIZE
    tq = q_ref.shape[0]
    gi = pl.program_id(1)
    t0 = pl.program_id(2) * tq
    nt = (((1,), (1,)), ((), ()))
    tn = (((0,), (0,)), ((), ()))

    q = q_ref[...]
    lane = lax.broadcasted_iota(I32, (tq, HEAD_DIM), 1)
    qs, feats = [], []
    for r in range(r_heads):
        slope = slopes_ref[gi * r_heads + r]
        qs.append((q[:, r * HEAD_DIM:(r + 1) * HEAD_DIM].astype(F32) * ATTN_SCALE).astype(BF16))
        feats.append(jnp.where(lane == 0, slope * SEL_BLOCK, jnp.where(lane == 1, slope, 0.0)))

    def stack_q(extra):
        return jnp.concatenate([jnp.concatenate([qs[r], (feats[r] + extra).astype(BF16)], axis=-1)
                                for r in range(r_heads)], axis=0)

    q4 = stack_q(0.0)
    t_col = t0 + lax.broadcasted_iota(I32, (tq, 1), 0)

    def head_rows(x):
        return [x[r * tq:(r + 1) * tq] for r in range(r_heads)]

    n_pad = kcmp_ref.shape[0]
    c_idx = lax.broadcasted_iota(I32, (tq, n_pad), 1)
    mask_c = (t_col >= c_idx * CMP_STRIDE + (CMP_BLOCK - 1)) & (c_idx < n_cmp)
    s4 = lax.dot_general(q4, kcmp_ref[...], nt, preferred_element_type=F32)
    ps = []
    for s in head_rows(s4):
        s = jnp.where(mask_c, s, NEG_INF)
        p = jnp.where(mask_c, jnp.exp(s - jnp.max(s, axis=-1, keepdims=True)), 0.0)
        ps.append(p * (1.0 / jnp.maximum(jnp.sum(p, axis=-1, keepdims=True), 1e-30)))
    p4 = jnp.concatenate(ps, axis=0).astype(BF16)
    o_cmp = head_rows(jnp.dot(p4, vcmp_ref[...], preferred_element_type=F32))

    n_rows = ovl_ref.shape[0]
    imp4 = lax.dot_general(ovl_ref[...], p4, nt, preferred_element_type=F32)
    imp = imp4[:, 0:tq]
    for r in range(1, r_heads):
        imp = imp + imp4[:, r * tq:(r + 1) * tq]
    j_idx = lax.broadcasted_iota(I32, (n_rows, tq), 0)
    t_row = t0 + lax.broadcasted_iota(I32, (n_rows, tq), 1)
    q_blk = lax.shift_right_arithmetic(t_row, int(np.log2(SEL_BLOCK)))
    valid = (j_idx * SEL_BLOCK <= t_row) & (j_idx < n_sel)
    forced = (j_idx == 0) | (j_idx == q_blk) | (j_idx == q_blk - 1)
    val = jnp.where(j_idx < n_sel, jnp.where(forced, imp + 1e3, jnp.where(valid, imp, -1e3)), -2e3)
    rank = jnp.zeros((n_rows, tq), F32)
    for k in range(n_sel):
        vk = val[k:k + 1, :]
        rank = rank + jnp.where(vk > val, 1.0, jnp.where(vk == val, jnp.where(j_idx > k, 1.0, 0.0), 0.0))
    unselected = jnp.where(valid, jnp.where(rank < TOP_N, 0.0, NEG_INF), NEG_INF).astype(BF16)
    q4_sel = stack_q(lax.dot_general(unselected, place_ref[...], tn, preferred_element_type=F32))

    ck = NSA_KV_CHUNK

    def sel_step(c, carry, causal):
        c0 = pl.multiple_of(c * ck, ck)
        s4 = lax.dot_general(q4_sel, ks_ref[pl.ds(c0, ck), :], nt, preferred_element_type=F32)
        if causal:
            visible = t_col >= c0 + lax.broadcasted_iota(I32, (tq, ck), 1)
        ps, new = [], []
        for r, s in enumerate(head_rows(s4)):
            if causal:
                s = jnp.where(visible, s, NEG_INF)
            m_old, acc_old = carry[r]
            m_new = jnp.maximum(m_old, jnp.max(s, axis=-1, keepdims=True))
            ps.append(jnp.exp(s - m_new))
            new.append((m_new, jnp.exp(m_old - m_new) * acc_old))
        pv = jnp.dot(jnp.concatenate(ps, axis=0).astype(BF16), vs_ref[pl.ds(c0, ck), :], preferred_element_type=F32)
        return tuple((m, acc + o) for (m, acc), o in zip(new, head_rows(pv)))

    def normalise(acc):
        return acc[:, 0:HEAD_DIM] * (1.0 / acc[:, HEAD_DIM:HEAD_DIM + 1])

    init = tuple((jnp.full((tq, 1), NEG_INF, F32), jnp.zeros((tq, LANES), F32)) for _ in range(r_heads))
    c_last = t0 // ck
    carry = lax.fori_loop(0, c_last, lambda c, carry: sel_step(c, carry, False), init)
    o_sel = [normalise(acc) for (_, acc) in sel_step(c_last, carry, True)]

    n_win = WINDOW + tq
    w0 = pl.multiple_of(jnp.maximum(t0 - WINDOW, 0), tq)
    dist_w = t_col - (w0 + lax.broadcasted_iota(I32, (tq, n_win), 1))
    mask_w = (dist_w >= 0) & (dist_w < WINDOW)
    s4 = lax.dot_general(q4, kw_ref[pl.ds(w0, n_win), :], nt, preferred_element_type=F32)
    ps = []
    for s in head_rows(s4):
        s = jnp.where(mask_w, s, NEG_INF)
        ps.append(jnp.exp(s - jnp.max(s, axis=-1, keepdims=True)))
    pv = jnp.dot(jnp.concatenate(ps, axis=0).astype(BF16), vw_ref[pl.ds(w0, n_win), :], preferred_element_type=F32)
    o_win = [normalise(o) for o in head_rows(pv)]

    gate = _sigmoid(gate_ref[...])
    outs = []
    for r in range(r_heads):
        c = N_BRANCHES * r
        outs.append(gate[:, c:c + 1] * o_cmp[r] + gate[:, c + 1:c + 2] * o_sel[r] + gate[:, c + 2:c + 3] * o_win[r])
    o_ref[...] = jnp.concatenate(outs, axis=-1).astype(BF16)


def _nsa_attention(nq, cmp_kv, ks_aug, vs, kw_aug, vw, gates, b, s):
    t = b * s
    g, r = NSA_KV_GROUPS, NSA_GROUP_SIZE
    tq = NSA_Q_TILE
    n_q = s // tq
    n_sel = s // SEL_BLOCK
    n_pad = s // CMP_STRIDE
    n_cmp = n_pad - 1
    n_rows = -(-n_sel // 8) * 8
    assert 2 + n_sel <= HEAD_DIM and s % NSA_KV_CHUNK == 0 and s >= WINDOW + tq and NSA_KV_CHUNK % tq == 0
    slopes = jnp.asarray(2.0 ** (-8.0 * np.arange(1, NSA_HEADS + 1) / NSA_HEADS), F32)
    cmp_start = np.arange(n_pad) * CMP_STRIDE
    cmp_end = cmp_start + CMP_BLOCK - 1
    feat = np.zeros((n_pad, HEAD_DIM), np.float32)
    feat[:, 0], feat[:, 1] = cmp_end // SEL_BLOCK, cmp_end % SEL_BLOCK
    kcmp_aug = jnp.concatenate([cmp_kv[0], jnp.broadcast_to(jnp.asarray(feat, BF16), cmp_kv[0].shape)], axis=-1)
    sel_start = np.arange(n_rows) * SEL_BLOCK
    ovl = ((cmp_start[None, :] < sel_start[:, None] + SEL_BLOCK) & (cmp_start[None, :] + CMP_BLOCK > sel_start[:, None])
           & (np.arange(n_pad)[None, :] < n_cmp) & (np.arange(n_rows)[:, None] < n_sel))
    ovl = jnp.asarray(ovl.astype(np.float32), BF16)
    place = jnp.asarray((np.arange(n_rows)[:, None] + 2 == np.arange(HEAD_DIM)[None, :]).astype(np.float32), BF16)
    seq = lambda w: pl.BlockSpec((None, s, w), lambda bi, gi, qi: (gi, bi, 0))
    cmp = lambda w: pl.BlockSpec((None, None, n_pad, w), lambda bi, gi, qi: (bi, gi, 0, 0))
    return pl.pallas_call(
        functools.partial(_nsa_kernel, n_sel=n_sel, n_cmp=n_cmp),
        name="nsa_attention",
        grid=(b, g, n_q),
        in_specs=[pl.BlockSpec(memory_space=pltpu.SMEM),
                  pl.BlockSpec((tq, r * HEAD_DIM), lambda bi, gi, qi: (bi * n_q + qi, gi)),
                  cmp(2 * HEAD_DIM), cmp(HEAD_DIM), seq(LANES), seq(LANES), seq(LANES), seq(LANES),
                  pl.BlockSpec((None, tq, LANES), lambda bi, gi, qi: (gi, bi * n_q + qi, 0)),
                  pl.BlockSpec(ovl.shape, lambda bi, gi, qi: (0, 0)),
                  pl.BlockSpec(place.shape, lambda bi, gi, qi: (0, 0))],
        out_specs=pl.BlockSpec((tq, r * HEAD_DIM), lambda bi, gi, qi: (bi * n_q + qi, gi)),
        out_shape=jax.ShapeDtypeStruct((t, NSA_W), BF16),
        compiler_params=_params("arbitrary", "arbitrary", "arbitrary"),
    )(slopes, nq, kcmp_aug, cmp_kv[1], ks_aug, vs, kw_aug, vw, gates, ovl, place)


def _even_outproj_kernel(h_ref, a_ref, b_ref, wa_ref, wb_ref, o_ref):
    y = jnp.dot(a_ref[...], wa_ref[...], preferred_element_type=F32)
    y = y + jnp.dot(b_ref[...], wb_ref[...], preferred_element_type=F32)
    o_ref[...] = h_ref[...] + y


def _even_outproj(h, y_ret, y_nsa, w_out):
    t = h.shape[0]
    tm = ROW_TILE
    wb = w_out.astype(BF16)
    wa, wbt = wb[:RET_W], wb[RET_W:]
    row = lambda n: pl.BlockSpec((tm, n), lambda i: (i, 0))
    full = lambda a: pl.BlockSpec(a.shape, lambda i: (0, 0))
    return pl.pallas_call(
        _even_outproj_kernel,
        name="even_outproj",
        grid=(t // tm,),
        in_specs=[row(D_MODEL), row(RET_W), row(NSA_W), full(wa), full(wbt)],
        out_specs=row(D_MODEL),
        out_shape=jax.ShapeDtypeStruct((t, D_MODEL), F32),
        compiler_params=_params("arbitrary"),
    )(h, y_ret, y_nsa, wa, wbt)


def _odd_mixer_kernel(h_ref, g_ref, win_ref, cw_ref, cb_ref, wout_ref, o_ref, tail_ref):
    @pl.when(pl.program_id(1) == 0)
    def _():
        tail_ref[...] = jnp.zeros_like(tail_ref)

    h = h_ref[...]
    ts = h.shape[0]
    xn = _rms(h, g_ref[...]).astype(BF16)
    proj = jnp.dot(xn, win_ref[...], preferred_element_type=F32)
    gate_b, gate_c, hid = proj[:, 0:D_MODEL], proj[:, D_MODEL:2 * D_MODEL], proj[:, 2 * D_MODEL:3 * D_MODEL]
    u = gate_c * hid
    row = lax.broadcasted_iota(I32, (ts, D_MODEL), 0)
    tail = tail_ref[...]
    prev1, prev2 = tail[7:8, :], tail[6:7, :]
    u1 = jnp.where(row >= 1, pltpu.roll(u, 1, axis=0), prev1)
    u2 = jnp.where(row >= 2, pltpu.roll(u, 2, axis=0), jnp.where(row == 1, prev1, prev2))
    cw = cw_ref[...]
    y = cw[0:1, :] * u2 + cw[1:2, :] * u1 + cw[2:3, :] * u + cb_ref[...]
    tail_ref[...] = u[ts - 8:ts, :]
    z = (gate_b * y).astype(BF16)
    o_ref[...] = h + jnp.dot(z, wout_ref[...], preferred_element_type=F32)


def _odd_mixer(h, g, w_in, conv_w, conv_b, w_out, b, s):
    t = b * s
    ts = ROW_TILE
    n_s = s // ts
    win, wout = w_in.astype(BF16), w_out.astype(BF16)
    gg, cb = g[None, :], conv_b[None, :]
    cw = jnp.pad(conv_w, ((0, 8 - conv_w.shape[0]), (0, 0)))
    full = lambda a: pl.BlockSpec(a.shape, lambda bi, si: (0, 0))
    row = pl.BlockSpec((ts, D_MODEL), lambda bi, si: (bi * n_s + si, 0))
    return pl.pallas_call(
        _odd_mixer_kernel,
        name="odd_mixer",
        grid=(b, n_s),
        in_specs=[row, full(gg), full(win), full(cw), full(cb), full(wout)],
        out_specs=row,
        out_shape=jax.ShapeDtypeStruct((t, D_MODEL), F32),
        scratch_shapes=[pltpu.VMEM((8, D_MODEL), F32)],
        compiler_params=_params("arbitrary", "arbitrary"),
    )(h, gg, win, cw, cb, wout)


def _store_slabs(ref, x, stride):
    rows = x.shape[0]
    for j in range(SLAB):
        ref[pl.ds(j, rows, stride=stride), :] = x[:, j * LANES:(j + 1) * LANES]


def _load_slabs(ref, rows, stride):
    return jnp.concatenate([ref[pl.ds(j, rows, stride=stride), :] for j in range(SLAB)], axis=-1)


def _router_kernel(h_ref, g_ref, wr_ref, br_ref, rec_ref, cls_ref):
    tm = h_ref.shape[0]
    xn = _rms(h_ref[...], g_ref[...])
    _store_slabs(rec_ref, xn, RECORD)
    logits = lax.dot_general(wr_ref[...], xn, (((1,), (1,)), ((), ())), precision=HIGHEST,
                             preferred_element_type=F32) + br_ref[...]
    row = lambda i: logits[i:i + 1, :]

    def softmax(xs):
        m = functools.reduce(jnp.maximum, xs)
        es = [jnp.exp(x - m) for x in xs]
        tot = functools.reduce(lambda a, c: a + c, es)
        return [e / tot for e in es]

    def argmax(ps):
        best_p, best_i = ps[0], jnp.zeros_like(ps[0], dtype=I32)
        for i in range(1, len(ps)):
            upd = ps[i] > best_p
            best_p = jnp.where(upd, ps[i], best_p)
            best_i = jnp.where(upd, i, best_i)
        return best_p, best_i

    grp_p, grp_i = argmax(softmax([row(i) for i in range(N_GROUPS)]))
    in_grp = []
    for e in range(EXPERTS_PER_GROUP):
        x = row(N_GROUPS + e)
        for gidx in range(1, N_GROUPS):
            x = jnp.where(grp_i == gidx, row(N_GROUPS + gidx * EXPERTS_PER_GROUP + e), x)
        in_grp.append(x)
    pe = softmax(in_grp)
    p1, i1 = argmax(pe)
    p2, i2 = argmax([jnp.where(i1 == e, -1.0, pe[e]) for e in range(EXPERTS_PER_GROUP)])
    tot = p1 + p2
    g1, g2 = grp_p * p1 / tot, grp_p * p2 / tot
    first_low = i1 < i2
    gate_rows = [jnp.where(first_low, g1, g2), jnp.where(first_low, g2, g1), jnp.zeros((LANES - TOP_K_IN_GROUP, tm), F32)]
    rec_ref[pl.ds(SLAB, tm, stride=RECORD), :] = jnp.concatenate(gate_rows, axis=0).T
    zero = jnp.zeros((tm, LANES), F32)
    for j in range(SLAB + 1, RECORD):
        rec_ref[pl.ds(j, tm, stride=RECORD), :] = zero
    cls_ref[...] = (grp_i * EXPERTS_PER_GROUP + jnp.minimum(i1, i2)) * EXPERTS_PER_GROUP + jnp.maximum(i1, i2)


def _router(h, g, wg_group, bg_group, wg_expert, bg_expert):
    t = h.shape[0]
    tm = ROW_TILE
    n_logit = N_GROUPS + N_EXPERTS
    wr = jnp.pad(jnp.concatenate([wg_group, wg_expert], axis=1).T, ((0, ROUTER_ROWS - n_logit), (0, 0)))
    br = jnp.pad(jnp.concatenate([bg_group, bg_expert]), (0, ROUTER_ROWS - n_logit))
    br = jnp.broadcast_to(br[:, None], (ROUTER_ROWS, tm))
    gg = g[None, :]
    full = lambda a: pl.BlockSpec(a.shape, lambda i: (0, 0))
    return pl.pallas_call(
        _router_kernel,
        name="moe_router",
        grid=(t // tm,),
        in_specs=[pl.BlockSpec((tm, D_MODEL), lambda i: (i, 0)), full(gg), full(wr), full(br)],
        out_specs=[pl.BlockSpec((tm * RECORD, LANES), lambda i: (i, 0)), pl.BlockSpec((1, tm), lambda i: (0, i))],
        out_shape=[jax.ShapeDtypeStruct((t * RECORD, LANES), F32), jax.ShapeDtypeStruct((1, t), I32)],
        compiler_params=_params("arbitrary"),
    )(h, gg, wr, br)


def _expert_kernel(lo_ref, hi_ref, n_used_ref, plan_hbm, rec_hbm, wg_lo, wu_lo, wd_lo, wg_hi, wu_hi, wd_hi, y_hbm,
                   plan, xbuf, ybuf, sem_plan, sem_in, sem_out):
    del lo_ref, hi_ref
    depth = ROW_BUFS
    i = pl.program_id(0)
    n_used = n_used_ref[0]
    cur, prev, ahead = i % depth, (i + depth - 1) % depth, (i + depth - 1) % depth

    def plan_copy(k):
        return pltpu.make_async_copy(plan_hbm.at[k], plan.at[k % PLAN_SLOTS], sem_plan.at[k % PLAN_SLOTS])

    def rec_copy(r, tok, s):
        return pltpu.make_async_copy(rec_hbm.at[pl.ds(pl.multiple_of(tok * RECORD, RECORD), RECORD)],
                                     xbuf.at[s].at[pl.ds(r * RECORD, RECORD)], sem_in.at[s])

    def out_copy(r, row, s):
        return pltpu.make_async_copy(ybuf.at[s].at[pl.ds(r * SLAB, SLAB)],
                                     y_hbm.at[pl.ds(pl.multiple_of(row * SLAB, SLAB), SLAB)], sem_out.at[s])

    def wait_gather(s):
        for r in range(MOE_BLOCK):
            rec_copy(r, 0, s).wait()

    def wait_scatter(s):
        for r in range(MOE_BLOCK):
            out_copy(r, 0, s).wait()

    @pl.when(i == 0)
    def _():
        ybuf[...] = jnp.zeros_like(ybuf)
        n_tok = y_hbm.shape[0] // SLAB - depth * MOE_BLOCK
        for s in range(depth - 1):
            for r in range(MOE_BLOCK):
                out_copy(r, n_tok + s * MOE_BLOCK + r, s).start()
        for s in range(depth - 1):
            wait_scatter(s)
        for k in range(depth + 1):
            plan_copy(k).start()
        for k in range(depth + 1):
            plan_copy(k).wait()
        for blk in range(depth - 1):
            for r in range(MOE_BLOCK):
                rec_copy(r, plan[blk + 1, 0, r], blk).start()

    @pl.when(i <= n_used)
    def _():
        @pl.when(i >= 1)
        def _():
            plan_copy(i + depth).wait()

        plan_copy(i + depth + 1).start()
        wait_gather(cur)

        @pl.when(i >= depth - 1)
        def _():
            wait_scatter(cur)

        nxt, prv = (i + depth) % PLAN_SLOTS, i % PLAN_SLOTS
        for r in range(MOE_BLOCK):
            rec_copy(r, plan[nxt, 0, r], ahead).start(priority=r % 2)
            out_copy(r, plan[prv, 0, MOE_BLOCK + r], prev).start(priority=(r + 1) % 2)
        rec = xbuf.at[cur]
        x = _load_slabs(rec, MOE_BLOCK, RECORD).astype(BF16)
        gates = rec[pl.ds(SLAB, MOE_BLOCK, stride=RECORD), :]

        def expert(wg_ref, wu_ref, wd_ref):
            a = jnp.dot(x, wg_ref[...].astype(BF16), preferred_element_type=F32)
            u = jnp.dot(x, wu_ref[...].astype(BF16), preferred_element_type=F32)
            hid = (a * _sigmoid(a) * u).astype(BF16)
            return jnp.dot(hid, wd_ref[...].astype(BF16), preferred_element_type=F32)

        y = gates[:, 0:1] * expert(wg_lo, wu_lo, wd_lo) + gates[:, 1:2] * expert(wg_hi, wu_hi, wd_hi)
        _store_slabs(ybuf.at[cur], y, SLAB)

        @pl.when(i == n_used)
        def _():
            plan_copy(i + depth + 1).wait()
            for k in range(1, depth):
                wait_gather((i + k) % depth)

                @pl.when(i >= k - 1)
                def _():
                    wait_scatter((i + depth - k) % depth)


def _moe_plan(cls, t):
    epg = EXPERTS_PER_GROUP
    n_cls = N_GROUPS * epg * epg
    n_blocks = t // MOE_BLOCK + N_GROUPS * epg * (epg - 1) // 2
    order = jnp.argsort(cls).astype(I32)
    counts = jnp.sum(cls[:, None] == jnp.arange(n_cls, dtype=I32)[None, :], axis=0, dtype=I32)
    n_blk = (counts + MOE_BLOCK - 1) // MOE_BLOCK
    blk_end = jnp.cumsum(n_blk)
    start = jnp.cumsum(counts) - counts
    blocks = jnp.arange(n_blocks, dtype=I32)
    blk_cls = jnp.minimum(jnp.sum(blk_end[None, :] <= blocks[:, None], axis=1, dtype=I32), n_cls - 1)
    row0 = (blocks - (blk_end - n_blk)[blk_cls]) * MOE_BLOCK
    rows = row0[:, None] + jnp.arange(MOE_BLOCK, dtype=I32)[None, :]
    is_tok = (rows >= 0) & (rows < counts[blk_cls][:, None])
    tok = jnp.where(is_tok, order[jnp.clip(start[blk_cls][:, None] + rows, 0, t - 1)], 0).astype(I32)
    valid = is_tok.astype(I32)
    tail = ROW_BUFS + 2
    n_rows = n_blocks + 1 + tail
    lead = lambda a: jnp.concatenate([jnp.zeros((1, MOE_BLOCK), I32), a, jnp.zeros((tail, MOE_BLOCK), I32)])
    trash = (t + (jnp.arange(n_rows, dtype=I32)[:, None] + ROW_BUFS - 1) % ROW_BUFS * MOE_BLOCK
             + jnp.arange(MOE_BLOCK, dtype=I32)[None, :])
    dst_rows = jnp.where(lead(valid) > 0, lead(tok), trash)
    plan = jnp.concatenate([lead(tok), dst_rows], axis=1)
    first = blk_cls // (epg * epg) * epg
    return first + blk_cls // epg % epg, first + blk_cls % epg, blk_end[-1:].astype(I32), plan[:, None, :]


def _experts(records, cls, w_gate, w_up, w_down, layer):
    t = records.shape[0] // RECORD
    e_lo, e_hi, n_used, plan = _moe_plan(cls, t)
    n_blocks = e_lo.shape[0]
    lo_spec = lambda a: pl.BlockSpec((None, None) + a.shape[2:], lambda i, lo, hi, nu: (layer, lo[i], 0, 0))
    hi_spec = lambda a: pl.BlockSpec((None, None) + a.shape[2:], lambda i, lo, hi, nu: (layer, hi[i], 0, 0))
    return pl.pallas_call(
        _expert_kernel,
        name="moe_experts",
        grid_spec=pltpu.PrefetchScalarGridSpec(
            num_scalar_prefetch=3,
            grid=(n_blocks,),
            in_specs=[pl.BlockSpec(memory_space=pl.ANY), pl.BlockSpec(memory_space=pl.ANY),
                      lo_spec(w_gate), lo_spec(w_up), lo_spec(w_down), hi_spec(w_gate), hi_spec(w_up), hi_spec(w_down)],
            out_specs=pl.BlockSpec(memory_space=pl.ANY),
            scratch_shapes=[pltpu.SMEM((PLAN_SLOTS, 1, 2 * MOE_BLOCK), I32),
                            pltpu.VMEM((ROW_BUFS, MOE_BLOCK * RECORD, LANES), F32),
                            pltpu.VMEM((ROW_BUFS, MOE_BLOCK * SLAB, LANES), F32),
                            pltpu.SemaphoreType.DMA((PLAN_SLOTS,)), pltpu.SemaphoreType.DMA((ROW_BUFS,)),
                            pltpu.SemaphoreType.DMA((ROW_BUFS,))]),
        out_shape=jax.ShapeDtypeStruct(((t + ROW_BUFS * MOE_BLOCK) * SLAB, LANES), F32),
        compiler_params=_params("arbitrary"),
    )(e_lo, e_hi, n_used, plan, records, w_gate, w_up, w_down, w_gate, w_up, w_down)


def _combine_kernel(h_ref, y_ref, o_ref):
    o_ref[...] = h_ref[...] + _load_slabs(y_ref, h_ref.shape[0], SLAB)


def _combine(h, y_slabs):
    t = h.shape[0]
    tm = ROW_TILE
    return pl.pallas_call(
        _combine_kernel,
        name="moe_combine",
        grid=(t // tm,),
        in_specs=[pl.BlockSpec((tm, D_MODEL), lambda i: (i, 0)), pl.BlockSpec((tm * SLAB, LANES), lambda i: (i, 0))],
        out_specs=pl.BlockSpec((tm, D_MODEL), lambda i: (i, 0)),
        out_shape=jax.ShapeDtypeStruct((t, D_MODEL), F32),
        compiler_params=_params("arbitrary"),
    )(h, y_slabs)


def _moe(h, g, wg_group, bg_group, wg_expert, bg_expert, w_gate, w_up, w_down, layer):
    records, cls = _router(h, g, wg_group, bg_group, wg_expert, bg_expert)
    y_slabs = _experts(records, cls[0], w_gate, w_up, w_down, layer)
    return _combine(h, y_slabs)


def _even_mixer(h, g, w_in, cmp_pe, w_cmp1, w_cmp2, q_norm_g, k_norm_g, ret_norm_g, w_out, b, s):
    ret, ret_kt, nq, chunks, ks_aug, vs, kw_aug, vw, gates = _even_inproj(h, g, w_in, q_norm_g, k_norm_g, s)
    y_ret = _retention(ret, ret_kt, ret_norm_g, b, s)
    cmp_kv = _compress(chunks, cmp_pe, w_cmp1, w_cmp2, k_norm_g, b, s)
    y_nsa = _nsa_attention(nq, cmp_kv, ks_aug, vs, kw_aug, vw, gates, b, s)
    return _even_outproj(h, y_ret, y_nsa, w_out)


def kernel(x, mix_norm_g, ffn_norm_g, ev_w_in, ev_cmp_pe, ev_w_cmp1, ev_w_cmp2, ev_q_norm_g, ev_k_norm_g, ev_ret_norm_g, ev_w_out, od_w_in, od_conv_w, od_conv_b, od_w_out, moe_wg_group, moe_bg_group, moe_wg_expert, moe_bg_expert, moe_w_gate, moe_w_up, moe_w_down):
    b, s, d = x.shape
    h = x.reshape(b * s, d)
    for layer in range(mix_norm_g.shape[0]):
        i = layer // 2
        if layer % 2 == 0:
            h = _even_mixer(h, mix_norm_g[layer], ev_w_in[i], ev_cmp_pe[i], ev_w_cmp1[i], ev_w_cmp2[i],
                            ev_q_norm_g[i], ev_k_norm_g[i], ev_ret_norm_g[i], ev_w_out[i], b, s)
        else:
            h = _odd_mixer(h, mix_norm_g[layer], od_w_in[i], od_conv_w[i], od_conv_b[i], od_w_out[i], b, s)
        h = _moe(h, ffn_norm_g[layer], moe_wg_group[layer], moe_bg_group[layer], moe_wg_expert[layer],
                 moe_bg_expert[layer], moe_w_gate, moe_w_up, moe_w_down, layer)
    return h.reshape(b, s, d)
```

```python
import functools

import jax
import jax.numpy as jnp
import numpy as np
from jax import lax
from jax.experimental import pallas as pl
from jax.experimental.pallas import tpu as pltpu

F32 = jnp.float32
BF16 = jnp.bfloat16
I32 = jnp.int32
HIGHEST = lax.Precision.HIGHEST

D_MODEL = 1024
HEAD_DIM = 64
RET_HEADS = 8
NSA_HEADS = 8
NSA_KV_GROUPS = 2
NSA_GROUP_SIZE = NSA_HEADS // NSA_KV_GROUPS
RET_W = RET_HEADS * HEAD_DIM
NSA_W = NSA_HEADS * HEAD_DIM
NSA_KV_W = NSA_KV_GROUPS * HEAD_DIM
N_BRANCHES = 3
RET_CHUNK = 128
CMP_STRIDE = 16
CMP_BLOCK = 2 * CMP_STRIDE
CMP_HIDDEN = 128
SEL_BLOCK = 64
TOP_N = 8
WINDOW = 256
N_GROUPS = 4
EXPERTS_PER_GROUP = 8
N_EXPERTS = N_GROUPS * EXPERTS_PER_GROUP
TOP_K_IN_GROUP = 2
D_EXPERT = 256
RMS_EPS = 1e-6
NEG_INF = -1e30
ATTN_SCALE = HEAD_DIM ** -0.5

LANES = 128
VMEM_LIMIT = 48 * 1024 * 1024
ROW_TILE = 512
NSA_Q_TILE = 256
NSA_KV_CHUNK = 512
MOE_BLOCK = 256
ROUTER_ROWS = 40
SLAB = D_MODEL // LANES
PLAN_SLOTS = 8
ROW_BUFS = 4


def _params(*sem):
    return pltpu.CompilerParams(dimension_semantics=sem, vmem_limit_bytes=VMEM_LIMIT)


def _rms(x, g):
    return x * lax.rsqrt(jnp.mean(x * x, axis=-1, keepdims=True) + RMS_EPS) * g


def _group_rms(x, bd, gain):
    sq = x * x
    hi = sq.astype(BF16)
    lo = (sq - hi.astype(F32)).astype(BF16)
    ms = jnp.dot(hi, bd, preferred_element_type=F32) + jnp.dot(lo, bd, preferred_element_type=F32)
    return x * lax.rsqrt(ms + RMS_EPS) * gain


def _sigmoid(x):
    return 1.0 / (1.0 + jnp.exp(-x))


def _block_diag_mean(n):
    idx = np.arange(n) // HEAD_DIM
    return jnp.asarray((idx[:, None] == idx[None, :]).astype(np.float32) / HEAD_DIM, BF16)


def _even_inproj_kernel(x_ref, g_ref, wret_ref, wkt_ref, wnq_ref, wkv_ref, wng_ref, qgain_ref, kgain_ref, bd512_ref,
                        bd128_ref, ret_ref, kt_ref, nq_ref, chunk_ref, ks_ref, vs_ref, kw_ref, vw_ref, gate_ref, kcv_ref, *, seq_tiles):
    tm = x_ref.shape[0]
    xn = _rms(x_ref[...], g_ref[...]).astype(BF16)
    ret_ref[...] = jnp.dot(xn, wret_ref[...], preferred_element_type=F32).astype(BF16)
    kt_ref[...] = lax.dot_general(wkt_ref[...], xn, (((1,), (1,)), ((), ())), preferred_element_type=F32).astype(BF16)
    nq = jnp.dot(xn, wnq_ref[...], preferred_element_type=F32)
    nq_ref[...] = _group_rms(nq, bd512_ref[...], qgain_ref[...]).astype(BF16)
    kv = jnp.dot(xn, wkv_ref[...], preferred_element_type=F32)
    w = NSA_KV_W
    n_chunk = tm // CMP_STRIDE
    for kv_i in range(2):
        kcv_ref[kv_i] = kv[:, kv_i * w:(kv_i + 1) * w]
        for j in range(CMP_STRIDE // 2):
            even = kcv_ref[kv_i, pl.ds(2 * j, n_chunk, stride=CMP_STRIDE), :]
            odd = kcv_ref[kv_i, pl.ds(2 * j + 1, n_chunk, stride=CMP_STRIDE), :]
            for g in range(NSA_KV_GROUPS):
                sl = slice(g * HEAD_DIM, (g + 1) * HEAD_DIM)
                pair = jnp.concatenate([even[:, sl], odd[:, sl]], axis=-1).astype(BF16)
                chunk_ref[kv_i, g, :, j * LANES:(j + 1) * LANES] = pair
    ks = _group_rms(kv[:, 2 * w:3 * w], bd128_ref[...], kgain_ref[0:1, :])
    kw = _group_rms(kv[:, 4 * w:5 * w], bd128_ref[...], kgain_ref[1:2, :])
    vs, vw = kv[:, 3 * w:4 * w], kv[:, 5 * w:6 * w]
    pos = (pl.program_id(0) % seq_tiles) * tm + lax.broadcasted_iota(I32, (tm, HEAD_DIM), 0)
    lane = lax.broadcasted_iota(I32, (tm, HEAD_DIM), 1)
    blk = lax.shift_right_arithmetic(pos, int(np.log2(SEL_BLOCK)))
    feat_w = jnp.where(lane == 0, blk, jnp.where(lane == 1, pos & (SEL_BLOCK - 1), 0)).astype(F32)
    feat_s = feat_w + jnp.where(lane == blk + 2, 1.0, 0.0)
    ones_col = jnp.where(lane == 0, 1.0, 0.0)
    for g in range(NSA_KV_GROUPS):
        sl = slice(g * HEAD_DIM, (g + 1) * HEAD_DIM)
        ks_ref[g] = jnp.concatenate([ks[:, sl], feat_s], axis=-1).astype(BF16)
        kw_ref[g] = jnp.concatenate([kw[:, sl], feat_w], axis=-1).astype(BF16)
        vs_ref[g] = jnp.concatenate([vs[:, sl], ones_col], axis=-1).astype(BF16)
        vw_ref[g] = jnp.concatenate([vw[:, sl], ones_col], axis=-1).astype(BF16)
    ng = jnp.dot(xn, wng_ref[...], preferred_element_type=F32)
    for g in range(NSA_KV_GROUPS):
        gate_ref[g] = ng[:, g * LANES:(g + 1) * LANES]


def _even_inproj(h, g, w_in, q_norm_g, k_norm_g, s):
    t = h.shape[0]
    tm = ROW_TILE
    c_ret, c_nq, c_kv = 4 * RET_W, 4 * RET_W + NSA_W, 4 * RET_W + NSA_W + 6 * NSA_KV_W
    n_gate = NSA_GROUP_SIZE * N_BRANCHES
    flat = CMP_STRIDE * HEAD_DIM
    wb = w_in.astype(BF16)
    wret = jnp.concatenate([wb[:, :RET_W], wb[:, 2 * RET_W:c_ret]], axis=1)
    wkt = wb[:, RET_W:2 * RET_W].T
    wnq, wkv = wb[:, c_ret:c_nq], wb[:, c_nq:c_kv]
    wng = jnp.concatenate([jnp.pad(wb[:, c_kv + gi * n_gate:c_kv + (gi + 1) * n_gate], ((0, 0), (0, LANES - n_gate)))
                           for gi in range(NSA_KV_GROUPS)], axis=1)
    qgain = jnp.tile(q_norm_g, NSA_HEADS)[None, :]
    kgain = jnp.stack([jnp.tile(k_norm_g[1], NSA_KV_GROUPS), jnp.tile(k_norm_g[2], NSA_KV_GROUPS)])
    full = lambda a: pl.BlockSpec(a.shape, lambda i: (0,) * a.ndim)
    row = lambda n: pl.BlockSpec((tm, n), lambda i: (i, 0))
    grp = lambda n: pl.BlockSpec((NSA_KV_GROUPS, tm, n), lambda i: (0, i, 0))
    grp_shape = lambda n, dt: jax.ShapeDtypeStruct((NSA_KV_GROUPS, t, n), dt)
    bd512, bd128 = _block_diag_mean(NSA_W), _block_diag_mean(NSA_KV_W)
    gg = g[None, :]
    return pl.pallas_call(
        functools.partial(_even_inproj_kernel, seq_tiles=s // tm),
        name="even_inproj",
        grid=(t // tm,),
        in_specs=[row(D_MODEL), full(gg), full(wret), full(wkt), full(wnq), full(wkv), full(wng), full(qgain),
                  full(kgain), full(bd512), full(bd128)],
        out_specs=[row(3 * RET_W), pl.BlockSpec((RET_W, tm), lambda i: (0, i)), row(NSA_W),
                   pl.BlockSpec((2, NSA_KV_GROUPS, tm // CMP_STRIDE, flat), lambda i: (0, 0, i, 0)),
                   grp(LANES), grp(LANES), grp(LANES), grp(LANES), grp(LANES)],
        out_shape=[jax.ShapeDtypeStruct((t, 3 * RET_W), BF16), jax.ShapeDtypeStruct((RET_W, t), BF16),
                   jax.ShapeDtypeStruct((t, NSA_W), BF16), jax.ShapeDtypeStruct((2, NSA_KV_GROUPS, t // CMP_STRIDE, flat), BF16), grp_shape(LANES, BF16),
                   grp_shape(LANES, BF16), grp_shape(LANES, BF16), grp_shape(LANES, BF16), grp_shape(LANES, F32)],
        scratch_shapes=[pltpu.VMEM((2, tm, NSA_KV_W), F32)],
        compiler_params=_params("arbitrary"),
    )(h, gg, wret, wkt, wnq, wkv, wng, qgain, kgain, bd512, bd128)


def _retention_kernel(q_ref, kt_ref, v_ref, rg_ref, decay_ref, qd_ref, kd_ref, cd_ref, diag_ref, bd_ref, gain_ref, o_ref):
    n_chunks = q_ref.shape[0] // RET_CHUNK
    gain, diag, bd = gain_ref[...], diag_ref[...], bd_ref[...]
    qd, kd, cd = qd_ref[0], kd_ref[0], cd_ref[0]
    first = lax.broadcasted_iota(I32, (RET_CHUNK, LANES), 1) < HEAD_DIM
    zero = jnp.zeros((RET_CHUNK, LANES), BF16)

    def body(c, state):
        r0 = pl.multiple_of(c * RET_CHUNK, RET_CHUNK)
        rows = pl.ds(r0, RET_CHUNK)
        q2, v2 = q_ref[rows, :], v_ref[rows, :]
        kt = kt_ref[:, rows].astype(F32) * ATTN_SCALE
        kt_b = kt.astype(BF16)
        s0 = jnp.dot(jnp.where(first, q2, zero), kt_b, preferred_element_type=F32) * decay_ref[0]
        s1 = jnp.dot(jnp.where(first, zero, q2), kt_b, preferred_element_type=F32) * decay_ref[1]
        y = jnp.where(first, jnp.dot(s0.astype(BF16), v2, preferred_element_type=F32),
                      jnp.dot(s1.astype(BF16), v2, preferred_element_type=F32))
        y = y + jnp.dot((q2.astype(F32) * qd).astype(BF16), state.astype(BF16), preferred_element_type=F32)
        kv = jnp.dot((kt * kd).astype(BF16), v2, preferred_element_type=F32)
        rg = rg_ref[rows, :].astype(F32)
        o_ref[rows, :] = (_group_rms(y, bd, gain) * (rg * _sigmoid(rg))).astype(BF16)
        return cd * state + diag * kv

    lax.fori_loop(0, n_chunks, body, jnp.zeros((LANES, LANES), F32), unroll=True)


def _retention_tables():
    h = np.arange(RET_HEADS, dtype=np.float64)
    log_g = np.log(1.0 - 2.0 ** (-5.0 - h))
    pos = np.arange(RET_CHUNK, dtype=np.float64)
    diff = pos[:, None] - pos[None, :]
    decay = np.where(diff >= 0, np.exp(log_g[:, None, None] * np.maximum(diff, 0.0)), 0.0)
    qd = np.exp(log_g[:, None] * (pos + 1.0))
    kd = np.exp(log_g[:, None] * (RET_CHUNK - 1 - pos))
    cd = np.exp(log_g * RET_CHUNK)
    pairs = RET_HEADS // 2
    rep = lambda a: np.repeat(a, HEAD_DIM, axis=1)
    qd2 = rep(qd.reshape(pairs, 2, RET_CHUNK)).transpose(0, 2, 1)
    kd2 = rep(kd.reshape(pairs, 2, RET_CHUNK))
    cd2 = rep(cd.reshape(pairs, 2, 1)) * np.ones((1, 1, LANES))
    half = np.arange(LANES) // HEAD_DIM
    diag = (half[:, None] == half[None, :]).astype(np.float64)
    return [jnp.asarray(a, F32) for a in (decay.reshape(pairs, 2, RET_CHUNK, RET_CHUNK), qd2, kd2, cd2, diag)]


def _retention(ret, ret_kt, ret_norm_g, b, s):
    t = b * s
    n_pairs = RET_HEADS // 2
    decay, qd, kd, cd, diag = _retention_tables()
    bd = _block_diag_mean(LANES)
    col = lambda off: pl.BlockSpec((s, LANES), lambda bi, p: (bi, off * n_pairs + p))
    tab = lambda a: pl.BlockSpec((1,) + a.shape[1:], lambda bi, p: (p,) + (0,) * (a.ndim - 1))
    full = lambda a: pl.BlockSpec(a.shape, lambda bi, p: (0, 0))
    gain = jnp.tile(ret_norm_g, 2)[None, :]
    return pl.pallas_call(
        _retention_kernel,
        name="retention",
        grid=(b, n_pairs),
        in_specs=[col(0), pl.BlockSpec((LANES, s), lambda bi, p: (p, bi)), col(1), col(2),
                  pl.BlockSpec((None, 2, RET_CHUNK, RET_CHUNK), lambda bi, p: (p, 0, 0, 0)), tab(qd), tab(kd), tab(cd),
                  full(diag), full(bd), full(gain)],
        out_specs=pl.BlockSpec((s, LANES), lambda bi, p: (bi, p)),
        out_shape=jax.ShapeDtypeStruct((t, RET_W), BF16),
        compiler_params=_params("arbitrary", "arbitrary"),
    )(ret, ret_kt, ret, ret, decay, qd, kd, cd, diag, bd, gain)


def _compress_kernel(x_ref, pe_ref, w1_ref, w2_ref, kgain_ref, o_ref):
    is_key = pl.program_id(0) == 0
    half = CMP_STRIDE * HEAD_DIM
    x = x_ref[...]
    n = x.shape[0]
    a = jnp.dot(x, w1_ref[0:half, :], preferred_element_type=F32)
    bm = jnp.dot(x, w1_ref[half:2 * half, :], preferred_element_type=F32)
    pew = jnp.dot(pe_ref[...], w1_ref[...], preferred_element_type=F32)[0:1, :]
    pre = a + pltpu.roll(bm, n - 1, axis=0) + pew
    hid = 0.5 * pre * (1.0 + jnp.tanh(np.sqrt(2.0 / np.pi) * (pre + 0.044715 * pre * pre * pre)))
    out = jnp.dot(hid.astype(BF16), w2_ref[...], preferred_element_type=F32)
    normed = _rms(out, kgain_ref[...])
    o_ref[...] = jnp.where(is_key, normed, out).astype(BF16)


def _compress(chunks, cmp_pe, w_cmp1, w_cmp2, k_norm_g, b, s):
    n = s // CMP_STRIDE
    g = NSA_KV_GROUPS
    flat = CMP_STRIDE * HEAD_DIM
    pe = jnp.broadcast_to(cmp_pe.reshape(2, 1, 2 * flat), (2, 8, 2 * flat)).astype(BF16)
    w1, w2 = w_cmp1.astype(BF16), w_cmp2.astype(BF16)
    kgain = k_norm_g[0][None, :]
    return pl.pallas_call(
        _compress_kernel,
        name="nsa_compress",
        grid=(2, b, g),
        in_specs=[pl.BlockSpec((None, None, n, flat), lambda kv, bi, gi: (kv, gi, bi, 0)),
                  pl.BlockSpec((None, 8, 2 * flat), lambda kv, bi, gi: (kv, 0, 0)),
                  pl.BlockSpec((None, 2 * flat, CMP_HIDDEN), lambda kv, bi, gi: (kv, 0, 0)),
                  pl.BlockSpec((None, CMP_HIDDEN, HEAD_DIM), lambda kv, bi, gi: (kv, 0, 0)),
                  pl.BlockSpec(kgain.shape, lambda kv, bi, gi: (0, 0))],
        out_specs=pl.BlockSpec((None, None, None, n, HEAD_DIM), lambda kv, bi, gi: (kv, bi, gi, 0, 0)),
        out_shape=jax.ShapeDtypeStruct((2, b, g, n, HEAD_DIM), BF16),
        compiler_params=_params("arbitrary", "arbitrary", "arbitrary"),
    )(chunks, pe, w1, w2, kgain)


def _nsa_kernel(slopes_ref, q_ref, kcmp_ref, vcmp_ref, ks_ref, vs_ref, kw_ref, vw_ref, gate_ref, ovl_ref, place_ref,
                o_ref, *, n_sel, n_cmp):
    r_heads = NSA_GROUP_SIZE
    tq = q_ref.shape[0]
    gi = pl.program_id(1)
    t0 = pl.program_id(2) * tq
    nt = (((1,), (1,)), ((), ()))
    tn = (((0,), (0,)), ((), ()))

    q = q_ref[...]
    lane = lax.broadcasted_iota(I32, (tq, HEAD_DIM), 1)
    qs, feats = [], []
    for r in range(r_heads):
        slope = slopes_ref[gi * r_heads + r]
        qs.append((q[:, r * HEAD_DIM:(r + 1) * HEAD_DIM].astype(F32) * ATTN_SCALE).astype(BF16))
        feats.append(jnp.where(lane == 0, slope * SEL_BLOCK, jnp.where(lane == 1, slope, 0.0)))

    def stack_q(extra):
        return jnp.concatenate([jnp.concatenate([qs[r], (feats[r] + extra).astype(BF16)], axis=-1)
                                for r in range(r_heads)], axis=0)

    q4 = stack_q(0.0)
    t_col = t0 + lax.broadcasted_iota(I32, (tq, 1), 0)

    def head_rows(x):
        return [x[r * tq:(r + 1) * tq] for r in range(r_heads)]

    n_pad = kcmp_ref.shape[0]
    c_idx = lax.broadcasted_iota(I32, (tq, n_pad), 1)
    mask_c = (t_col >= c_idx * CMP_STRIDE + (CMP_BLOCK - 1)) & (c_idx < n_cmp)
    s4 = lax.dot_general(q4, kcmp_ref[...], nt, preferred_element_type=F32)
    ps = []
    for s in head_rows(s4):
        s = jnp.where(mask_c, s, NEG_INF)
        p = jnp.where(mask_c, jnp.exp(s - jnp.max(s, axis=-1, keepdims=True)), 0.0)
        ps.append(p * (1.0 / jnp.maximum(jnp.sum(p, axis=-1, keepdims=True), 1e-30)))
    p4 = jnp.concatenate(ps, axis=0).astype(BF16)
    o_cmp = head_rows(jnp.dot(p4, vcmp_ref[...], preferred_element_type=F32))

    n_rows = ovl_ref.shape[0]
    imp4 = lax.dot_general(ovl_ref[...], p4, nt, preferred_element_type=F32)
    imp = imp4[:, 0:tq]
    for r in range(1, r_heads):
        imp = imp + imp4[:, r * tq:(r + 1) * tq]
    j_idx = lax.broadcasted_iota(I32, (n_rows, tq), 0)
    t_row = t0 + lax.broadcasted_iota(I32, (n_rows, tq), 1)
    q_blk = lax.shift_right_arithmetic(t_row, int(np.log2(SEL_BLOCK)))
    valid = (j_idx * SEL_BLOCK <= t_row) & (j_idx < n_sel)
    forced = (j_idx == 0) | (j_idx == q_blk) | (j_idx == q_blk - 1)
    val = jnp.where(j_idx < n_sel, jnp.where(forced, imp + 1e3, jnp.where(valid, imp, -1e3)), -2e3)
    rank = jnp.zeros((n_rows, tq), F32)
    for k in range(n_sel):
        vk = val[k:k + 1, :]
        rank = rank + jnp.where(vk > val, 1.0, jnp.where(vk == val, jnp.where(j_idx > k, 1.0, 0.0), 0.0))
    unselected = jnp.where(valid, jnp.where(rank < TOP_N, 0.0, NEG_INF), NEG_INF).astype(BF16)
    q4_sel = stack_q(lax.dot_general(unselected, place_ref[...], tn, preferred_element_type=F32))

    ck = NSA_KV_CHUNK

    def sel_step(c, carry, causal):
        c0 = pl.multiple_of(c * ck, ck)
        s4 = lax.dot_general(q4_sel, ks_ref[pl.ds(c0, ck), :], nt, preferred_element_type=F32)
        if causal:
            visible = t_col >= c0 + lax.broadcasted_iota(I32, (tq, ck), 1)
        ps, new = [], []
        for r, s in enumerate(head_rows(s4)):
            if causal:
                s = jnp.where(visible, s, NEG_INF)
            m_old, acc_old = carry[r]
            m_new = jnp.maximum(m_old, jnp.max(s, axis=-1, keepdims=True))
            ps.append(jnp.exp(s - m_new))
            new.append((m_new, jnp.exp(m_old - m_new) * acc_old))
        pv = jnp.dot(jnp.concatenate(ps, axis=0).astype(BF16), vs_ref[pl.ds(c0, ck), :], preferred_element_type=F32)
        return tuple((m, acc + o) for (m, acc), o in zip(new, head_rows(pv)))

    def normalise(acc):
        return acc[:, 0:HEAD_DIM] * (1.0 / acc[:, HEAD_DIM:HEAD_DIM + 1])

    init = tuple((jnp.full((tq, 1), NEG_INF, F32), jnp.zeros((tq, LANES), F32)) for _ in range(r_heads))
    c_last = t0 // ck
    carry = lax.fori_loop(0, c_last, lambda c, carry: sel_step(c, carry, False), init)
    o_sel = [normalise(acc) for (_, acc) in sel_step(c_last, carry, True)]

    n_win = WINDOW + tq
    w0 = pl.multiple_of(jnp.maximum(t0 - WINDOW, 0), tq)
    dist_w = t_col - (w0 + lax.broadcasted_iota(I32, (tq, n_win), 1))
    mask_w = (dist_w >= 0) & (dist_w < WINDOW)
    s4 = lax.dot_general(q4, kw_ref[pl.ds(w0, n_win), :], nt, preferred_element_type=F32)
    ps = []
    for s in head_rows(s4):
        s = jnp.where(mask_w, s, NEG_INF)
        ps.append(jnp.exp(s - jnp.max(s, axis=-1, keepdims=True)))
    pv = jnp.dot(jnp.concatenate(ps, axis=0).astype(BF16), vw_ref[pl.ds(w0, n_win), :], preferred_element_type=F32)
    o_win = [normalise(o) for o in head_rows(pv)]

    gate = _sigmoid(gate_ref[...])
    outs = []
    for r in range(r_heads):
        c = N_BRANCHES * r
        outs.append(gate[:, c:c + 1] * o_cmp[r] + gate[:, c + 1:c + 2] * o_sel[r] + gate[:, c + 2:c + 3] * o_win[r])
    o_ref[...] = jnp.concatenate(outs, axis=-1).astype(BF16)


def _nsa_attention(nq, cmp_kv, ks_aug, vs, kw_aug, vw, gates, b, s):
    t = b * s
    g, r = NSA_KV_GROUPS, NSA_GROUP_SIZE
    tq = NSA_Q_TILE
    n_q = s // tq
    n_sel = s // SEL_BLOCK
    n_pad = s // CMP_STRIDE
    n_cmp = n_pad - 1
    n_rows = -(-n_sel // 8) * 8
    assert 2 + n_sel <= HEAD_DIM and s % NSA_KV_CHUNK == 0 and s >= WINDOW + tq and NSA_KV_CHUNK % tq == 0
    slopes = jnp.asarray(2.0 ** (-8.0 * np.arange(1, NSA_HEADS + 1) / NSA_HEADS), F32)
    cmp_start = np.arange(n_pad) * CMP_STRIDE
    cmp_end = cmp_start + CMP_BLOCK - 1
    feat = np.zeros((n_pad, HEAD_DIM), np.float32)
    feat[:, 0], feat[:, 1] = cmp_end // SEL_BLOCK, cmp_end % SEL_BLOCK
    kcmp_aug = jnp.concatenate([cmp_kv[0], jnp.broadcast_to(jnp.asarray(feat, BF16), cmp_kv[0].shape)], axis=-1)
    sel_start = np.arange(n_rows) * SEL_BLOCK
    ovl = ((cmp_start[None, :] < sel_start[:, None] + SEL_BLOCK) & (cmp_start[None, :] + CMP_BLOCK > sel_start[:, None])
           & (np.arange(n_pad)[None, :] < n_cmp) & (np.arange(n_rows)[:, None] < n_sel))
    ovl = jnp.asarray(ovl.astype(np.float32), BF16)
    place = jnp.asarray((np.arange(n_rows)[:, None] + 2 == np.arange(HEAD_DIM)[None, :]).astype(np.float32), BF16)
    seq = lambda w: pl.BlockSpec((None, s, w), lambda bi, gi, qi: (gi, bi, 0))
    cmp = lambda w: pl.BlockSpec((None, None, n_pad, w), lambda bi, gi, qi: (bi, gi, 0, 0))
    return pl.pallas_call(
        functools.partial(_nsa_kernel, n_sel=n_sel, n_cmp=n_cmp),
        name="nsa_attention",
        grid=(b, g, n_q),
        in_specs=[pl.BlockSpec(memory_space=pltpu.SMEM),
                  pl.BlockSpec((tq, r * HEAD_DIM), lambda bi, gi, qi: (bi * n_q + qi, gi)),
                  cmp(2 * HEAD_DIM), cmp(HEAD_DIM), seq(LANES), seq(LANES), seq(LANES), seq(LANES),
                  pl.BlockSpec((None, tq, LANES), lambda bi, gi, qi: (gi, bi * n_q + qi, 0)),
                  pl.BlockSpec(ovl.shape, lambda bi, gi, qi: (0, 0)),
                  pl.BlockSpec(place.shape, lambda bi, gi, qi: (0, 0))],
        out_specs=pl.BlockSpec((tq, r * HEAD_DIM), lambda bi, gi, qi: (bi * n_q + qi, gi)),
        out_shape=jax.ShapeDtypeStruct((t, NSA_W), BF16),
        compiler_params=_params("arbitrary", "arbitrary", "arbitrary"),
    )(slopes, nq, kcmp_aug, cmp_kv[1], ks_aug, vs, kw_aug, vw, gates, ovl, place)


def _even_outproj_kernel(h_ref, a_ref, b_ref, wa_ref, wb_ref, g_ref, wr_ref, br_ref, o_ref, xn_ref, ids_ref, gates_ref):
    y = jnp.dot(a_ref[...], wa_ref[...], preferred_element_type=F32)
    y = y + jnp.dot(b_ref[...], wb_ref[...], preferred_element_type=F32)
    h = h_ref[...] + y
    o_ref[...] = h
    _route(h, g_ref, wr_ref, br_ref, xn_ref, ids_ref, gates_ref)


def _even_outproj(h, y_ret, y_nsa, w_out, router):
    t = h.shape[0]
    tm = ROW_TILE
    wb = w_out.astype(BF16)
    wa, wbt = wb[:RET_W], wb[RET_W:]
    gg, wr, br = _router_operands(*router, tm)
    row = lambda n: pl.BlockSpec((tm, n), lambda i: (i, 0))
    full = lambda a: pl.BlockSpec(a.shape, lambda i: (0, 0))
    r_specs, r_shapes = _router_outputs(t, tm, lambda i: i)
    return pl.pallas_call(
        _even_outproj_kernel,
        name="even_outproj",
        grid=(t // tm,),
        in_specs=[row(D_MODEL), row(RET_W), row(NSA_W), full(wa), full(wbt), full(gg), full(wr), full(br)],
        out_specs=[row(D_MODEL)] + r_specs,
        out_shape=[jax.ShapeDtypeStruct((t, D_MODEL), F32)] + r_shapes,
        compiler_params=_params("arbitrary"),
    )(h, y_ret, y_nsa, wa, wbt, gg, wr, br)


def _odd_mixer_kernel(h_ref, g_ref, win_ref, cw_ref, cb_ref, wout_ref, rg_ref, wr_ref, br_ref,
                      o_ref, xn_ref, ids_ref, gates_ref, tail_ref):
    @pl.when(pl.program_id(1) == 0)
    def _():
        tail_ref[...] = jnp.zeros_like(tail_ref)

    h = h_ref[...]
    ts = h.shape[0]
    xn = _rms(h, g_ref[...]).astype(BF16)
    proj = jnp.dot(xn, win_ref[...], preferred_element_type=F32)
    gate_b, gate_c, hid = proj[:, 0:D_MODEL], proj[:, D_MODEL:2 * D_MODEL], proj[:, 2 * D_MODEL:3 * D_MODEL]
    u = gate_c * hid
    row = lax.broadcasted_iota(I32, (ts, D_MODEL), 0)
    tail = tail_ref[...]
    prev1, prev2 = tail[7:8, :], tail[6:7, :]
    u1 = jnp.where(row >= 1, pltpu.roll(u, 1, axis=0), prev1)
    u2 = jnp.where(row >= 2, pltpu.roll(u, 2, axis=0), jnp.where(row == 1, prev1, prev2))
    cw = cw_ref[...]
    y = cw[0:1, :] * u2 + cw[1:2, :] * u1 + cw[2:3, :] * u + cb_ref[...]
    tail_ref[...] = u[ts - 8:ts, :]
    z = (gate_b * y).astype(BF16)
    h_new = h + jnp.dot(z, wout_ref[...], preferred_element_type=F32)
    o_ref[...] = h_new
    _route(h_new, rg_ref, wr_ref, br_ref, xn_ref, ids_ref, gates_ref)


def _odd_mixer(h, g, w_in, conv_w, conv_b, w_out, router, b, s):
    t = b * s
    ts = ROW_TILE
    n_s = s // ts
    win, wout = w_in.astype(BF16), w_out.astype(BF16)
    gg, cb = g[None, :], conv_b[None, :]
    cw = jnp.pad(conv_w, ((0, 8 - conv_w.shape[0]), (0, 0)))
    rg, wr, br = _router_operands(*router, ts)
    full = lambda a: pl.BlockSpec(a.shape, lambda bi, si: (0, 0))
    row = pl.BlockSpec((ts, D_MODEL), lambda bi, si: (bi * n_s + si, 0))
    r_specs, r_shapes = _router_outputs(t, ts, lambda bi, si: bi * n_s + si)
    return pl.pallas_call(
        _odd_mixer_kernel,
        name="odd_mixer",
        grid=(b, n_s),
        in_specs=[row, full(gg), full(win), full(cw), full(cb), full(wout), full(rg), full(wr), full(br)],
        out_specs=[row] + r_specs,
        out_shape=[jax.ShapeDtypeStruct((t, D_MODEL), F32)] + r_shapes,
        scratch_shapes=[pltpu.VMEM((8, D_MODEL), F32)],
        compiler_params=_params("arbitrary", "arbitrary"),
    )(h, gg, win, cw, cb, wout, rg, wr, br)


def _store_slabs(ref, x, stride):
    rows = x.shape[0]
    for j in range(SLAB):
        ref[pl.ds(j, rows, stride=stride), :] = x[:, j * LANES:(j + 1) * LANES]


def _load_slabs(ref, rows, stride, offset=0):
    return jnp.concatenate([ref[pl.ds(offset + j, rows, stride=stride), :] for j in range(SLAB)], axis=-1)


def _route(h, g_ref, wr_ref, br_ref, xn_ref, ids_ref, gates_ref):
    xn = _rms(h, g_ref[...])
    _store_slabs(xn_ref, xn, SLAB)
    logits = lax.dot_general(wr_ref[...], xn, (((1,), (1,)), ((), ())), precision=HIGHEST,
                             preferred_element_type=F32) + br_ref[...]
    row = lambda i: logits[i:i + 1, :]

    def softmax(xs):
        m = functools.reduce(jnp.maximum, xs)
        es = [jnp.exp(x - m) for x in xs]
        tot = functools.reduce(lambda a, c: a + c, es)
        return [e / tot for e in es]

    def argmax(ps):
        best_p, best_i = ps[0], jnp.zeros_like(ps[0], dtype=I32)
        for i in range(1, len(ps)):
            upd = ps[i] > best_p
            best_p = jnp.where(upd, ps[i], best_p)
            best_i = jnp.where(upd, i, best_i)
        return best_p, best_i

    grp_p, grp_i = argmax(softmax([row(i) for i in range(N_GROUPS)]))
    in_grp = []
    for e in range(EXPERTS_PER_GROUP):
        x = row(N_GROUPS + e)
        for gidx in range(1, N_GROUPS):
            x = jnp.where(grp_i == gidx, row(N_GROUPS + gidx * EXPERTS_PER_GROUP + e), x)
        in_grp.append(x)
    pe = softmax(in_grp)
    p1, i1 = argmax(pe)
    p2, i2 = argmax([jnp.where(i1 == e, -1.0, pe[e]) for e in range(EXPERTS_PER_GROUP)])
    tot = p1 + p2
    ids_ref[...] = jnp.concatenate([grp_i * EXPERTS_PER_GROUP + i1, grp_i * EXPERTS_PER_GROUP + i2], axis=0)
    gates_ref[...] = jnp.concatenate([grp_p * p1 / tot, grp_p * p2 / tot], axis=0)


def _router_operands(g, wg_group, bg_group, wg_expert, bg_expert, tm):
    n_logit = N_GROUPS + N_EXPERTS
    wr = jnp.pad(jnp.concatenate([wg_group, wg_expert], axis=1).T, ((0, ROUTER_ROWS - n_logit), (0, 0)))
    br = jnp.pad(jnp.concatenate([bg_group, bg_expert]), (0, ROUTER_ROWS - n_logit))
    return g[None, :], wr, jnp.broadcast_to(br[:, None], (ROUTER_ROWS, tm))


def _router_outputs(t, tm, row_index):
    specs = [pl.BlockSpec((tm * SLAB, LANES), lambda *i: (row_index(*i), 0)),
             pl.BlockSpec((TOP_K_IN_GROUP, tm), lambda *i: (0, row_index(*i))),
             pl.BlockSpec((TOP_K_IN_GROUP, tm), lambda *i: (0, row_index(*i)))]
    shapes = [jax.ShapeDtypeStruct((t * SLAB, LANES), F32), jax.ShapeDtypeStruct((TOP_K_IN_GROUP, t), I32),
              jax.ShapeDtypeStruct((TOP_K_IN_GROUP, t), F32)]
    return specs, shapes


def _expert_kernel(blk_expert_ref, n_used_ref, plan_hbm, x_hbm, wg_ref, wu_ref, wd_ref, y_hbm,
                   plan, xbuf, ybuf, sem_plan, sem_in, sem_out):
    del blk_expert_ref
    depth = ROW_BUFS
    i = pl.program_id(0)
    n_used = n_used_ref[0]
    cur, prev, ahead = i % depth, (i + depth - 1) % depth, (i + depth - 1) % depth

    def plan_copy(k):
        return pltpu.make_async_copy(plan_hbm.at[k], plan.at[k % PLAN_SLOTS], sem_plan.at[k % PLAN_SLOTS])

    def row_copy(hbm, hbm_row, buf, r, sem, to_hbm):
        h = hbm.at[pl.ds(pl.multiple_of(hbm_row * SLAB, SLAB), SLAB)]
        v = buf.at[pl.ds(r * SLAB, SLAB)]
        return pltpu.make_async_copy(v, h, sem) if to_hbm else pltpu.make_async_copy(h, v, sem)

    def wait_gather(s):
        for r in range(MOE_BLOCK):
            row_copy(x_hbm, 0, xbuf.at[s], r, sem_in.at[s], False).wait()

    def wait_scatter(s):
        for r in range(MOE_BLOCK):
            row_copy(y_hbm, 0, ybuf.at[s], r, sem_out.at[s], True).wait()

    @pl.when(i == 0)
    def _():
        ybuf[...] = jnp.zeros_like(ybuf)
        n_slot = y_hbm.shape[0] // SLAB - depth * MOE_BLOCK
        for s in range(depth - 1):
            for r in range(MOE_BLOCK):
                row_copy(y_hbm, n_slot + s * MOE_BLOCK + r, ybuf.at[s], r, sem_out.at[s], True).start()
        for s in range(depth - 1):
            wait_scatter(s)
        for k in range(depth + 1):
            plan_copy(k).start()
        for k in range(depth + 1):
            plan_copy(k).wait()
        for blk in range(depth - 1):
            for r in range(MOE_BLOCK):
                row_copy(x_hbm, plan[blk + 1, 0, r], xbuf.at[blk], r, sem_in.at[blk], False).start()

    @pl.when(i <= n_used)
    def _():
        @pl.when(i >= 1)
        def _():
            plan_copy(i + depth).wait()

        plan_copy(i + depth + 1).start()
        wait_gather(cur)

        @pl.when(i >= depth - 1)
        def _():
            wait_scatter(cur)

        nxt, prv = (i + depth) % PLAN_SLOTS, i % PLAN_SLOTS
        for r in range(MOE_BLOCK):
            row_copy(x_hbm, plan[nxt, 0, r], xbuf.at[ahead], r, sem_in.at[ahead], False).start(priority=r % 2)
            row_copy(y_hbm, plan[prv, 0, MOE_BLOCK + r], ybuf.at[prev], r, sem_out.at[prev], True).start(priority=(r + 1) % 2)
        x = _load_slabs(xbuf.at[cur], MOE_BLOCK, SLAB).astype(BF16)
        a = jnp.dot(x, wg_ref[...].astype(BF16), preferred_element_type=F32)
        u = jnp.dot(x, wu_ref[...].astype(BF16), preferred_element_type=F32)
        hid = (a * _sigmoid(a) * u).astype(BF16)
        _store_slabs(ybuf.at[cur], jnp.dot(hid, wd_ref[...].astype(BF16), preferred_element_type=F32), SLAB)

        @pl.when(i == n_used)
        def _():
            plan_copy(i + depth + 1).wait()
            for k in range(1, depth):
                wait_gather((i + k) % depth)

                @pl.when(i >= k - 1)
                def _():
                    wait_scatter((i + depth - k) % depth)


def _moe_plan(ids, t):
    n_slot = t * TOP_K_IN_GROUP
    n_blocks = n_slot // MOE_BLOCK + N_EXPERTS
    flat_e = ids.T.reshape(-1)
    order = jnp.argsort(flat_e).astype(I32)
    experts = jnp.arange(N_EXPERTS, dtype=I32)
    counts = jnp.sum(flat_e[:, None] == experts[None, :], axis=0, dtype=I32)
    n_blk = (counts + MOE_BLOCK - 1) // MOE_BLOCK
    blk_end = jnp.cumsum(n_blk)
    start = jnp.cumsum(counts) - counts
    blocks = jnp.arange(n_blocks, dtype=I32)
    blk_expert = jnp.minimum(jnp.sum(blk_end[None, :] <= blocks[:, None], axis=1, dtype=I32), N_EXPERTS - 1)
    row0 = (blocks - (blk_end - n_blk)[blk_expert]) * MOE_BLOCK
    rows = row0[:, None] + jnp.arange(MOE_BLOCK, dtype=I32)[None, :]
    is_slot = (rows >= 0) & (rows < counts[blk_expert][:, None])
    slot = order[jnp.clip(start[blk_expert][:, None] + rows, 0, n_slot - 1)]
    src = jnp.where(is_slot, slot // TOP_K_IN_GROUP, 0).astype(I32)
    dst = jnp.where(is_slot, slot, 0).astype(I32)
    valid = is_slot.astype(I32)
    tail = ROW_BUFS + 2
    n_rows = n_blocks + 1 + tail
    lead = lambda a: jnp.concatenate([jnp.zeros((1, MOE_BLOCK), I32), a, jnp.zeros((tail, MOE_BLOCK), I32)])
    trash = (n_slot + (jnp.arange(n_rows, dtype=I32)[:, None] + ROW_BUFS - 1) % ROW_BUFS * MOE_BLOCK
             + jnp.arange(MOE_BLOCK, dtype=I32)[None, :])
    dst_rows = jnp.where(lead(valid) > 0, lead(dst), trash)
    plan = jnp.concatenate([lead(src), dst_rows], axis=1)
    return blk_expert, blk_end[-1:].astype(I32), plan[:, None, :]


def _experts(xn_slabs, ids, w_gate, w_up, w_down, layer):
    t = xn_slabs.shape[0] // SLAB
    blk_expert, n_used, plan = _moe_plan(ids, t)
    n_blocks = blk_expert.shape[0]
    w_spec = lambda a: pl.BlockSpec((None, None) + a.shape[2:], lambda i, be, nu: (layer, be[i], 0, 0))
    buf = pltpu.VMEM((ROW_BUFS, MOE_BLOCK * SLAB, LANES), F32)
    return pl.pallas_call(
        _expert_kernel,
        name="moe_experts",
        grid_spec=pltpu.PrefetchScalarGridSpec(
            num_scalar_prefetch=2,
            grid=(n_blocks,),
            in_specs=[pl.BlockSpec(memory_space=pl.ANY), pl.BlockSpec(memory_space=pl.ANY),
                      w_spec(w_gate), w_spec(w_up), w_spec(w_down)],
            out_specs=pl.BlockSpec(memory_space=pl.ANY),
            scratch_shapes=[pltpu.SMEM((PLAN_SLOTS, 1, 2 * MOE_BLOCK), I32), buf, buf,
                            pltpu.SemaphoreType.DMA((PLAN_SLOTS,)), pltpu.SemaphoreType.DMA((ROW_BUFS,)),
                            pltpu.SemaphoreType.DMA((ROW_BUFS,))]),
        out_shape=jax.ShapeDtypeStruct(((t * TOP_K_IN_GROUP + ROW_BUFS * MOE_BLOCK) * SLAB, LANES), F32),
        compiler_params=_params("arbitrary"),
    )(blk_expert, n_used, plan, xn_slabs, w_gate, w_up, w_down)


def _combine_kernel(h_ref, y_ref, gate_ref, o_ref):
    gate = gate_ref[...]
    rows = h_ref.shape[0]
    y0 = _load_slabs(y_ref, rows, TOP_K_IN_GROUP * SLAB)
    y1 = _load_slabs(y_ref, rows, TOP_K_IN_GROUP * SLAB, SLAB)
    o_ref[...] = h_ref[...] + gate[:, 0:1] * y0 + gate[:, 1:2] * y1


def _combine(h, y_slabs, gates):
    t = h.shape[0]
    tm = ROW_TILE
    return pl.pallas_call(
        _combine_kernel,
        name="moe_combine",
        grid=(t // tm,),
        in_specs=[pl.BlockSpec((tm, D_MODEL), lambda i: (i, 0)),
                  pl.BlockSpec((tm * TOP_K_IN_GROUP * SLAB, LANES), lambda i: (i, 0)),
                  pl.BlockSpec((tm, TOP_K_IN_GROUP), lambda i: (i, 0))],
        out_specs=pl.BlockSpec((tm, D_MODEL), lambda i: (i, 0)),
        out_shape=jax.ShapeDtypeStruct((t, D_MODEL), F32),
        compiler_params=_params("arbitrary"),
    )(h, y_slabs, gates.T)


def _moe(h, routed, w_gate, w_up, w_down, layer):
    xn_slabs, ids, gates = routed
    y_slabs = _experts(xn_slabs, ids, w_gate, w_up, w_down, layer)
    return _combine(h, y_slabs, gates)


def _even_mixer(h, g, w_in, cmp_pe, w_cmp1, w_cmp2, q_norm_g, k_norm_g, ret_norm_g, w_out, router, b, s):
    ret, ret_kt, nq, chunks, ks_aug, vs, kw_aug, vw, gates = _even_inproj(h, g, w_in, q_norm_g, k_norm_g, s)
    y_ret = _retention(ret, ret_kt, ret_norm_g, b, s)
    cmp_kv = _compress(chunks, cmp_pe, w_cmp1, w_cmp2, k_norm_g, b, s)
    y_nsa = _nsa_attention(nq, cmp_kv, ks_aug, vs, kw_aug, vw, gates, b, s)
    return _even_outproj(h, y_ret, y_nsa, w_out, router)


def kernel(x, mix_norm_g, ffn_norm_g, ev_w_in, ev_cmp_pe, ev_w_cmp1, ev_w_cmp2, ev_q_norm_g, ev_k_norm_g, ev_ret_norm_g, ev_w_out, od_w_in, od_conv_w, od_conv_b, od_w_out, moe_wg_group, moe_bg_group, moe_wg_expert, moe_bg_expert, moe_w_gate, moe_w_up, moe_w_down):
    b, s, d = x.shape
    h = x.reshape(b * s, d)
    for layer in range(mix_norm_g.shape[0]):
        i = layer // 2
        router = (ffn_norm_g[layer], moe_wg_group[layer], moe_bg_group[layer], moe_wg_expert[layer], moe_bg_expert[layer])
        if layer % 2 == 0:
            h, *routed = _even_mixer(h, mix_norm_g[layer], ev_w_in[i], ev_cmp_pe[i], ev_w_cmp1[i], ev_w_cmp2[i],
                                     ev_q_norm_g[i], ev_k_norm_g[i], ev_ret_norm_g[i], ev_w_out[i], router, b, s)
        else:
            h, *routed = _odd_mixer(h, mix_norm_g[layer], od_w_in[i], od_conv_w[i], od_conv_b[i], od_w_out[i], router, b, s)
        h = _moe(h, routed, moe_w_gate, moe_w_up, moe_w_down, layer)
    return h.reshape(b, s, d)
```

```python
import functools

import jax
import jax.numpy as jnp
import numpy as np
from jax import lax
from jax.experimental import pallas as pl
from jax.experimental.pallas import tpu as pltpu

F32 = jnp.float32
BF16 = jnp.bfloat16
I32 = jnp.int32
HIGHEST = lax.Precision.HIGHEST

D_MODEL = 1024
HEAD_DIM = 64
RET_HEADS = 8
NSA_HEADS = 8
NSA_KV_GROUPS = 2
NSA_GROUP_SIZE = NSA_HEADS // NSA_KV_GROUPS
RET_W = RET_HEADS * HEAD_DIM
NSA_W = NSA_HEADS * HEAD_DIM
NSA_KV_W = NSA_KV_GROUPS * HEAD_DIM
N_BRANCHES = 3
RET_CHUNK = 128
CMP_STRIDE = 16
CMP_BLOCK = 2 * CMP_STRIDE
CMP_HIDDEN = 128
SEL_BLOCK = 64
TOP_N = 8
WINDOW = 256
N_GROUPS = 4
EXPERTS_PER_GROUP = 8
N_EXPERTS = N_GROUPS * EXPERTS_PER_GROUP
TOP_K_IN_GROUP = 2
D_EXPERT = 256
RMS_EPS = 1e-6
NEG_INF = -1e30
ATTN_SCALE = HEAD_DIM ** -0.5

LANES = 128
VMEM_LIMIT = 48 * 1024 * 1024
ROW_TILE = 512
NSA_Q_TILE = 256
NSA_KV_CHUNK = 512
MOE_BLOCK = 256
ROUTER_ROWS = 40
SLAB = D_MODEL // LANES
PLAN_SLOTS = 8
ROW_BUFS = 4


def _params(*sem):
    return pltpu.CompilerParams(dimension_semantics=sem, vmem_limit_bytes=VMEM_LIMIT)


def _rms(x, g):
    return x * lax.rsqrt(jnp.mean(x * x, axis=-1, keepdims=True) + RMS_EPS) * g


def _group_rms(x, bd, gain):
    sq = x * x
    hi = sq.astype(BF16)
    lo = (sq - hi.astype(F32)).astype(BF16)
    ms = jnp.dot(hi, bd, preferred_element_type=F32) + jnp.dot(lo, bd, preferred_element_type=F32)
    return x * lax.rsqrt(ms + RMS_EPS) * gain


def _sigmoid(x):
    return 1.0 / (1.0 + jnp.exp(-x))


def _block_diag_mean(n):
    idx = np.arange(n) // HEAD_DIM
    return jnp.asarray((idx[:, None] == idx[None, :]).astype(np.float32) / HEAD_DIM, BF16)


def _even_inproj_kernel(x_ref, g_ref, wret_ref, wkt_ref, wnq_ref, wkv_ref, wng_ref, qgain_ref, kgain_ref, bd512_ref,
                        bd128_ref, ret_ref, kt_ref, nq_ref, chunk_ref, ks_ref, vs_ref, kw_ref, vw_ref, gate_ref, kcv_ref, *, seq_tiles):
    tm = x_ref.shape[0]
    xn = _rms(x_ref[...], g_ref[...]).astype(BF16)
    ret_ref[...] = jnp.dot(xn, wret_ref[...], preferred_element_type=F32).astype(BF16)
    kt_ref[...] = lax.dot_general(wkt_ref[...], xn, (((1,), (1,)), ((), ())), preferred_element_type=F32).astype(BF16)
    nq = jnp.dot(xn, wnq_ref[...], preferred_element_type=F32)
    nq_ref[...] = _group_rms(nq, bd512_ref[...], qgain_ref[...]).astype(BF16)
    kv = jnp.dot(xn, wkv_ref[...], preferred_element_type=F32)
    w = NSA_KV_W
    n_chunk = tm // CMP_STRIDE
    for kv_i in range(2):
        kcv_ref[kv_i] = kv[:, kv_i * w:(kv_i + 1) * w]
        for j in range(CMP_STRIDE // 2):
            even = kcv_ref[kv_i, pl.ds(2 * j, n_chunk, stride=CMP_STRIDE), :]
            odd = kcv_ref[kv_i, pl.ds(2 * j + 1, n_chunk, stride=CMP_STRIDE), :]
            for g in range(NSA_KV_GROUPS):
                sl = slice(g * HEAD_DIM, (g + 1) * HEAD_DIM)
                pair = jnp.concatenate([even[:, sl], odd[:, sl]], axis=-1).astype(BF16)
                chunk_ref[kv_i, g, :, j * LANES:(j + 1) * LANES] = pair
    ks = _group_rms(kv[:, 2 * w:3 * w], bd128_ref[...], kgain_ref[0:1, :])
    kw = _group_rms(kv[:, 4 * w:5 * w], bd128_ref[...], kgain_ref[1:2, :])
    vs, vw = kv[:, 3 * w:4 * w], kv[:, 5 * w:6 * w]
    pos = (pl.program_id(0) % seq_tiles) * tm + lax.broadcasted_iota(I32, (tm, HEAD_DIM), 0)
    lane = lax.broadcasted_iota(I32, (tm, HEAD_DIM), 1)
    blk = lax.shift_right_arithmetic(pos, int(np.log2(SEL_BLOCK)))
    feat_w = jnp.where(lane == 0, blk, jnp.where(lane == 1, pos & (SEL_BLOCK - 1), 0)).astype(F32)
    feat_s = feat_w + jnp.where(lane == blk + 2, 1.0, 0.0)
    ones_col = jnp.where(lane == 0, 1.0, 0.0)
    for g in range(NSA_KV_GROUPS):
        sl = slice(g * HEAD_DIM, (g + 1) * HEAD_DIM)
        ks_ref[g] = jnp.concatenate([ks[:, sl], feat_s], axis=-1).astype(BF16)
        kw_ref[g] = jnp.concatenate([kw[:, sl], feat_w], axis=-1).astype(BF16)
        vs_ref[g] = jnp.concatenate([vs[:, sl], ones_col], axis=-1).astype(BF16)
        vw_ref[g] = jnp.concatenate([vw[:, sl], ones_col], axis=-1).astype(BF16)
    ng = jnp.dot(xn, wng_ref[...], preferred_element_type=F32)
    for g in range(NSA_KV_GROUPS):
        gate_ref[g] = ng[:, g * LANES:(g + 1) * LANES]


def _even_inproj(h, g, w_in, q_norm_g, k_norm_g, s):
    t = h.shape[0]
    tm = ROW_TILE
    c_ret, c_nq, c_kv = 4 * RET_W, 4 * RET_W + NSA_W, 4 * RET_W + NSA_W + 6 * NSA_KV_W
    n_gate = NSA_GROUP_SIZE * N_BRANCHES
    flat = CMP_STRIDE * HEAD_DIM
    wb = w_in.astype(BF16)
    wret = jnp.concatenate([wb[:, :RET_W], wb[:, 2 * RET_W:c_ret]], axis=1)
    wkt = wb[:, RET_W:2 * RET_W].T
    wnq, wkv = wb[:, c_ret:c_nq], wb[:, c_nq:c_kv]
    wng = jnp.concatenate([jnp.pad(wb[:, c_kv + gi * n_gate:c_kv + (gi + 1) * n_gate], ((0, 0), (0, LANES - n_gate)))
                           for gi in range(NSA_KV_GROUPS)], axis=1)
    qgain = jnp.tile(q_norm_g, NSA_HEADS)[None, :]
    kgain = jnp.stack([jnp.tile(k_norm_g[1], NSA_KV_GROUPS), jnp.tile(k_norm_g[2], NSA_KV_GROUPS)])
    full = lambda a: pl.BlockSpec(a.shape, lambda i: (0,) * a.ndim)
    row = lambda n: pl.BlockSpec((tm, n), lambda i: (i, 0))
    grp = lambda n: pl.BlockSpec((NSA_KV_GROUPS, tm, n), lambda i: (0, i, 0))
    grp_shape = lambda n, dt: jax.ShapeDtypeStruct((NSA_KV_GROUPS, t, n), dt)
    bd512, bd128 = _block_diag_mean(NSA_W), _block_diag_mean(NSA_KV_W)
    gg = g[None, :]
    return pl.pallas_call(
        functools.partial(_even_inproj_kernel, seq_tiles=s // tm),
        name="even_inproj",
        grid=(t // tm,),
        in_specs=[row(D_MODEL), full(gg), full(wret), full(wkt), full(wnq), full(wkv), full(wng), full(qgain),
                  full(kgain), full(bd512), full(bd128)],
        out_specs=[row(3 * RET_W), pl.BlockSpec((RET_W, tm), lambda i: (0, i)), row(NSA_W),
                   pl.BlockSpec((2, NSA_KV_GROUPS, tm // CMP_STRIDE, flat), lambda i: (0, 0, i, 0)),
                   grp(LANES), grp(LANES), grp(LANES), grp(LANES), grp(LANES)],
        out_shape=[jax.ShapeDtypeStruct((t, 3 * RET_W), BF16), jax.ShapeDtypeStruct((RET_W, t), BF16),
                   jax.ShapeDtypeStruct((t, NSA_W), BF16), jax.ShapeDtypeStruct((2, NSA_KV_GROUPS, t // CMP_STRIDE, flat), BF16), grp_shape(LANES, BF16),
                   grp_shape(LANES, BF16), grp_shape(LANES, BF16), grp_shape(LANES, BF16), grp_shape(LANES, F32)],
        scratch_shapes=[pltpu.VMEM((2, tm, NSA_KV_W), F32)],
        compiler_params=_params("arbitrary"),
    )(h, gg, wret, wkt, wnq, wkv, wng, qgain, kgain, bd512, bd128)


def _retention_kernel(q_ref, kt_ref, v_ref, rg_ref, decay_ref, qd_ref, kd_ref, cd_ref, diag_ref, bd_ref, gain_ref, o_ref):
    n_chunks = q_ref.shape[0] // RET_CHUNK
    gain, diag, bd = gain_ref[...], diag_ref[...], bd_ref[...]
    qd, kd, cd = qd_ref[0], kd_ref[0], cd_ref[0]
    first = lax.broadcasted_iota(I32, (RET_CHUNK, LANES), 1) < HEAD_DIM
    zero = jnp.zeros((RET_CHUNK, LANES), BF16)

    def body(c, state):
        r0 = pl.multiple_of(c * RET_CHUNK, RET_CHUNK)
        rows = pl.ds(r0, RET_CHUNK)
        q2, v2 = q_ref[rows, :], v_ref[rows, :]
        kt = kt_ref[:, rows].astype(F32) * ATTN_SCALE
        kt_b = kt.astype(BF16)
        s0 = jnp.dot(jnp.where(first, q2, zero), kt_b, preferred_element_type=F32) * decay_ref[0]
        s1 = jnp.dot(jnp.where(first, zero, q2), kt_b, preferred_element_type=F32) * decay_ref[1]
        y = jnp.where(first, jnp.dot(s0.astype(BF16), v2, preferred_element_type=F32),
                      jnp.dot(s1.astype(BF16), v2, preferred_element_type=F32))
        y = y + jnp.dot((q2.astype(F32) * qd).astype(BF16), state.astype(BF16), preferred_element_type=F32)
        kv = jnp.dot((kt * kd).astype(BF16), v2, preferred_element_type=F32)
        rg = rg_ref[rows, :].astype(F32)
        o_ref[rows, :] = (_group_rms(y, bd, gain) * (rg * _sigmoid(rg))).astype(BF16)
        return cd * state + diag * kv

    lax.fori_loop(0, n_chunks, body, jnp.zeros((LANES, LANES), F32), unroll=True)


def _retention_tables():
    h = np.arange(RET_HEADS, dtype=np.float64)
    log_g = np.log(1.0 - 2.0 ** (-5.0 - h))
    pos = np.arange(RET_CHUNK, dtype=np.float64)
    diff = pos[:, None] - pos[None, :]
    decay = np.where(diff >= 0, np.exp(log_g[:, None, None] * np.maximum(diff, 0.0)), 0.0)
    qd = np.exp(log_g[:, None] * (pos + 1.0))
    kd = np.exp(log_g[:, None] * (RET_CHUNK - 1 - pos))
    cd = np.exp(log_g * RET_CHUNK)
    pairs = RET_HEADS // 2
    rep = lambda a: np.repeat(a, HEAD_DIM, axis=1)
    qd2 = rep(qd.reshape(pairs, 2, RET_CHUNK)).transpose(0, 2, 1)
    kd2 = rep(kd.reshape(pairs, 2, RET_CHUNK))
    cd2 = rep(cd.reshape(pairs, 2, 1)) * np.ones((1, 1, LANES))
    half = np.arange(LANES) // HEAD_DIM
    diag = (half[:, None] == half[None, :]).astype(np.float64)
    return [jnp.asarray(a, F32) for a in (decay.reshape(pairs, 2, RET_CHUNK, RET_CHUNK), qd2, kd2, cd2, diag)]


def _retention(ret, ret_kt, ret_norm_g, b, s):
    t = b * s
    n_pairs = RET_HEADS // 2
    decay, qd, kd, cd, diag = _retention_tables()
    bd = _block_diag_mean(LANES)
    col = lambda off: pl.BlockSpec((s, LANES), lambda bi, p: (bi, off * n_pairs + p))
    tab = lambda a: pl.BlockSpec((1,) + a.shape[1:], lambda bi, p: (p,) + (0,) * (a.ndim - 1))
    full = lambda a: pl.BlockSpec(a.shape, lambda bi, p: (0, 0))
    gain = jnp.tile(ret_norm_g, 2)[None, :]
    return pl.pallas_call(
        _retention_kernel,
        name="retention",
        grid=(b, n_pairs),
        in_specs=[col(0), pl.BlockSpec((LANES, s), lambda bi, p: (p, bi)), col(1), col(2),
                  pl.BlockSpec((None, 2, RET_CHUNK, RET_CHUNK), lambda bi, p: (p, 0, 0, 0)), tab(qd), tab(kd), tab(cd),
                  full(diag), full(bd), full(gain)],
        out_specs=pl.BlockSpec((s, LANES), lambda bi, p: (bi, p)),
        out_shape=jax.ShapeDtypeStruct((t, RET_W), BF16),
        compiler_params=_params("arbitrary", "arbitrary"),
    )(ret, ret_kt, ret, ret, decay, qd, kd, cd, diag, bd, gain)


def _compress_kernel(x_ref, pe_ref, w1_ref, w2_ref, kgain_ref, o_ref):
    is_key = pl.program_id(0) == 0
    half = CMP_STRIDE * HEAD_DIM
    x = x_ref[...]
    n = x.shape[0]
    a = jnp.dot(x, w1_ref[0:half, :], preferred_element_type=F32)
    bm = jnp.dot(x, w1_ref[half:2 * half, :], preferred_element_type=F32)
    pew = jnp.dot(pe_ref[...], w1_ref[...], preferred_element_type=F32)[0:1, :]
    pre = a + pltpu.roll(bm, n - 1, axis=0) + pew
    hid = 0.5 * pre * (1.0 + jnp.tanh(np.sqrt(2.0 / np.pi) * (pre + 0.044715 * pre * pre * pre)))
    out = jnp.dot(hid.astype(BF16), w2_ref[...], preferred_element_type=F32)
    normed = _rms(out, kgain_ref[...])
    o_ref[...] = jnp.where(is_key, normed, out).astype(BF16)


def _compress(chunks, cmp_pe, w_cmp1, w_cmp2, k_norm_g, b, s):
    n = s // CMP_STRIDE
    g = NSA_KV_GROUPS
    flat = CMP_STRIDE * HEAD_DIM
    pe = jnp.broadcast_to(cmp_pe.reshape(2, 1, 2 * flat), (2, 8, 2 * flat)).astype(BF16)
    w1, w2 = w_cmp1.astype(BF16), w_cmp2.astype(BF16)
    kgain = k_norm_g[0][None, :]
    return pl.pallas_call(
        _compress_kernel,
        name="nsa_compress",
        grid=(2, b, g),
        in_specs=[pl.BlockSpec((None, None, n, flat), lambda kv, bi, gi: (kv, gi, bi, 0)),
                  pl.BlockSpec((None, 8, 2 * flat), lambda kv, bi, gi: (kv, 0, 0)),
                  pl.BlockSpec((None, 2 * flat, CMP_HIDDEN), lambda kv, bi, gi: (kv, 0, 0)),
                  pl.BlockSpec((None, CMP_HIDDEN, HEAD_DIM), lambda kv, bi, gi: (kv, 0, 0)),
                  pl.BlockSpec(kgain.shape, lambda kv, bi, gi: (0, 0))],
        out_specs=pl.BlockSpec((None, None, None, n, HEAD_DIM), lambda kv, bi, gi: (kv, bi, gi, 0, 0)),
        out_shape=jax.ShapeDtypeStruct((2, b, g, n, HEAD_DIM), BF16),
        compiler_params=_params("arbitrary", "arbitrary", "arbitrary"),
    )(chunks, pe, w1, w2, kgain)


def _nsa_kernel(slopes_ref, q_ref, kcmp_ref, vcmp_ref, ks_ref, vs_ref, kw_ref, vw_ref, gate_ref, ovl_ref, place_ref,
                o_ref, *, n_sel, n_cmp):
    r_heads = NSA_GROUP_SIZE
    tq = q_ref.shape[0]
    gi = pl.program_id(1)
    t0 = pl.program_id(2) * tq
    nt = (((1,), (1,)), ((), ()))
    tn = (((0,), (0,)), ((), ()))

    q = q_ref[...]
    lane = lax.broadcasted_iota(I32, (tq, HEAD_DIM), 1)
    qs, feats = [], []
    for r in range(r_heads):
        slope = slopes_ref[gi * r_heads + r]
        qs.append((q[:, r * HEAD_DIM:(r + 1) * HEAD_DIM].astype(F32) * ATTN_SCALE).astype(BF16))
        feats.append(jnp.where(lane == 0, slope * SEL_BLOCK, jnp.where(lane == 1, slope, 0.0)))

    def stack_q(extra):
        return jnp.concatenate([jnp.concatenate([qs[r], (feats[r] + extra).astype(BF16)], axis=-1)
                                for r in range(r_heads)], axis=0)

    q4 = stack_q(0.0)
    t_col = t0 + lax.broadcasted_iota(I32, (tq, 1), 0)

    def head_rows(x):
        return [x[r * tq:(r + 1) * tq] for r in range(r_heads)]

    n_pad = kcmp_ref.shape[0]
    c_idx = lax.broadcasted_iota(I32, (tq, n_pad), 1)
    mask_c = (t_col >= c_idx * CMP_STRIDE + (CMP_BLOCK - 1)) & (c_idx < n_cmp)
    s4 = lax.dot_general(q4, kcmp_ref[...], nt, preferred_element_type=F32)
    ps = []
    for s in head_rows(s4):
        s = jnp.where(mask_c, s, NEG_INF)
        p = jnp.where(mask_c, jnp.exp(s - jnp.max(s, axis=-1, keepdims=True)), 0.0)
        ps.append(p * (1.0 / jnp.maximum(jnp.sum(p, axis=-1, keepdims=True), 1e-30)))
    p4 = jnp.concatenate(ps, axis=0).astype(BF16)
    o_cmp = head_rows(jnp.dot(p4, vcmp_ref[...], preferred_element_type=F32))

    n_rows = ovl_ref.shape[0]
    imp4 = lax.dot_general(ovl_ref[...], p4, nt, preferred_element_type=F32)
    imp = imp4[:, 0:tq]
    for r in range(1, r_heads):
        imp = imp + imp4[:, r * tq:(r + 1) * tq]
    j_idx = lax.broadcasted_iota(I32, (n_rows, tq), 0)
    t_row = t0 + lax.broadcasted_iota(I32, (n_rows, tq), 1)
    q_blk = lax.shift_right_arithmetic(t_row, int(np.log2(SEL_BLOCK)))
    valid = (j_idx * SEL_BLOCK <= t_row) & (j_idx < n_sel)
    forced = (j_idx == 0) | (j_idx == q_blk) | (j_idx == q_blk - 1)
    val = jnp.where(j_idx < n_sel, jnp.where(forced, imp + 1e3, jnp.where(valid, imp, -1e3)), -2e3)
    rank = jnp.zeros((n_rows, tq), F32)
    for k in range(n_sel):
        vk = val[k:k + 1, :]
        rank = rank + jnp.where(vk > val, 1.0, jnp.where(vk == val, jnp.where(j_idx > k, 1.0, 0.0), 0.0))
    unselected = jnp.where(valid, jnp.where(rank < TOP_N, 0.0, NEG_INF), NEG_INF).astype(BF16)
    q4_sel = stack_q(lax.dot_general(unselected, place_ref[...], tn, preferred_element_type=F32))

    ck = NSA_KV_CHUNK

    def sel_step(c, carry, causal):
        c0 = pl.multiple_of(c * ck, ck)
        s4 = lax.dot_general(q4_sel, ks_ref[pl.ds(c0, ck), :], nt, preferred_element_type=F32)
        if causal:
            visible = t_col >= c0 + lax.broadcasted_iota(I32, (tq, ck), 1)
        ps, new = [], []
        for r, s in enumerate(head_rows(s4)):
            if causal:
                s = jnp.where(visible, s, NEG_INF)
            m_old, acc_old = carry[r]
            m_new = jnp.maximum(m_old, jnp.max(s, axis=-1, keepdims=True))
            ps.append(jnp.exp(s - m_new))
            new.append((m_new, jnp.exp(m_old - m_new) * acc_old))
        pv = jnp.dot(jnp.concatenate(ps, axis=0).astype(BF16), vs_ref[pl.ds(c0, ck), :], preferred_element_type=F32)
        return tuple((m, acc + o) for (m, acc), o in zip(new, head_rows(pv)))

    def normalise(acc):
        return acc[:, 0:HEAD_DIM] * (1.0 / acc[:, HEAD_DIM:HEAD_DIM + 1])

    init = tuple((jnp.full((tq, 1), NEG_INF, F32), jnp.zeros((tq, LANES), F32)) for _ in range(r_heads))
    c_last = t0 // ck
    carry = lax.fori_loop(0, c_last, lambda c, carry: sel_step(c, carry, False), init)
    o_sel = [normalise(acc) for (_, acc) in sel_step(c_last, carry, True)]

    n_win = WINDOW + tq
    w0 = pl.multiple_of(jnp.maximum(t0 - WINDOW, 0), tq)
    dist_w = t_col - (w0 + lax.broadcasted_iota(I32, (tq, n_win), 1))
    mask_w = (dist_w >= 0) & (dist_w < WINDOW)
    s4 = lax.dot_general(q4, kw_ref[pl.ds(w0, n_win), :], nt, preferred_element_type=F32)
    ps = []
    for s in head_rows(s4):
        s = jnp.where(mask_w, s, NEG_INF)
        ps.append(jnp.exp(s - jnp.max(s, axis=-1, keepdims=True)))
    pv = jnp.dot(jnp.concatenate(ps, axis=0).astype(BF16), vw_ref[pl.ds(w0, n_win), :], preferred_element_type=F32)
    o_win = [normalise(o) for o in head_rows(pv)]

    gate = _sigmoid(gate_ref[...])
    outs = []
    for r in range(r_heads):
        c = N_BRANCHES * r
        outs.append(gate[:, c:c + 1] * o_cmp[r] + gate[:, c + 1:c + 2] * o_sel[r] + gate[:, c + 2:c + 3] * o_win[r])
    o_ref[...] = jnp.concatenate(outs, axis=-1).astype(BF16)


def _nsa_attention(nq, cmp_kv, ks_aug, vs, kw_aug, vw, gates, b, s):
    t = b * s
    g, r = NSA_KV_GROUPS, NSA_GROUP_SIZE
    tq = NSA_Q_TILE
    n_q = s // tq
    n_sel = s // SEL_BLOCK
    n_pad = s // CMP_STRIDE
    n_cmp = n_pad - 1
    n_rows = -(-n_sel // 8) * 8
    assert 2 + n_sel <= HEAD_DIM and s % NSA_KV_CHUNK == 0 and s >= WINDOW + tq and NSA_KV_CHUNK % tq == 0
    slopes = jnp.asarray(2.0 ** (-8.0 * np.arange(1, NSA_HEADS + 1) / NSA_HEADS), F32)
    cmp_start = np.arange(n_pad) * CMP_STRIDE
    cmp_end = cmp_start + CMP_BLOCK - 1
    feat = np.zeros((n_pad, HEAD_DIM), np.float32)
    feat[:, 0], feat[:, 1] = cmp_end // SEL_BLOCK, cmp_end % SEL_BLOCK
    kcmp_aug = jnp.concatenate([cmp_kv[0], jnp.broadcast_to(jnp.asarray(feat, BF16), cmp_kv[0].shape)], axis=-1)
    sel_start = np.arange(n_rows) * SEL_BLOCK
    ovl = ((cmp_start[None, :] < sel_start[:, None] + SEL_BLOCK) & (cmp_start[None, :] + CMP_BLOCK > sel_start[:, None])
           & (np.arange(n_pad)[None, :] < n_cmp) & (np.arange(n_rows)[:, None] < n_sel))
    ovl = jnp.asarray(ovl.astype(np.float32), BF16)
    place = jnp.asarray((np.arange(n_rows)[:, None] + 2 == np.arange(HEAD_DIM)[None, :]).astype(np.float32), BF16)
    seq = lambda w: pl.BlockSpec((None, s, w), lambda bi, gi, qi: (gi, bi, 0))
    cmp = lambda w: pl.BlockSpec((None, None, n_pad, w), lambda bi, gi, qi: (bi, gi, 0, 0))
    return pl.pallas_call(
        functools.partial(_nsa_kernel, n_sel=n_sel, n_cmp=n_cmp),
        name="nsa_attention",
        grid=(b, g, n_q),
        in_specs=[pl.BlockSpec(memory_space=pltpu.SMEM),
                  pl.BlockSpec((tq, r * HEAD_DIM), lambda bi, gi, qi: (bi * n_q + qi, gi)),
                  cmp(2 * HEAD_DIM), cmp(HEAD_DIM), seq(LANES), seq(LANES), seq(LANES), seq(LANES),
                  pl.BlockSpec((None, tq, LANES), lambda bi, gi, qi: (gi, bi * n_q + qi, 0)),
                  pl.BlockSpec(ovl.shape, lambda bi, gi, qi: (0, 0)),
                  pl.BlockSpec(place.shape, lambda bi, gi, qi: (0, 0))],
        out_specs=pl.BlockSpec((tq, r * HEAD_DIM), lambda bi, gi, qi: (bi * n_q + qi, gi)),
        out_shape=jax.ShapeDtypeStruct((t, NSA_W), BF16),
        compiler_params=_params("arbitrary", "arbitrary", "arbitrary"),
    )(slopes, nq, kcmp_aug, cmp_kv[1], ks_aug, vs, kw_aug, vw, gates, ovl, place)


def _even_outproj_kernel(h_ref, a_ref, b_ref, wa_ref, wb_ref, g_ref, wr_ref, br_ref, o_ref, xn_ref, ids_ref, gates_ref):
    y = jnp.dot(a_ref[...], wa_ref[...], preferred_element_type=F32)
    y = y + jnp.dot(b_ref[...], wb_ref[...], preferred_element_type=F32)
    h = h_ref[...] + y
    o_ref[...] = h
    _route(h, g_ref, wr_ref, br_ref, xn_ref, ids_ref, gates_ref)


def _even_outproj(h, y_ret, y_nsa, w_out, router):
    t = h.shape[0]
    tm = ROW_TILE
    wb = w_out.astype(BF16)
    wa, wbt = wb[:RET_W], wb[RET_W:]
    gg, wr, br = _router_operands(*router, tm)
    row = lambda n: pl.BlockSpec((tm, n), lambda i: (i, 0))
    full = lambda a: pl.BlockSpec(a.shape, lambda i: (0, 0))
    r_specs, r_shapes = _router_outputs(t, tm, lambda i: i)
    return pl.pallas_call(
        _even_outproj_kernel,
        name="even_outproj",
        grid=(t // tm,),
        in_specs=[row(D_MODEL), row(RET_W), row(NSA_W), full(wa), full(wbt), full(gg), full(wr), full(br)],
        out_specs=[row(D_MODEL)] + r_specs,
        out_shape=[jax.ShapeDtypeStruct((t, D_MODEL), F32)] + r_shapes,
        compiler_params=_params("arbitrary"),
    )(h, y_ret, y_nsa, wa, wbt, gg, wr, br)


def _odd_mixer_kernel(h_ref, y_ref, mg_ref, g_ref, win_ref, cw_ref, cb_ref, wout_ref, rg_ref, wr_ref, br_ref,
                      o_ref, xn_ref, ids_ref, gates_ref, tail_ref):
    @pl.when(pl.program_id(1) == 0)
    def _():
        tail_ref[...] = jnp.zeros_like(tail_ref)

    ts = h_ref.shape[0]
    h = _moe_sum(h_ref, y_ref, mg_ref)
    xn = _rms(h, g_ref[...]).astype(BF16)
    proj = jnp.dot(xn, win_ref[...], preferred_element_type=F32)
    gate_b, gate_c, hid = proj[:, 0:D_MODEL], proj[:, D_MODEL:2 * D_MODEL], proj[:, 2 * D_MODEL:3 * D_MODEL]
    u = gate_c * hid
    row = lax.broadcasted_iota(I32, (ts, D_MODEL), 0)
    tail = tail_ref[...]
    prev1, prev2 = tail[7:8, :], tail[6:7, :]
    u1 = jnp.where(row >= 1, pltpu.roll(u, 1, axis=0), prev1)
    u2 = jnp.where(row >= 2, pltpu.roll(u, 2, axis=0), jnp.where(row == 1, prev1, prev2))
    cw = cw_ref[...]
    y = cw[0:1, :] * u2 + cw[1:2, :] * u1 + cw[2:3, :] * u + cb_ref[...]
    tail_ref[...] = u[ts - 8:ts, :]
    z = (gate_b * y).astype(BF16)
    h_new = h + jnp.dot(z, wout_ref[...], preferred_element_type=F32)
    o_ref[...] = h_new
    _route(h_new, rg_ref, wr_ref, br_ref, xn_ref, ids_ref, gates_ref)


def _odd_mixer(h, y_slabs, moe_gates, g, w_in, conv_w, conv_b, w_out, router, b, s):
    t = b * s
    ts = ROW_TILE
    n_s = s // ts
    win, wout = w_in.astype(BF16), w_out.astype(BF16)
    gg, cb = g[None, :], conv_b[None, :]
    cw = jnp.pad(conv_w, ((0, 8 - conv_w.shape[0]), (0, 0)))
    rg, wr, br = _router_operands(*router, ts)
    full = lambda a: pl.BlockSpec(a.shape, lambda bi, si: (0, 0), pipeline_mode=pl.Buffered(1))
    row = pl.BlockSpec((ts, D_MODEL), lambda bi, si: (bi * n_s + si, 0))
    y_spec = pl.BlockSpec((ts * TOP_K_IN_GROUP * SLAB, LANES), lambda bi, si: (bi * n_s + si, 0))
    mg_spec = pl.BlockSpec((ts, TOP_K_IN_GROUP), lambda bi, si: (bi * n_s + si, 0))
    r_specs, r_shapes = _router_outputs(t, ts, lambda bi, si: bi * n_s + si)
    return pl.pallas_call(
        _odd_mixer_kernel,
        name="odd_mixer",
        grid=(b, n_s),
        in_specs=[row, y_spec, mg_spec, full(gg), full(win), full(cw), full(cb), full(wout), full(rg), full(wr), full(br)],
        out_specs=[row] + r_specs,
        out_shape=[jax.ShapeDtypeStruct((t, D_MODEL), F32)] + r_shapes,
        scratch_shapes=[pltpu.VMEM((8, D_MODEL), F32)],
        compiler_params=_params("arbitrary", "arbitrary"),
    )(h, y_slabs, moe_gates.T, gg, win, cw, cb, wout, rg, wr, br)


def _store_slabs(ref, x, stride):
    rows = x.shape[0]
    for j in range(SLAB):
        ref[pl.ds(j, rows, stride=stride), :] = x[:, j * LANES:(j + 1) * LANES]


def _load_slabs(ref, rows, stride, offset=0):
    return jnp.concatenate([ref[pl.ds(offset + j, rows, stride=stride), :] for j in range(SLAB)], axis=-1)


def _route(h, g_ref, wr_ref, br_ref, xn_ref, ids_ref, gates_ref):
    xn = _rms(h, g_ref[...])
    _store_slabs(xn_ref, xn, SLAB)
    logits = lax.dot_general(wr_ref[...], xn, (((1,), (1,)), ((), ())), precision=HIGHEST,
                             preferred_element_type=F32) + br_ref[...]
    row = lambda i: logits[i:i + 1, :]

    def softmax(xs):
        m = functools.reduce(jnp.maximum, xs)
        es = [jnp.exp(x - m) for x in xs]
        tot = functools.reduce(lambda a, c: a + c, es)
        return [e / tot for e in es]

    def argmax(ps):
        best_p, best_i = ps[0], jnp.zeros_like(ps[0], dtype=I32)
        for i in range(1, len(ps)):
            upd = ps[i] > best_p
            best_p = jnp.where(upd, ps[i], best_p)
            best_i = jnp.where(upd, i, best_i)
        return best_p, best_i

    grp_p, grp_i = argmax(softmax([row(i) for i in range(N_GROUPS)]))
    in_grp = []
    for e in range(EXPERTS_PER_GROUP):
        x = row(N_GROUPS + e)
        for gidx in range(1, N_GROUPS):
            x = jnp.where(grp_i == gidx, row(N_GROUPS + gidx * EXPERTS_PER_GROUP + e), x)
        in_grp.append(x)
    pe = softmax(in_grp)
    p1, i1 = argmax(pe)
    p2, i2 = argmax([jnp.where(i1 == e, -1.0, pe[e]) for e in range(EXPERTS_PER_GROUP)])
    tot = p1 + p2
    ids_ref[...] = jnp.concatenate([grp_i * EXPERTS_PER_GROUP + i1, grp_i * EXPERTS_PER_GROUP + i2], axis=0)
    gates_ref[...] = jnp.concatenate([grp_p * p1 / tot, grp_p * p2 / tot], axis=0)


def _router_operands(g, wg_group, bg_group, wg_expert, bg_expert, tm):
    n_logit = N_GROUPS + N_EXPERTS
    wr = jnp.pad(jnp.concatenate([wg_group, wg_expert], axis=1).T, ((0, ROUTER_ROWS - n_logit), (0, 0)))
    br = jnp.pad(jnp.concatenate([bg_group, bg_expert]), (0, ROUTER_ROWS - n_logit))
    return g[None, :], wr, jnp.broadcast_to(br[:, None], (ROUTER_ROWS, tm))


def _router_outputs(t, tm, row_index):
    specs = [pl.BlockSpec((tm * SLAB, LANES), lambda *i: (row_index(*i), 0)),
             pl.BlockSpec((TOP_K_IN_GROUP, tm), lambda *i: (0, row_index(*i))),
             pl.BlockSpec((TOP_K_IN_GROUP, tm), lambda *i: (0, row_index(*i)))]
    shapes = [jax.ShapeDtypeStruct((t * SLAB, LANES), F32), jax.ShapeDtypeStruct((TOP_K_IN_GROUP, t), I32),
              jax.ShapeDtypeStruct((TOP_K_IN_GROUP, t), F32)]
    return specs, shapes


def _expert_kernel(blk_expert_ref, n_used_ref, plan_hbm, x_hbm, wg_ref, wu_ref, wd_ref, y_hbm,
                   plan, xbuf, ybuf, sem_plan, sem_in, sem_out):
    del blk_expert_ref
    depth = ROW_BUFS
    i = pl.program_id(0)
    n_used = n_used_ref[0]
    cur, prev, ahead = i % depth, (i + depth - 1) % depth, (i + depth - 1) % depth

    def plan_copy(k):
        return pltpu.make_async_copy(plan_hbm.at[k], plan.at[k % PLAN_SLOTS], sem_plan.at[k % PLAN_SLOTS])

    def row_copy(hbm, hbm_row, buf, r, sem, to_hbm):
        h = hbm.at[pl.ds(pl.multiple_of(hbm_row * SLAB, SLAB), SLAB)]
        v = buf.at[pl.ds(r * SLAB, SLAB)]
        return pltpu.make_async_copy(v, h, sem) if to_hbm else pltpu.make_async_copy(h, v, sem)

    def wait_gather(s):
        for r in range(MOE_BLOCK):
            row_copy(x_hbm, 0, xbuf.at[s], r, sem_in.at[s], False).wait()

    def wait_scatter(s):
        for r in range(MOE_BLOCK):
            row_copy(y_hbm, 0, ybuf.at[s], r, sem_out.at[s], True).wait()

    @pl.when(i == 0)
    def _():
        ybuf[...] = jnp.zeros_like(ybuf)
        n_slot = y_hbm.shape[0] // SLAB - depth * MOE_BLOCK
        for s in range(depth - 1):
            for r in range(MOE_BLOCK):
                row_copy(y_hbm, n_slot + s * MOE_BLOCK + r, ybuf.at[s], r, sem_out.at[s], True).start()
        for s in range(depth - 1):
            wait_scatter(s)
        for k in range(depth + 1):
            plan_copy(k).start()
        for k in range(depth + 1):
            plan_copy(k).wait()
        for blk in range(depth - 1):
            for r in range(MOE_BLOCK):
                row_copy(x_hbm, plan[blk + 1, 0, r], xbuf.at[blk], r, sem_in.at[blk], False).start()

    @pl.when(i <= n_used)
    def _():
        @pl.when(i >= 1)
        def _():
            plan_copy(i + depth).wait()

        plan_copy(i + depth + 1).start()
        wait_gather(cur)

        @pl.when(i >= depth - 1)
        def _():
            wait_scatter(cur)

        nxt, prv = (i + depth) % PLAN_SLOTS, i % PLAN_SLOTS
        for r in range(MOE_BLOCK):
            row_copy(x_hbm, plan[nxt, 0, r], xbuf.at[ahead], r, sem_in.at[ahead], False).start(priority=r % 2)
            row_copy(y_hbm, plan[prv, 0, MOE_BLOCK + r], ybuf.at[prev], r, sem_out.at[prev], True).start(priority=(r + 1) % 2)
        x = _load_slabs(xbuf.at[cur], MOE_BLOCK, SLAB).astype(BF16)
        a = jnp.dot(x, wg_ref[...].astype(BF16), preferred_element_type=F32)
        u = jnp.dot(x, wu_ref[...].astype(BF16), preferred_element_type=F32)
        hid = (a * _sigmoid(a) * u).astype(BF16)
        _store_slabs(ybuf.at[cur], jnp.dot(hid, wd_ref[...].astype(BF16), preferred_element_type=F32), SLAB)

        @pl.when(i == n_used)
        def _():
            plan_copy(i + depth + 1).wait()
            for k in range(1, depth):
                wait_gather((i + k) % depth)

                @pl.when(i >= k - 1)
                def _():
                    wait_scatter((i + depth - k) % depth)


def _moe_plan(ids, t):
    n_slot = t * TOP_K_IN_GROUP
    n_blocks = n_slot // MOE_BLOCK + N_EXPERTS
    flat_e = ids.T.reshape(-1)
    order = jnp.argsort(flat_e).astype(I32)
    experts = jnp.arange(N_EXPERTS, dtype=I32)
    counts = jnp.sum(flat_e[:, None] == experts[None, :], axis=0, dtype=I32)
    n_blk = (counts + MOE_BLOCK - 1) // MOE_BLOCK
    blk_end = jnp.cumsum(n_blk)
    start = jnp.cumsum(counts) - counts
    blocks = jnp.arange(n_blocks, dtype=I32)
    blk_expert = jnp.minimum(jnp.sum(blk_end[None, :] <= blocks[:, None], axis=1, dtype=I32), N_EXPERTS - 1)
    row0 = (blocks - (blk_end - n_blk)[blk_expert]) * MOE_BLOCK
    rows = row0[:, None] + jnp.arange(MOE_BLOCK, dtype=I32)[None, :]
    is_slot = (rows >= 0) & (rows < counts[blk_expert][:, None])
    slot = order[jnp.clip(start[blk_expert][:, None] + rows, 0, n_slot - 1)]
    src = jnp.where(is_slot, slot // TOP_K_IN_GROUP, 0).astype(I32)
    dst = jnp.where(is_slot, slot, 0).astype(I32)
    valid = is_slot.astype(I32)
    tail = ROW_BUFS + 2
    n_rows = n_blocks + 1 + tail
    lead = lambda a: jnp.concatenate([jnp.zeros((1, MOE_BLOCK), I32), a, jnp.zeros((tail, MOE_BLOCK), I32)])
    trash = (n_slot + (jnp.arange(n_rows, dtype=I32)[:, None] + ROW_BUFS - 1) % ROW_BUFS * MOE_BLOCK
             + jnp.arange(MOE_BLOCK, dtype=I32)[None, :])
    dst_rows = jnp.where(lead(valid) > 0, lead(dst), trash)
    plan = jnp.concatenate([lead(src), dst_rows], axis=1)
    return blk_expert, blk_end[-1:].astype(I32), plan[:, None, :]


def _experts(xn_slabs, ids, w_gate, w_up, w_down, layer):
    t = xn_slabs.shape[0] // SLAB
    blk_expert, n_used, plan = _moe_plan(ids, t)
    n_blocks = blk_expert.shape[0]
    w_spec = lambda a: pl.BlockSpec((None, None) + a.shape[2:], lambda i, be, nu: (layer, be[i], 0, 0))
    buf = pltpu.VMEM((ROW_BUFS, MOE_BLOCK * SLAB, LANES), F32)
    return pl.pallas_call(
        _expert_kernel,
        name="moe_experts",
        grid_spec=pltpu.PrefetchScalarGridSpec(
            num_scalar_prefetch=2,
            grid=(n_blocks,),
            in_specs=[pl.BlockSpec(memory_space=pl.ANY), pl.BlockSpec(memory_space=pl.ANY),
                      w_spec(w_gate), w_spec(w_up), w_spec(w_down)],
            out_specs=pl.BlockSpec(memory_space=pl.ANY),
            scratch_shapes=[pltpu.SMEM((PLAN_SLOTS, 1, 2 * MOE_BLOCK), I32), buf, buf,
                            pltpu.SemaphoreType.DMA((PLAN_SLOTS,)), pltpu.SemaphoreType.DMA((ROW_BUFS,)),
                            pltpu.SemaphoreType.DMA((ROW_BUFS,))]),
        out_shape=jax.ShapeDtypeStruct(((t * TOP_K_IN_GROUP + ROW_BUFS * MOE_BLOCK) * SLAB, LANES), F32),
        compiler_params=_params("arbitrary"),
    )(blk_expert, n_used, plan, xn_slabs, w_gate, w_up, w_down)


def _moe_sum(h_ref, y_ref, gate_ref):
    gate = gate_ref[...]
    rows = h_ref.shape[0]
    y0 = _load_slabs(y_ref, rows, TOP_K_IN_GROUP * SLAB)
    y1 = _load_slabs(y_ref, rows, TOP_K_IN_GROUP * SLAB, SLAB)
    return h_ref[...] + gate[:, 0:1] * y0 + gate[:, 1:2] * y1


def _combine_kernel(h_ref, y_ref, gate_ref, o_ref):
    o_ref[...] = _moe_sum(h_ref, y_ref, gate_ref)


def _combine(h, y_slabs, gates):
    t = h.shape[0]
    tm = ROW_TILE
    return pl.pallas_call(
        _combine_kernel,
        name="moe_combine",
        grid=(t // tm,),
        in_specs=[pl.BlockSpec((tm, D_MODEL), lambda i: (i, 0)),
                  pl.BlockSpec((tm * TOP_K_IN_GROUP * SLAB, LANES), lambda i: (i, 0)),
                  pl.BlockSpec((tm, TOP_K_IN_GROUP), lambda i: (i, 0))],
        out_specs=pl.BlockSpec((tm, D_MODEL), lambda i: (i, 0)),
        out_shape=jax.ShapeDtypeStruct((t, D_MODEL), F32),
        compiler_params=_params("arbitrary"),
    )(h, y_slabs, gates.T)


def _moe(routed, w_gate, w_up, w_down, layer):
    xn_slabs, ids, gates = routed
    return _experts(xn_slabs, ids, w_gate, w_up, w_down, layer), gates


def _even_mixer(h, g, w_in, cmp_pe, w_cmp1, w_cmp2, q_norm_g, k_norm_g, ret_norm_g, w_out, router, b, s):
    ret, ret_kt, nq, chunks, ks_aug, vs, kw_aug, vw, gates = _even_inproj(h, g, w_in, q_norm_g, k_norm_g, s)
    y_ret = _retention(ret, ret_kt, ret_norm_g, b, s)
    cmp_kv = _compress(chunks, cmp_pe, w_cmp1, w_cmp2, k_norm_g, b, s)
    y_nsa = _nsa_attention(nq, cmp_kv, ks_aug, vs, kw_aug, vw, gates, b, s)
    return _even_outproj(h, y_ret, y_nsa, w_out, router)


def kernel(x, mix_norm_g, ffn_norm_g, ev_w_in, ev_cmp_pe, ev_w_cmp1, ev_w_cmp2, ev_q_norm_g, ev_k_norm_g, ev_ret_norm_g, ev_w_out, od_w_in, od_conv_w, od_conv_b, od_w_out, moe_wg_group, moe_bg_group, moe_wg_expert, moe_bg_expert, moe_w_gate, moe_w_up, moe_w_down):
    b, s, d = x.shape
    h = x.reshape(b * s, d)
    pending = None
    for layer in range(mix_norm_g.shape[0]):
        i = layer // 2
        router = (ffn_norm_g[layer], moe_wg_group[layer], moe_bg_group[layer], moe_wg_expert[layer], moe_bg_expert[layer])
        if layer % 2 == 0:
            if pending is not None:
                h = _combine(h, *pending)
            h, *routed = _even_mixer(h, mix_norm_g[layer], ev_w_in[i], ev_cmp_pe[i], ev_w_cmp1[i], ev_w_cmp2[i],
                                     ev_q_norm_g[i], ev_k_norm_g[i], ev_ret_norm_g[i], ev_w_out[i], router, b, s)
        else:
            h, *routed = _odd_mixer(h, *pending, mix_norm_g[layer], od_w_in[i], od_conv_w[i], od_conv_b[i],
                                    od_w_out[i], router, b, s)
        pending = _moe(routed, moe_w_gate, moe_w_up, moe_w_down, layer)
    return _combine(h, *pending).reshape(b, s, d)
```

```python
import functools

import jax
import jax.numpy as jnp
import numpy as np
from jax import lax
from jax.experimental import pallas as pl
from jax.experimental.pallas import tpu as pltpu

F32 = jnp.float32
BF16 = jnp.bfloat16
I32 = jnp.int32
HIGHEST = lax.Precision.HIGHEST

D_MODEL = 1024
HEAD_DIM = 64
RET_HEADS = 8
NSA_HEADS = 8
NSA_KV_GROUPS = 2
NSA_GROUP_SIZE = NSA_HEADS // NSA_KV_GROUPS
RET_W = RET_HEADS * HEAD_DIM
NSA_W = NSA_HEADS * HEAD_DIM
NSA_KV_W = NSA_KV_GROUPS * HEAD_DIM
N_BRANCHES = 3
RET_CHUNK = 128
CMP_STRIDE = 16
CMP_BLOCK = 2 * CMP_STRIDE
CMP_HIDDEN = 128
SEL_BLOCK = 64
TOP_N = 8
WINDOW = 256
N_GROUPS = 4
EXPERTS_PER_GROUP = 8
N_EXPERTS = N_GROUPS * EXPERTS_PER_GROUP
TOP_K_IN_GROUP = 2
D_EXPERT = 256
RMS_EPS = 1e-6
NEG_INF = -1e30
ATTN_SCALE = HEAD_DIM ** -0.5

LANES = 128
VMEM_LIMIT = 48 * 1024 * 1024
ROW_TILE = 512
NSA_Q_TILE = 256
NSA_KV_CHUNK = 512
MOE_BLOCK = 256
ROUTER_ROWS = 40
SLAB = D_MODEL // LANES
PLAN_SLOTS = 8
ROW_BUFS = 4


def _params(*sem):
    return pltpu.CompilerParams(dimension_semantics=sem, vmem_limit_bytes=VMEM_LIMIT)


def _rms(x, g):
    return x * lax.rsqrt(jnp.mean(x * x, axis=-1, keepdims=True) + RMS_EPS) * g


def _group_rms(x, bd, gain):
    sq = x * x
    hi = sq.astype(BF16)
    lo = (sq - hi.astype(F32)).astype(BF16)
    ms = jnp.dot(hi, bd, preferred_element_type=F32) + jnp.dot(lo, bd, preferred_element_type=F32)
    return x * lax.rsqrt(ms + RMS_EPS) * gain


def _sigmoid(x):
    return 1.0 / (1.0 + jnp.exp(-x))


def _block_diag_mean(n):
    idx = np.arange(n) // HEAD_DIM
    return jnp.asarray((idx[:, None] == idx[None, :]).astype(np.float32) / HEAD_DIM, BF16)


def _even_inproj_kernel(x_ref, g_ref, wret_ref, wkt_ref, wnq_ref, wkv_ref, wng_ref, qgain_ref, kgain_ref, bd512_ref,
                        bd128_ref, ret_ref, kt_ref, nq_ref, chunk_ref, ks_ref, vs_ref, kw_ref, vw_ref, gate_ref, kcv_ref, *, seq_tiles):
    tm = x_ref.shape[0]
    xn = _rms(x_ref[...], g_ref[...]).astype(BF16)
    ret_ref[...] = jnp.dot(xn, wret_ref[...], preferred_element_type=F32).astype(BF16)
    kt_ref[...] = lax.dot_general(wkt_ref[...], xn, (((1,), (1,)), ((), ())), preferred_element_type=F32).astype(BF16)
    nq = jnp.dot(xn, wnq_ref[...], preferred_element_type=F32)
    nq_ref[...] = _group_rms(nq, bd512_ref[...], qgain_ref[...]).astype(BF16)
    kv = jnp.dot(xn, wkv_ref[...], preferred_element_type=F32)
    w = NSA_KV_W
    n_chunk = tm // CMP_STRIDE
    for kv_i in range(2):
        kcv_ref[kv_i] = kv[:, kv_i * w:(kv_i + 1) * w]
        for j in range(CMP_STRIDE // 2):
            even = kcv_ref[kv_i, pl.ds(2 * j, n_chunk, stride=CMP_STRIDE), :]
            odd = kcv_ref[kv_i, pl.ds(2 * j + 1, n_chunk, stride=CMP_STRIDE), :]
            for g in range(NSA_KV_GROUPS):
                sl = slice(g * HEAD_DIM, (g + 1) * HEAD_DIM)
                pair = jnp.concatenate([even[:, sl], odd[:, sl]], axis=-1).astype(BF16)
                chunk_ref[kv_i, g, :, j * LANES:(j + 1) * LANES] = pair
    ks = _group_rms(kv[:, 2 * w:3 * w], bd128_ref[...], kgain_ref[0:1, :])
    kw = _group_rms(kv[:, 4 * w:5 * w], bd128_ref[...], kgain_ref[1:2, :])
    vs, vw = kv[:, 3 * w:4 * w], kv[:, 5 * w:6 * w]
    pos = (pl.program_id(0) % seq_tiles) * tm + lax.broadcasted_iota(I32, (tm, HEAD_DIM), 0)
    lane = lax.broadcasted_iota(I32, (tm, HEAD_DIM), 1)
    blk = lax.shift_right_arithmetic(pos, int(np.log2(SEL_BLOCK)))
    feat_w = jnp.where(lane == 0, blk, jnp.where(lane == 1, pos & (SEL_BLOCK - 1), 0)).astype(F32)
    feat_s = feat_w + jnp.where(lane == blk + 2, 1.0, 0.0)
    ones_col = jnp.where(lane == 0, 1.0, 0.0)
    for g in range(NSA_KV_GROUPS):
        sl = slice(g * HEAD_DIM, (g + 1) * HEAD_DIM)
        ks_ref[g] = jnp.concatenate([ks[:, sl], feat_s], axis=-1).astype(BF16)
        kw_ref[g] = jnp.concatenate([kw[:, sl], feat_w], axis=-1).astype(BF16)
        vs_ref[g] = jnp.concatenate([vs[:, sl], ones_col], axis=-1).astype(BF16)
        vw_ref[g] = jnp.concatenate([vw[:, sl], ones_col], axis=-1).astype(BF16)
    ng = jnp.dot(xn, wng_ref[...], preferred_element_type=F32)
    for g in range(NSA_KV_GROUPS):
        gate_ref[g] = ng[:, g * LANES:(g + 1) * LANES]


def _even_inproj(h, g, w_in, q_norm_g, k_norm_g, s):
    t = h.shape[0]
    tm = ROW_TILE
    c_ret, c_nq, c_kv = 4 * RET_W, 4 * RET_W + NSA_W, 4 * RET_W + NSA_W + 6 * NSA_KV_W
    n_gate = NSA_GROUP_SIZE * N_BRANCHES
    flat = CMP_STRIDE * HEAD_DIM
    wb = w_in.astype(BF16)
    wret = jnp.concatenate([wb[:, :RET_W], wb[:, 2 * RET_W:c_ret]], axis=1)
    wkt = wb[:, RET_W:2 * RET_W].T
    wnq, wkv = wb[:, c_ret:c_nq], wb[:, c_nq:c_kv]
    wng = jnp.concatenate([jnp.pad(wb[:, c_kv + gi * n_gate:c_kv + (gi + 1) * n_gate], ((0, 0), (0, LANES - n_gate)))
                           for gi in range(NSA_KV_GROUPS)], axis=1)
    qgain = jnp.tile(q_norm_g, NSA_HEADS)[None, :]
    kgain = jnp.stack([jnp.tile(k_norm_g[1], NSA_KV_GROUPS), jnp.tile(k_norm_g[2], NSA_KV_GROUPS)])
    full = lambda a: pl.BlockSpec(a.shape, lambda i: (0,) * a.ndim)
    row = lambda n: pl.BlockSpec((tm, n), lambda i: (i, 0))
    grp = lambda n: pl.BlockSpec((NSA_KV_GROUPS, tm, n), lambda i: (0, i, 0))
    grp_shape = lambda n, dt: jax.ShapeDtypeStruct((NSA_KV_GROUPS, t, n), dt)
    bd512, bd128 = _block_diag_mean(NSA_W), _block_diag_mean(NSA_KV_W)
    gg = g[None, :]
    return pl.pallas_call(
        functools.partial(_even_inproj_kernel, seq_tiles=s // tm),
        name="even_inproj",
        grid=(t // tm,),
        in_specs=[row(D_MODEL), full(gg), full(wret), full(wkt), full(wnq), full(wkv), full(wng), full(qgain),
                  full(kgain), full(bd512), full(bd128)],
        out_specs=[row(3 * RET_W), pl.BlockSpec((RET_W, tm), lambda i: (0, i)), row(NSA_W),
                   pl.BlockSpec((2, NSA_KV_GROUPS, tm // CMP_STRIDE, flat), lambda i: (0, 0, i, 0)),
                   grp(LANES), grp(LANES), grp(LANES), grp(LANES), grp(LANES)],
        out_shape=[jax.ShapeDtypeStruct((t, 3 * RET_W), BF16), jax.ShapeDtypeStruct((RET_W, t), BF16),
                   jax.ShapeDtypeStruct((t, NSA_W), BF16), jax.ShapeDtypeStruct((2, NSA_KV_GROUPS, t // CMP_STRIDE, flat), BF16), grp_shape(LANES, BF16),
                   grp_shape(LANES, BF16), grp_shape(LANES, BF16), grp_shape(LANES, BF16), grp_shape(LANES, F32)],
        scratch_shapes=[pltpu.VMEM((2, tm, NSA_KV_W), F32)],
        compiler_params=_params("arbitrary"),
    )(h, gg, wret, wkt, wnq, wkv, wng, qgain, kgain, bd512, bd128)


def _retention_kernel(q_ref, kt_ref, v_ref, rg_ref, decay_ref, qd_ref, kd_ref, cd_ref, diag_ref, bd_ref, gain_ref, o_ref):
    n_chunks = q_ref.shape[0] // RET_CHUNK
    gain, diag, bd = gain_ref[...], diag_ref[...], bd_ref[...]
    qd, kd, cd = qd_ref[0], kd_ref[0], cd_ref[0]
    first = lax.broadcasted_iota(I32, (RET_CHUNK, LANES), 1) < HEAD_DIM
    zero = jnp.zeros((RET_CHUNK, LANES), BF16)

    def body(c, state):
        r0 = pl.multiple_of(c * RET_CHUNK, RET_CHUNK)
        rows = pl.ds(r0, RET_CHUNK)
        q2, v2 = q_ref[rows, :], v_ref[rows, :]
        kt = kt_ref[:, rows].astype(F32) * ATTN_SCALE
        kt_b = kt.astype(BF16)
        s0 = jnp.dot(jnp.where(first, q2, zero), kt_b, preferred_element_type=F32) * decay_ref[0]
        s1 = jnp.dot(jnp.where(first, zero, q2), kt_b, preferred_element_type=F32) * decay_ref[1]
        y = jnp.where(first, jnp.dot(s0.astype(BF16), v2, preferred_element_type=F32),
                      jnp.dot(s1.astype(BF16), v2, preferred_element_type=F32))
        y = y + jnp.dot((q2.astype(F32) * qd).astype(BF16), state.astype(BF16), preferred_element_type=F32)
        kv = jnp.dot((kt * kd).astype(BF16), v2, preferred_element_type=F32)
        rg = rg_ref[rows, :].astype(F32)
        o_ref[rows, :] = (_group_rms(y, bd, gain) * (rg * _sigmoid(rg))).astype(BF16)
        return cd * state + diag * kv

    lax.fori_loop(0, n_chunks, body, jnp.zeros((LANES, LANES), F32), unroll=True)


def _retention_tables():
    h = np.arange(RET_HEADS, dtype=np.float64)
    log_g = np.log(1.0 - 2.0 ** (-5.0 - h))
    pos = np.arange(RET_CHUNK, dtype=np.float64)
    diff = pos[:, None] - pos[None, :]
    decay = np.where(diff >= 0, np.exp(log_g[:, None, None] * np.maximum(diff, 0.0)), 0.0)
    qd = np.exp(log_g[:, None] * (pos + 1.0))
    kd = np.exp(log_g[:, None] * (RET_CHUNK - 1 - pos))
    cd = np.exp(log_g * RET_CHUNK)
    pairs = RET_HEADS // 2
    rep = lambda a: np.repeat(a, HEAD_DIM, axis=1)
    qd2 = rep(qd.reshape(pairs, 2, RET_CHUNK)).transpose(0, 2, 1)
    kd2 = rep(kd.reshape(pairs, 2, RET_CHUNK))
    cd2 = rep(cd.reshape(pairs, 2, 1)) * np.ones((1, 1, LANES))
    half = np.arange(LANES) // HEAD_DIM
    diag = (half[:, None] == half[None, :]).astype(np.float64)
    return [jnp.asarray(a, F32) for a in (decay.reshape(pairs, 2, RET_CHUNK, RET_CHUNK), qd2, kd2, cd2, diag)]


def _retention(ret, ret_kt, ret_norm_g, b, s):
    t = b * s
    n_pairs = RET_HEADS // 2
    decay, qd, kd, cd, diag = _retention_tables()
    bd = _block_diag_mean(LANES)
    col = lambda off: pl.BlockSpec((s, LANES), lambda bi, p: (bi, off * n_pairs + p))
    tab = lambda a: pl.BlockSpec((1,) + a.shape[1:], lambda bi, p: (p,) + (0,) * (a.ndim - 1))
    full = lambda a: pl.BlockSpec(a.shape, lambda bi, p: (0, 0))
    gain = jnp.tile(ret_norm_g, 2)[None, :]
    return pl.pallas_call(
        _retention_kernel,
        name="retention",
        grid=(b, n_pairs),
        in_specs=[col(0), pl.BlockSpec((LANES, s), lambda bi, p: (p, bi)), col(1), col(2),
                  pl.BlockSpec((None, 2, RET_CHUNK, RET_CHUNK), lambda bi, p: (p, 0, 0, 0)), tab(qd), tab(kd), tab(cd),
                  full(diag), full(bd), full(gain)],
        out_specs=pl.BlockSpec((s, LANES), lambda bi, p: (bi, p)),
        out_shape=jax.ShapeDtypeStruct((t, RET_W), BF16),
        compiler_params=_params("arbitrary", "arbitrary"),
    )(ret, ret_kt, ret, ret, decay, qd, kd, cd, diag, bd, gain)


def _compress_kernel(x_ref, pe_ref, w1_ref, w2_ref, kgain_ref, o_ref):
    is_key = pl.program_id(0) == 0
    half = CMP_STRIDE * HEAD_DIM
    x = x_ref[...]
    n = x.shape[0]
    a = jnp.dot(x, w1_ref[0:half, :], preferred_element_type=F32)
    bm = jnp.dot(x, w1_ref[half:2 * half, :], preferred_element_type=F32)
    pew = jnp.dot(pe_ref[...], w1_ref[...], preferred_element_type=F32)[0:1, :]
    pre = a + pltpu.roll(bm, n - 1, axis=0) + pew
    hid = 0.5 * pre * (1.0 + jnp.tanh(np.sqrt(2.0 / np.pi) * (pre + 0.044715 * pre * pre * pre)))
    out = jnp.dot(hid.astype(BF16), w2_ref[...], preferred_element_type=F32)
    normed = _rms(out, kgain_ref[...])
    o_ref[...] = jnp.where(is_key, normed, out).astype(BF16)


def _compress(chunks, cmp_pe, w_cmp1, w_cmp2, k_norm_g, b, s):
    n = s // CMP_STRIDE
    g = NSA_KV_GROUPS
    flat = CMP_STRIDE * HEAD_DIM
    pe = jnp.broadcast_to(cmp_pe.reshape(2, 1, 2 * flat), (2, 8, 2 * flat)).astype(BF16)
    w1, w2 = w_cmp1.astype(BF16), w_cmp2.astype(BF16)
    kgain = k_norm_g[0][None, :]
    return pl.pallas_call(
        _compress_kernel,
        name="nsa_compress",
        grid=(2, b, g),
        in_specs=[pl.BlockSpec((None, None, n, flat), lambda kv, bi, gi: (kv, gi, bi, 0)),
                  pl.BlockSpec((None, 8, 2 * flat), lambda kv, bi, gi: (kv, 0, 0)),
                  pl.BlockSpec((None, 2 * flat, CMP_HIDDEN), lambda kv, bi, gi: (kv, 0, 0)),
                  pl.BlockSpec((None, CMP_HIDDEN, HEAD_DIM), lambda kv, bi, gi: (kv, 0, 0)),
                  pl.BlockSpec(kgain.shape, lambda kv, bi, gi: (0, 0))],
        out_specs=pl.BlockSpec((None, None, None, n, HEAD_DIM), lambda kv, bi, gi: (kv, bi, gi, 0, 0)),
        out_shape=jax.ShapeDtypeStruct((2, b, g, n, HEAD_DIM), BF16),
        compiler_params=_params("arbitrary", "arbitrary", "arbitrary"),
    )(chunks, pe, w1, w2, kgain)


def _nsa_kernel(slopes_ref, q_ref, kcmp_ref, vcmp_ref, ks_ref, vs_ref, kw_ref, vw_ref, gate_ref, ovl_ref, place_ref,
                o_ref, *, n_sel, n_cmp):
    r_heads = NSA_GROUP_SIZE
    tq = q_ref.shape[0]
    gi = pl.program_id(1)
    t0 = pl.program_id(2) * tq
    nt = (((1,), (1,)), ((), ()))
    tn = (((0,), (0,)), ((), ()))

    q = q_ref[...]
    lane = lax.broadcasted_iota(I32, (tq, HEAD_DIM), 1)
    qs, feats = [], []
    for r in range(r_heads):
        slope = slopes_ref[gi * r_heads + r]
        qs.append((q[:, r * HEAD_DIM:(r + 1) * HEAD_DIM].astype(F32) * ATTN_SCALE).astype(BF16))
        feats.append(jnp.where(lane == 0, slope * SEL_BLOCK, jnp.where(lane == 1, slope, 0.0)))

    def stack_q(extra):
        return jnp.concatenate([jnp.concatenate([qs[r], (feats[r] + extra).astype(BF16)], axis=-1)
                                for r in range(r_heads)], axis=0)

    q4 = stack_q(0.0)
    t_col = t0 + lax.broadcasted_iota(I32, (tq, 1), 0)

    def head_rows(x):
        return [x[r * tq:(r + 1) * tq] for r in range(r_heads)]

    n_pad = kcmp_ref.shape[0]
    c_idx = lax.broadcasted_iota(I32, (tq, n_pad), 1)
    mask_c = (t_col >= c_idx * CMP_STRIDE + (CMP_BLOCK - 1)) & (c_idx < n_cmp)
    s4 = lax.dot_general(q4, kcmp_ref[...], nt, preferred_element_type=F32)
    ps = []
    for s in head_rows(s4):
        s = jnp.where(mask_c, s, NEG_INF)
        p = jnp.where(mask_c, jnp.exp(s - jnp.max(s, axis=-1, keepdims=True)), 0.0)
        ps.append(p * (1.0 / jnp.maximum(jnp.sum(p, axis=-1, keepdims=True), 1e-30)))
    p4 = jnp.concatenate(ps, axis=0).astype(BF16)
    o_cmp = head_rows(jnp.dot(p4, vcmp_ref[...], preferred_element_type=F32))

    n_rows = ovl_ref.shape[0]
    imp4 = lax.dot_general(ovl_ref[...], p4, nt, preferred_element_type=F32)
    imp = imp4[:, 0:tq]
    for r in range(1, r_heads):
        imp = imp + imp4[:, r * tq:(r + 1) * tq]
    j_idx = lax.broadcasted_iota(I32, (n_rows, tq), 0)
    t_row = t0 + lax.broadcasted_iota(I32, (n_rows, tq), 1)
    q_blk = lax.shift_right_arithmetic(t_row, int(np.log2(SEL_BLOCK)))
    valid = (j_idx * SEL_BLOCK <= t_row) & (j_idx < n_sel)
    forced = (j_idx == 0) | (j_idx == q_blk) | (j_idx == q_blk - 1)
    val = jnp.where(j_idx < n_sel, jnp.where(forced, imp + 1e3, jnp.where(valid, imp, -1e3)), -2e3)
    rank = jnp.zeros((n_rows, tq), F32)
    for k in range(n_sel):
        vk = val[k:k + 1, :]
        rank = rank + jnp.where(vk > val, 1.0, jnp.where(vk == val, jnp.where(j_idx > k, 1.0, 0.0), 0.0))
    unselected = jnp.where(valid, jnp.where(rank < TOP_N, 0.0, NEG_INF), NEG_INF).astype(BF16)
    q4_sel = stack_q(lax.dot_general(unselected, place_ref[...], tn, preferred_element_type=F32))

    ck = NSA_KV_CHUNK

    def sel_step(c, carry, causal):
        c0 = pl.multiple_of(c * ck, ck)
        s4 = lax.dot_general(q4_sel, ks_ref[pl.ds(c0, ck), :], nt, preferred_element_type=F32)
        if causal:
            visible = t_col >= c0 + lax.broadcasted_iota(I32, (tq, ck), 1)
        ps, new = [], []
        for r, s in enumerate(head_rows(s4)):
            if causal:
                s = jnp.where(visible, s, NEG_INF)
            m_old, acc_old = carry[r]
            m_new = jnp.maximum(m_old, jnp.max(s, axis=-1, keepdims=True))
            ps.append(jnp.exp(s - m_new))
            new.append((m_new, jnp.exp(m_old - m_new) * acc_old))
        pv = jnp.dot(jnp.concatenate(ps, axis=0).astype(BF16), vs_ref[pl.ds(c0, ck), :], preferred_element_type=F32)
        return tuple((m, acc + o) for (m, acc), o in zip(new, head_rows(pv)))

    def normalise(acc):
        return acc[:, 0:HEAD_DIM] * (1.0 / acc[:, HEAD_DIM:HEAD_DIM + 1])

    init = tuple((jnp.full((tq, 1), NEG_INF, F32), jnp.zeros((tq, LANES), F32)) for _ in range(r_heads))
    c_last = t0 // ck
    carry = lax.fori_loop(0, c_last, lambda c, carry: sel_step(c, carry, False), init)
    o_sel = [normalise(acc) for (_, acc) in sel_step(c_last, carry, True)]

    n_win = WINDOW + tq
    w0 = pl.multiple_of(jnp.maximum(t0 - WINDOW, 0), tq)
    dist_w = t_col - (w0 + lax.broadcasted_iota(I32, (tq, n_win), 1))
    mask_w = (dist_w >= 0) & (dist_w < WINDOW)
    s4 = lax.dot_general(q4, kw_ref[pl.ds(w0, n_win), :], nt, preferred_element_type=F32)
    ps = []
    for s in head_rows(s4):
        s = jnp.where(mask_w, s, NEG_INF)
        ps.append(jnp.exp(s - jnp.max(s, axis=-1, keepdims=True)))
    pv = jnp.dot(jnp.concatenate(ps, axis=0).astype(BF16), vw_ref[pl.ds(w0, n_win), :], preferred_element_type=F32)
    o_win = [normalise(o) for o in head_rows(pv)]

    gate = _sigmoid(gate_ref[...])
    outs = []
    for r in range(r_heads):
        c = N_BRANCHES * r
        outs.append(gate[:, c:c + 1] * o_cmp[r] + gate[:, c + 1:c + 2] * o_sel[r] + gate[:, c + 2:c + 3] * o_win[r])
    o_ref[...] = jnp.concatenate(outs, axis=-1).astype(BF16)


def _nsa_attention(nq, cmp_kv, ks_aug, vs, kw_aug, vw, gates, b, s):
    t = b * s
    g, r = NSA_KV_GROUPS, NSA_GROUP_SIZE
    tq = NSA_Q_TILE
    n_q = s // tq
    n_sel = s // SEL_BLOCK
    n_pad = s // CMP_STRIDE
    n_cmp = n_pad - 1
    n_rows = -(-n_sel // 8) * 8
    assert 2 + n_sel <= HEAD_DIM and s % NSA_KV_CHUNK == 0 and s >= WINDOW + tq and NSA_KV_CHUNK % tq == 0
    slopes = jnp.asarray(2.0 ** (-8.0 * np.arange(1, NSA_HEADS + 1) / NSA_HEADS), F32)
    cmp_start = np.arange(n_pad) * CMP_STRIDE
    cmp_end = cmp_start + CMP_BLOCK - 1
    feat = np.zeros((n_pad, HEAD_DIM), np.float32)
    feat[:, 0], feat[:, 1] = cmp_end // SEL_BLOCK, cmp_end % SEL_BLOCK
    kcmp_aug = jnp.concatenate([cmp_kv[0], jnp.broadcast_to(jnp.asarray(feat, BF16), cmp_kv[0].shape)], axis=-1)
    sel_start = np.arange(n_rows) * SEL_BLOCK
    ovl = ((cmp_start[None, :] < sel_start[:, None] + SEL_BLOCK) & (cmp_start[None, :] + CMP_BLOCK > sel_start[:, None])
           & (np.arange(n_pad)[None, :] < n_cmp) & (np.arange(n_rows)[:, None] < n_sel))
    ovl = jnp.asarray(ovl.astype(np.float32), BF16)
    place = jnp.asarray((np.arange(n_rows)[:, None] + 2 == np.arange(HEAD_DIM)[None, :]).astype(np.float32), BF16)
    seq = lambda w: pl.BlockSpec((None, s, w), lambda bi, gi, qi: (gi, bi, 0))
    cmp = lambda w: pl.BlockSpec((None, None, n_pad, w), lambda bi, gi, qi: (bi, gi, 0, 0))
    return pl.pallas_call(
        functools.partial(_nsa_kernel, n_sel=n_sel, n_cmp=n_cmp),
        name="nsa_attention",
        grid=(b, g, n_q),
        in_specs=[pl.BlockSpec(memory_space=pltpu.SMEM),
                  pl.BlockSpec((tq, r * HEAD_DIM), lambda bi, gi, qi: (bi * n_q + qi, gi)),
                  cmp(2 * HEAD_DIM), cmp(HEAD_DIM), seq(LANES), seq(LANES), seq(LANES), seq(LANES),
                  pl.BlockSpec((None, tq, LANES), lambda bi, gi, qi: (gi, bi * n_q + qi, 0)),
                  pl.BlockSpec(ovl.shape, lambda bi, gi, qi: (0, 0)),
                  pl.BlockSpec(place.shape, lambda bi, gi, qi: (0, 0))],
        out_specs=pl.BlockSpec((tq, r * HEAD_DIM), lambda bi, gi, qi: (bi * n_q + qi, gi)),
        out_shape=jax.ShapeDtypeStruct((t, NSA_W), BF16),
        compiler_params=_params("arbitrary", "arbitrary", "arbitrary"),
    )(slopes, nq, kcmp_aug, cmp_kv[1], ks_aug, vs, kw_aug, vw, gates, ovl, place)


def _even_outproj_kernel(h_ref, a_ref, b_ref, wa_ref, wb_ref, g_ref, wr_ref, br_ref, o_ref, xn_ref, ids_ref, gates_ref):
    y = jnp.dot(a_ref[...], wa_ref[...], preferred_element_type=F32)
    y = y + jnp.dot(b_ref[...], wb_ref[...], preferred_element_type=F32)
    h = h_ref[...] + y
    o_ref[...] = h
    _route(h, g_ref, wr_ref, br_ref, xn_ref, ids_ref, gates_ref)


def _even_outproj(h, y_ret, y_nsa, w_out, router):
    t = h.shape[0]
    tm = ROW_TILE
    wb = w_out.astype(BF16)
    wa, wbt = wb[:RET_W], wb[RET_W:]
    gg, wr, br = _router_operands(*router, tm)
    row = lambda n: pl.BlockSpec((tm, n), lambda i: (i, 0))
    full = lambda a: pl.BlockSpec(a.shape, lambda i: (0, 0))
    r_specs, r_shapes = _router_outputs(t, tm, lambda i: i)
    return pl.pallas_call(
        _even_outproj_kernel,
        name="even_outproj",
        grid=(t // tm,),
        in_specs=[row(D_MODEL), row(RET_W), row(NSA_W), full(wa), full(wbt), full(gg), full(wr), full(br)],
        out_specs=[row(D_MODEL)] + r_specs,
        out_shape=[jax.ShapeDtypeStruct((t, D_MODEL), F32)] + r_shapes,
        compiler_params=_params("arbitrary"),
    )(h, y_ret, y_nsa, wa, wbt, gg, wr, br)


def _odd_mixer_kernel(h_ref, y_ref, mg_ref, g_ref, win_ref, cw_ref, cb_ref, wout_ref, rg_ref, wr_ref, br_ref,
                      o_ref, xn_ref, ids_ref, gates_ref, tail_ref):
    @pl.when(pl.program_id(1) == 0)
    def _():
        tail_ref[...] = jnp.zeros_like(tail_ref)

    ts = h_ref.shape[0]
    h = _moe_sum(h_ref, y_ref, mg_ref)
    xn = _rms(h, g_ref[...]).astype(BF16)
    proj = jnp.dot(xn, win_ref[...], preferred_element_type=F32)
    gate_b, gate_c, hid = proj[:, 0:D_MODEL], proj[:, D_MODEL:2 * D_MODEL], proj[:, 2 * D_MODEL:3 * D_MODEL]
    u = gate_c * hid
    row = lax.broadcasted_iota(I32, (ts, D_MODEL), 0)
    tail = tail_ref[...]
    prev1, prev2 = tail[7:8, :], tail[6:7, :]
    u1 = jnp.where(row >= 1, pltpu.roll(u, 1, axis=0), prev1)
    u2 = jnp.where(row >= 2, pltpu.roll(u, 2, axis=0), jnp.where(row == 1, prev1, prev2))
    cw = cw_ref[...]
    y = cw[0:1, :] * u2 + cw[1:2, :] * u1 + cw[2:3, :] * u + cb_ref[...]
    tail_ref[...] = u[ts - 8:ts, :]
    z = (gate_b * y).astype(BF16)
    h_new = h + jnp.dot(z, wout_ref[...], preferred_element_type=F32)
    o_ref[...] = h_new
    _route(h_new, rg_ref, wr_ref, br_ref, xn_ref, ids_ref, gates_ref)


def _odd_mixer(h, y_slabs, moe_gates, g, w_in, conv_w, conv_b, w_out, router, b, s):
    t = b * s
    ts = ROW_TILE
    n_s = s // ts
    win, wout = w_in.astype(BF16), w_out.astype(BF16)
    gg, cb = g[None, :], conv_b[None, :]
    cw = jnp.pad(conv_w, ((0, 8 - conv_w.shape[0]), (0, 0)))
    rg, wr, br = _router_operands(*router, ts)
    full = lambda a: pl.BlockSpec(a.shape, lambda bi, si: (0, 0), pipeline_mode=pl.Buffered(1))
    row = pl.BlockSpec((ts, D_MODEL), lambda bi, si: (bi * n_s + si, 0))
    y_spec = pl.BlockSpec((ts * TOP_K_IN_GROUP * SLAB, LANES), lambda bi, si: (bi * n_s + si, 0))
    mg_spec = pl.BlockSpec((ts, TOP_K_IN_GROUP), lambda bi, si: (bi * n_s + si, 0))
    r_specs, r_shapes = _router_outputs(t, ts, lambda bi, si: bi * n_s + si)
    return pl.pallas_call(
        _odd_mixer_kernel,
        name="odd_mixer",
        grid=(b, n_s),
        in_specs=[row, y_spec, mg_spec, full(gg), full(win), full(cw), full(cb), full(wout), full(rg), full(wr), full(br)],
        out_specs=[row] + r_specs,
        out_shape=[jax.ShapeDtypeStruct((t, D_MODEL), F32)] + r_shapes,
        scratch_shapes=[pltpu.VMEM((8, D_MODEL), F32)],
        compiler_params=_params("arbitrary", "arbitrary"),
    )(h, y_slabs, moe_gates.T, gg, win, cw, cb, wout, rg, wr, br)


def _store_slabs(ref, x, stride):
    rows = x.shape[0]
    for j in range(SLAB):
        ref[pl.ds(j, rows, stride=stride), :] = x[:, j * LANES:(j + 1) * LANES]


def _load_slabs(ref, rows, stride, offset=0):
    return jnp.concatenate([ref[pl.ds(offset + j, rows, stride=stride), :] for j in range(SLAB)], axis=-1)


def _route(h, g_ref, wr_ref, br_ref, xn_ref, ids_ref, gates_ref):
    xn = _rms(h, g_ref[...])
    _store_slabs(xn_ref, xn, SLAB)
    nt = (((1,), (1,)), ((), ()))
    x_hi = xn.astype(BF16)
    x_lo = (xn - x_hi.astype(F32)).astype(BF16)
    w = wr_ref[...]
    w_hi = w.astype(BF16)
    w_lo = (w - w_hi.astype(F32)).astype(BF16)
    logits = (lax.dot_general(w_hi, x_hi, nt, preferred_element_type=F32)
              + lax.dot_general(w_hi, x_lo, nt, preferred_element_type=F32)
              + lax.dot_general(w_lo, x_hi, nt, preferred_element_type=F32)) + br_ref[...]
    row = lambda i: logits[i:i + 1, :]

    def softmax(xs):
        m = functools.reduce(jnp.maximum, xs)
        es = [jnp.exp(x - m) for x in xs]
        tot = functools.reduce(lambda a, c: a + c, es)
        return [e / tot for e in es]

    def argmax(ps):
        best_p, best_i = ps[0], jnp.zeros_like(ps[0], dtype=I32)
        for i in range(1, len(ps)):
            upd = ps[i] > best_p
            best_p = jnp.where(upd, ps[i], best_p)
            best_i = jnp.where(upd, i, best_i)
        return best_p, best_i

    grp_p, grp_i = argmax(softmax([row(i) for i in range(N_GROUPS)]))
    in_grp = []
    for e in range(EXPERTS_PER_GROUP):
        x = row(N_GROUPS + e)
        for gidx in range(1, N_GROUPS):
            x = jnp.where(grp_i == gidx, row(N_GROUPS + gidx * EXPERTS_PER_GROUP + e), x)
        in_grp.append(x)
    pe = softmax(in_grp)
    p1, i1 = argmax(pe)
    p2, i2 = argmax([jnp.where(i1 == e, -1.0, pe[e]) for e in range(EXPERTS_PER_GROUP)])
    tot = p1 + p2
    ids_ref[...] = jnp.concatenate([grp_i * EXPERTS_PER_GROUP + i1, grp_i * EXPERTS_PER_GROUP + i2], axis=0)
    gates_ref[...] = jnp.concatenate([grp_p * p1 / tot, grp_p * p2 / tot], axis=0)


def _router_operands(g, wg_group, bg_group, wg_expert, bg_expert, tm):
    n_logit = N_GROUPS + N_EXPERTS
    wr = jnp.pad(jnp.concatenate([wg_group, wg_expert], axis=1).T, ((0, ROUTER_ROWS - n_logit), (0, 0)))
    br = jnp.pad(jnp.concatenate([bg_group, bg_expert]), (0, ROUTER_ROWS - n_logit))
    return g[None, :], wr, jnp.broadcast_to(br[:, None], (ROUTER_ROWS, tm))


def _router_outputs(t, tm, row_index):
    specs = [pl.BlockSpec((tm * SLAB, LANES), lambda *i: (row_index(*i), 0)),
             pl.BlockSpec((TOP_K_IN_GROUP, tm), lambda *i: (0, row_index(*i))),
             pl.BlockSpec((TOP_K_IN_GROUP, tm), lambda *i: (0, row_index(*i)))]
    shapes = [jax.ShapeDtypeStruct((t * SLAB, LANES), F32), jax.ShapeDtypeStruct((TOP_K_IN_GROUP, t), I32),
              jax.ShapeDtypeStruct((TOP_K_IN_GROUP, t), F32)]
    return specs, shapes


def _expert_kernel(blk_expert_ref, n_used_ref, plan_hbm, x_hbm, wg_ref, wu_ref, wd_ref, y_hbm,
                   plan, xbuf, ybuf, sem_plan, sem_in, sem_out):
    del blk_expert_ref
    depth = ROW_BUFS
    i = pl.program_id(0)
    n_used = n_used_ref[0]
    cur, prev, ahead = i % depth, (i + depth - 1) % depth, (i + depth - 1) % depth

    def plan_copy(k):
        return pltpu.make_async_copy(plan_hbm.at[k], plan.at[k % PLAN_SLOTS], sem_plan.at[k % PLAN_SLOTS])

    def row_copy(hbm, hbm_row, buf, r, sem, to_hbm):
        h = hbm.at[pl.ds(pl.multiple_of(hbm_row * SLAB, SLAB), SLAB)]
        v = buf.at[pl.ds(r * SLAB, SLAB)]
        return pltpu.make_async_copy(v, h, sem) if to_hbm else pltpu.make_async_copy(h, v, sem)

    def wait_gather(s):
        for r in range(MOE_BLOCK):
            row_copy(x_hbm, 0, xbuf.at[s], r, sem_in.at[s], False).wait()

    def wait_scatter(s):
        for r in range(MOE_BLOCK):
            row_copy(y_hbm, 0, ybuf.at[s], r, sem_out.at[s], True).wait()

    @pl.when(i == 0)
    def _():
        ybuf[...] = jnp.zeros_like(ybuf)
        n_slot = y_hbm.shape[0] // SLAB - depth * MOE_BLOCK
        for s in range(depth - 1):
            for r in range(MOE_BLOCK):
                row_copy(y_hbm, n_slot + s * MOE_BLOCK + r, ybuf.at[s], r, sem_out.at[s], True).start()
        for s in range(depth - 1):
            wait_scatter(s)
        for k in range(depth + 1):
            plan_copy(k).start()
        for k in range(depth + 1):
            plan_copy(k).wait()
        for blk in range(depth - 1):
            for r in range(MOE_BLOCK):
                row_copy(x_hbm, plan[blk + 1, 0, r], xbuf.at[blk], r, sem_in.at[blk], False).start()

    @pl.when(i <= n_used)
    def _():
        @pl.when(i >= 1)
        def _():
            plan_copy(i + depth).wait()

        plan_copy(i + depth + 1).start()
        wait_gather(cur)

        @pl.when(i >= depth - 1)
        def _():
            wait_scatter(cur)

        nxt, prv = (i + depth) % PLAN_SLOTS, i % PLAN_SLOTS
        for r in range(MOE_BLOCK):
            row_copy(x_hbm, plan[nxt, 0, r], xbuf.at[ahead], r, sem_in.at[ahead], False).start(priority=r % 2)
            row_copy(y_hbm, plan[prv, 0, MOE_BLOCK + r], ybuf.at[prev], r, sem_out.at[prev], True).start(priority=(r + 1) % 2)
        x = _load_slabs(xbuf.at[cur], MOE_BLOCK, SLAB).astype(BF16)
        a = jnp.dot(x, wg_ref[...].astype(BF16), preferred_element_type=F32)
        u = jnp.dot(x, wu_ref[...].astype(BF16), preferred_element_type=F32)
        hid = (a * _sigmoid(a) * u).astype(BF16)
        _store_slabs(ybuf.at[cur], jnp.dot(hid, wd_ref[...].astype(BF16), preferred_element_type=F32), SLAB)

        @pl.when(i == n_used)
        def _():
            plan_copy(i + depth + 1).wait()
            for k in range(1, depth):
                wait_gather((i + k) % depth)

                @pl.when(i >= k - 1)
                def _():
                    wait_scatter((i + depth - k) % depth)


def _moe_plan(ids, t):
    n_slot = t * TOP_K_IN_GROUP
    n_blocks = n_slot // MOE_BLOCK + N_EXPERTS
    flat_e = ids.T.reshape(-1)
    order = jnp.argsort(flat_e).astype(I32)
    experts = jnp.arange(N_EXPERTS, dtype=I32)
    counts = jnp.sum(flat_e[:, None] == experts[None, :], axis=0, dtype=I32)
    n_blk = (counts + MOE_BLOCK - 1) // MOE_BLOCK
    blk_end = jnp.cumsum(n_blk)
    start = jnp.cumsum(counts) - counts
    blocks = jnp.arange(n_blocks, dtype=I32)
    blk_expert = jnp.minimum(jnp.sum(blk_end[None, :] <= blocks[:, None], axis=1, dtype=I32), N_EXPERTS - 1)
    row0 = (blocks - (blk_end - n_blk)[blk_expert]) * MOE_BLOCK
    rows = row0[:, None] + jnp.arange(MOE_BLOCK, dtype=I32)[None, :]
    is_slot = (rows >= 0) & (rows < counts[blk_expert][:, None])
    slot = order[jnp.clip(start[blk_expert][:, None] + rows, 0, n_slot - 1)]
    src = jnp.where(is_slot, slot // TOP_K_IN_GROUP, 0).astype(I32)
    dst = jnp.where(is_slot, slot, 0).astype(I32)
    valid = is_slot.astype(I32)
    tail = ROW_BUFS + 2
    n_rows = n_blocks + 1 + tail
    lead = lambda a: jnp.concatenate([jnp.zeros((1, MOE_BLOCK), I32), a, jnp.zeros((tail, MOE_BLOCK), I32)])
    trash = (n_slot + (jnp.arange(n_rows, dtype=I32)[:, None] + ROW_BUFS - 1) % ROW_BUFS * MOE_BLOCK
             + jnp.arange(MOE_BLOCK, dtype=I32)[None, :])
    dst_rows = jnp.where(lead(valid) > 0, lead(dst), trash)
    plan = jnp.concatenate([lead(src), dst_rows], axis=1)
    return blk_expert, blk_end[-1:].astype(I32), plan[:, None, :]


def _experts(xn_slabs, ids, w_gate, w_up, w_down, layer):
    t = xn_slabs.shape[0] // SLAB
    blk_expert, n_used, plan = _moe_plan(ids, t)
    n_blocks = blk_expert.shape[0]
    w_spec = lambda a: pl.BlockSpec((None, None) + a.shape[2:], lambda i, be, nu: (layer, be[i], 0, 0))
    buf = pltpu.VMEM((ROW_BUFS, MOE_BLOCK * SLAB, LANES), F32)
    return pl.pallas_call(
        _expert_kernel,
        name="moe_experts",
        grid_spec=pltpu.PrefetchScalarGridSpec(
            num_scalar_prefetch=2,
            grid=(n_blocks,),
            in_specs=[pl.BlockSpec(memory_space=pl.ANY), pl.BlockSpec(memory_space=pl.ANY),
                      w_spec(w_gate), w_spec(w_up), w_spec(w_down)],
            out_specs=pl.BlockSpec(memory_space=pl.ANY),
            scratch_shapes=[pltpu.SMEM((PLAN_SLOTS, 1, 2 * MOE_BLOCK), I32), buf, buf,
                            pltpu.SemaphoreType.DMA((PLAN_SLOTS,)), pltpu.SemaphoreType.DMA((ROW_BUFS,)),
                            pltpu.SemaphoreType.DMA((ROW_BUFS,))]),
        out_shape=jax.ShapeDtypeStruct(((t * TOP_K_IN_GROUP + ROW_BUFS * MOE_BLOCK) * SLAB, LANES), F32),
        compiler_params=_params("arbitrary"),
    )(blk_expert, n_used, plan, xn_slabs, w_gate, w_up, w_down)


def _moe_sum(h_ref, y_ref, gate_ref):
    gate = gate_ref[...]
    rows = h_ref.shape[0]
    y0 = _load_slabs(y_ref, rows, TOP_K_IN_GROUP * SLAB)
    y1 = _load_slabs(y_ref, rows, TOP_K_IN_GROUP * SLAB, SLAB)
    return h_ref[...] + gate[:, 0:1] * y0 + gate[:, 1:2] * y1


def _combine_kernel(h_ref, y_ref, gate_ref, o_ref):
    o_ref[...] = _moe_sum(h_ref, y_ref, gate_ref)


def _combine(h, y_slabs, gates):
    t = h.shape[0]
    tm = ROW_TILE
    return pl.pallas_call(
        _combine_kernel,
        name="moe_combine",
        grid=(t // tm,),
        in_specs=[pl.BlockSpec((tm, D_MODEL), lambda i: (i, 0)),
                  pl.BlockSpec((tm * TOP_K_IN_GROUP * SLAB, LANES), lambda i: (i, 0)),
                  pl.BlockSpec((tm, TOP_K_IN_GROUP), lambda i: (i, 0))],
        out_specs=pl.BlockSpec((tm, D_MODEL), lambda i: (i, 0)),
        out_shape=jax.ShapeDtypeStruct((t, D_MODEL), F32),
        compiler_params=_params("arbitrary"),
    )(h, y_slabs, gates.T)


def _moe(routed, w_gate, w_up, w_down, layer):
    xn_slabs, ids, gates = routed
    return _experts(xn_slabs, ids, w_gate, w_up, w_down, layer), gates


def _even_mixer(h, g, w_in, cmp_pe, w_cmp1, w_cmp2, q_norm_g, k_norm_g, ret_norm_g, w_out, router, b, s):
    ret, ret_kt, nq, chunks, ks_aug, vs, kw_aug, vw, gates = _even_inproj(h, g, w_in, q_norm_g, k_norm_g, s)
    y_ret = _retention(ret, ret_kt, ret_norm_g, b, s)
    cmp_kv = _compress(chunks, cmp_pe, w_cmp1, w_cmp2, k_norm_g, b, s)
    y_nsa = _nsa_attention(nq, cmp_kv, ks_aug, vs, kw_aug, vw, gates, b, s)
    return _even_outproj(h, y_ret, y_nsa, w_out, router)


def kernel(x, mix_norm_g, ffn_norm_g, ev_w_in, ev_cmp_pe, ev_w_cmp1, ev_w_cmp2, ev_q_norm_g, ev_k_norm_g, ev_ret_norm_g, ev_w_out, od_w_in, od_conv_w, od_conv_b, od_w_out, moe_wg_group, moe_bg_group, moe_wg_expert, moe_bg_expert, moe_w_gate, moe_w_up, moe_w_down):
    b, s, d = x.shape
    h = x.reshape(b * s, d)
    pending = None
    for layer in range(mix_norm_g.shape[0]):
        i = layer // 2
        router = (ffn_norm_g[layer], moe_wg_group[layer], moe_bg_group[layer], moe_wg_expert[layer], moe_bg_expert[layer])
        if layer % 2 == 0:
            if pending is not None:
                h = _combine(h, *pending)
            h, *routed = _even_mixer(h, mix_norm_g[layer], ev_w_in[i], ev_cmp_pe[i], ev_w_cmp1[i], ev_w_cmp2[i],
                                     ev_q_norm_g[i], ev_k_norm_g[i], ev_ret_norm_g[i], ev_w_out[i], router, b, s)
        else:
            h, *routed = _odd_mixer(h, *pending, mix_norm_g[layer], od_w_in[i], od_conv_w[i], od_conv_b[i],
                                    od_w_out[i], router, b, s)
        pending = _moe(routed, moe_w_gate, moe_w_up, moe_w_down, layer)
    return _combine(h, *pending).reshape(b, s, d)
```

```python
import functools

import jax
import jax.numpy as jnp
import numpy as np
from jax import lax
from jax.experimental import pallas as pl
from jax.experimental.pallas import tpu as pltpu

F32 = jnp.float32
BF16 = jnp.bfloat16
I32 = jnp.int32
HIGHEST = lax.Precision.HIGHEST

D_MODEL = 1024
HEAD_DIM = 64
RET_HEADS = 8
NSA_HEADS = 8
NSA_KV_GROUPS = 2
NSA_GROUP_SIZE = NSA_HEADS // NSA_KV_GROUPS
RET_W = RET_HEADS * HEAD_DIM
NSA_W = NSA_HEADS * HEAD_DIM
NSA_KV_W = NSA_KV_GROUPS * HEAD_DIM
N_BRANCHES = 3
RET_CHUNK = 128
CMP_STRIDE = 16
CMP_BLOCK = 2 * CMP_STRIDE
CMP_HIDDEN = 128
SEL_BLOCK = 64
TOP_N = 8
WINDOW = 256
N_GROUPS = 4
EXPERTS_PER_GROUP = 8
N_EXPERTS = N_GROUPS * EXPERTS_PER_GROUP
TOP_K_IN_GROUP = 2
D_EXPERT = 256
RMS_EPS = 1e-6
NEG_INF = -1e30
ATTN_SCALE = HEAD_DIM ** -0.5

LANES = 128
VMEM_LIMIT = 48 * 1024 * 1024
ROW_TILE = 512
NSA_Q_TILE = 256
NSA_KV_CHUNK = 512
MOE_BLOCK = 256
ROUTER_ROWS = 40
SLAB = D_MODEL // LANES
PLAN_SLOTS = 8
RET_PAIRS_PER_STEP = 4
ROW_BUFS = 4


def _params(*sem):
    return pltpu.CompilerParams(dimension_semantics=sem, vmem_limit_bytes=VMEM_LIMIT)


def _rms(x, g):
    return x * lax.rsqrt(jnp.mean(x * x, axis=-1, keepdims=True) + RMS_EPS) * g


def _group_rms(x, bd, gain):
    sq = x * x
    hi = sq.astype(BF16)
    lo = (sq - hi.astype(F32)).astype(BF16)
    ms = jnp.dot(hi, bd, preferred_element_type=F32) + jnp.dot(lo, bd, preferred_element_type=F32)
    return x * lax.rsqrt(ms + RMS_EPS) * gain


def _sigmoid(x):
    return 1.0 / (1.0 + jnp.exp(-x))


def _block_diag_mean(n):
    idx = np.arange(n) // HEAD_DIM
    return jnp.asarray((idx[:, None] == idx[None, :]).astype(np.float32) / HEAD_DIM, BF16)


def _even_inproj_kernel(x_ref, g_ref, wret_ref, wkt_ref, wnq_ref, wkv_ref, wng_ref, qgain_ref, kgain_ref, bd512_ref,
                        bd128_ref, ret_ref, kt_ref, nq_ref, chunk_ref, ks_ref, vs_ref, kw_ref, vw_ref, gate_ref, kcv_ref, *, seq_tiles):
    tm = x_ref.shape[0]
    xn = _rms(x_ref[...], g_ref[...]).astype(BF16)
    ret_ref[...] = jnp.dot(xn, wret_ref[...], preferred_element_type=F32).astype(BF16)
    kt_ref[...] = lax.dot_general(wkt_ref[...], xn, (((1,), (1,)), ((), ())), preferred_element_type=F32).astype(BF16)
    nq = jnp.dot(xn, wnq_ref[...], preferred_element_type=F32)
    nq_ref[...] = _group_rms(nq, bd512_ref[...], qgain_ref[...]).astype(BF16)
    kv = jnp.dot(xn, wkv_ref[...], preferred_element_type=F32)
    w = NSA_KV_W
    n_chunk = tm // CMP_STRIDE
    for kv_i in range(2):
        kcv_ref[kv_i] = kv[:, kv_i * w:(kv_i + 1) * w]
        for j in range(CMP_STRIDE // 2):
            even = kcv_ref[kv_i, pl.ds(2 * j, n_chunk, stride=CMP_STRIDE), :]
            odd = kcv_ref[kv_i, pl.ds(2 * j + 1, n_chunk, stride=CMP_STRIDE), :]
            for g in range(NSA_KV_GROUPS):
                sl = slice(g * HEAD_DIM, (g + 1) * HEAD_DIM)
                pair = jnp.concatenate([even[:, sl], odd[:, sl]], axis=-1).astype(BF16)
                chunk_ref[kv_i, g, :, j * LANES:(j + 1) * LANES] = pair
    ks = _group_rms(kv[:, 2 * w:3 * w], bd128_ref[...], kgain_ref[0:1, :])
    kw = _group_rms(kv[:, 4 * w:5 * w], bd128_ref[...], kgain_ref[1:2, :])
    vs, vw = kv[:, 3 * w:4 * w], kv[:, 5 * w:6 * w]
    pos = (pl.program_id(0) % seq_tiles) * tm + lax.broadcasted_iota(I32, (tm, HEAD_DIM), 0)
    lane = lax.broadcasted_iota(I32, (tm, HEAD_DIM), 1)
    blk = lax.shift_right_arithmetic(pos, int(np.log2(SEL_BLOCK)))
    feat_w = jnp.where(lane == 0, blk, jnp.where(lane == 1, pos & (SEL_BLOCK - 1), 0)).astype(F32)
    feat_s = feat_w + jnp.where(lane == blk + 2, 1.0, 0.0)
    ones_col = jnp.where(lane == 0, 1.0, 0.0)
    for g in range(NSA_KV_GROUPS):
        sl = slice(g * HEAD_DIM, (g + 1) * HEAD_DIM)
        ks_ref[g] = jnp.concatenate([ks[:, sl], feat_s], axis=-1).astype(BF16)
        kw_ref[g] = jnp.concatenate([kw[:, sl], feat_w], axis=-1).astype(BF16)
        vs_ref[g] = jnp.concatenate([vs[:, sl], ones_col], axis=-1).astype(BF16)
        vw_ref[g] = jnp.concatenate([vw[:, sl], ones_col], axis=-1).astype(BF16)
    ng = jnp.dot(xn, wng_ref[...], preferred_element_type=F32)
    for g in range(NSA_KV_GROUPS):
        gate_ref[g] = ng[:, g * LANES:(g + 1) * LANES]


def _even_inproj(h, g, w_in, q_norm_g, k_norm_g, s):
    t = h.shape[0]
    tm = ROW_TILE
    c_ret, c_nq, c_kv = 4 * RET_W, 4 * RET_W + NSA_W, 4 * RET_W + NSA_W + 6 * NSA_KV_W
    n_gate = NSA_GROUP_SIZE * N_BRANCHES
    flat = CMP_STRIDE * HEAD_DIM
    wb = w_in.astype(BF16)
    wret = jnp.concatenate([wb[:, :RET_W], wb[:, 2 * RET_W:c_ret]], axis=1)
    wkt = wb[:, RET_W:2 * RET_W].T
    wnq, wkv = wb[:, c_ret:c_nq], wb[:, c_nq:c_kv]
    wng = jnp.concatenate([jnp.pad(wb[:, c_kv + gi * n_gate:c_kv + (gi + 1) * n_gate], ((0, 0), (0, LANES - n_gate)))
                           for gi in range(NSA_KV_GROUPS)], axis=1)
    qgain = jnp.tile(q_norm_g, NSA_HEADS)[None, :]
    kgain = jnp.stack([jnp.tile(k_norm_g[1], NSA_KV_GROUPS), jnp.tile(k_norm_g[2], NSA_KV_GROUPS)])
    full = lambda a: pl.BlockSpec(a.shape, lambda i: (0,) * a.ndim)
    row = lambda n: pl.BlockSpec((tm, n), lambda i: (i, 0))
    grp = lambda n: pl.BlockSpec((NSA_KV_GROUPS, tm, n), lambda i: (0, i, 0))
    grp_shape = lambda n, dt: jax.ShapeDtypeStruct((NSA_KV_GROUPS, t, n), dt)
    bd512, bd128 = _block_diag_mean(NSA_W), _block_diag_mean(NSA_KV_W)
    gg = g[None, :]
    return pl.pallas_call(
        functools.partial(_even_inproj_kernel, seq_tiles=s // tm),
        name="even_inproj",
        grid=(t // tm,),
        in_specs=[row(D_MODEL), full(gg), full(wret), full(wkt), full(wnq), full(wkv), full(wng), full(qgain),
                  full(kgain), full(bd512), full(bd128)],
        out_specs=[row(3 * RET_W), pl.BlockSpec((RET_W, tm), lambda i: (0, i)), row(NSA_W),
                   pl.BlockSpec((2, NSA_KV_GROUPS, tm // CMP_STRIDE, flat), lambda i: (0, 0, i, 0)),
                   grp(LANES), grp(LANES), grp(LANES), grp(LANES), grp(LANES)],
        out_shape=[jax.ShapeDtypeStruct((t, 3 * RET_W), BF16), jax.ShapeDtypeStruct((RET_W, t), BF16),
                   jax.ShapeDtypeStruct((t, NSA_W), BF16), jax.ShapeDtypeStruct((2, NSA_KV_GROUPS, t // CMP_STRIDE, flat), BF16), grp_shape(LANES, BF16),
                   grp_shape(LANES, BF16), grp_shape(LANES, BF16), grp_shape(LANES, BF16), grp_shape(LANES, F32)],
        scratch_shapes=[pltpu.VMEM((2, tm, NSA_KV_W), F32)],
        compiler_params=_params("arbitrary"),
    )(h, gg, wret, wkt, wnq, wkv, wng, qgain, kgain, bd512, bd128)


def _retention_kernel(q_ref, kt_ref, v_ref, rg_ref, decay_ref, qd_ref, kd_ref, cd_ref, diag_ref, bd_ref, gain_ref, o_ref):
    n_chunks = q_ref.shape[0] // RET_CHUNK
    n_pairs = q_ref.shape[1] // LANES
    gain, diag, bd = gain_ref[...], diag_ref[...], bd_ref[...]
    first = lax.broadcasted_iota(I32, (RET_CHUNK, LANES), 1) < HEAD_DIM
    zero = jnp.zeros((RET_CHUNK, LANES), BF16)

    def body(c, states):
        r0 = pl.multiple_of(c * RET_CHUNK, RET_CHUNK)
        rows = pl.ds(r0, RET_CHUNK)
        new_states = []
        for j in range(n_pairs):
            lanes = slice(j * LANES, (j + 1) * LANES)
            q2, v2 = q_ref[rows, lanes], v_ref[rows, lanes]
            kt = kt_ref[lanes, rows].astype(F32) * ATTN_SCALE
            kt_b = kt.astype(BF16)
            s0 = jnp.dot(jnp.where(first, q2, zero), kt_b, preferred_element_type=F32) * decay_ref[j, 0]
            s1 = jnp.dot(jnp.where(first, zero, q2), kt_b, preferred_element_type=F32) * decay_ref[j, 1]
            y = jnp.where(first, jnp.dot(s0.astype(BF16), v2, preferred_element_type=F32),
                          jnp.dot(s1.astype(BF16), v2, preferred_element_type=F32))
            y = y + jnp.dot((q2.astype(F32) * qd_ref[j]).astype(BF16), states[j].astype(BF16),
                            preferred_element_type=F32)
            kv = jnp.dot((kt * kd_ref[j]).astype(BF16), v2, preferred_element_type=F32)
            rg = rg_ref[rows, lanes].astype(F32)
            o_ref[rows, lanes] = (_group_rms(y, bd, gain) * (rg * _sigmoid(rg))).astype(BF16)
            new_states.append(cd_ref[j] * states[j] + diag * kv)
        return tuple(new_states)

    lax.fori_loop(0, n_chunks, body, tuple(jnp.zeros((LANES, LANES), F32) for _ in range(n_pairs)), unroll=True)


def _retention_tables():
    h = np.arange(RET_HEADS, dtype=np.float64)
    log_g = np.log(1.0 - 2.0 ** (-5.0 - h))
    pos = np.arange(RET_CHUNK, dtype=np.float64)
    diff = pos[:, None] - pos[None, :]
    decay = np.where(diff >= 0, np.exp(log_g[:, None, None] * np.maximum(diff, 0.0)), 0.0)
    qd = np.exp(log_g[:, None] * (pos + 1.0))
    kd = np.exp(log_g[:, None] * (RET_CHUNK - 1 - pos))
    cd = np.exp(log_g * RET_CHUNK)
    pairs = RET_HEADS // 2
    rep = lambda a: np.repeat(a, HEAD_DIM, axis=1)
    qd2 = rep(qd.reshape(pairs, 2, RET_CHUNK)).transpose(0, 2, 1)
    kd2 = rep(kd.reshape(pairs, 2, RET_CHUNK))
    cd2 = rep(cd.reshape(pairs, 2, 1)) * np.ones((1, 1, LANES))
    half = np.arange(LANES) // HEAD_DIM
    diag = (half[:, None] == half[None, :]).astype(np.float64)
    return [jnp.asarray(a, F32) for a in (decay.reshape(pairs, 2, RET_CHUNK, RET_CHUNK), qd2, kd2, cd2, diag)]


def _retention(ret, ret_kt, ret_norm_g, b, s):
    t = b * s
    n_pairs = RET_HEADS // 2
    decay, qd, kd, cd, diag = _retention_tables()
    bd = _block_diag_mean(LANES)
    per_step = RET_PAIRS_PER_STEP
    n_steps = n_pairs // per_step
    col = lambda off: pl.BlockSpec((s, per_step * LANES), lambda bi, p: (bi, off * n_steps + p))
    tab = lambda a: pl.BlockSpec((per_step,) + a.shape[1:], lambda bi, p: (p,) + (0,) * (a.ndim - 1))
    full = lambda a: pl.BlockSpec(a.shape, lambda bi, p: (0, 0))
    gain = jnp.tile(ret_norm_g, 2)[None, :]
    return pl.pallas_call(
        _retention_kernel,
        name="retention",
        grid=(b, n_steps),
        in_specs=[col(0), pl.BlockSpec((per_step * LANES, s), lambda bi, p: (p, bi)), col(1), col(2),
                  tab(decay), tab(qd), tab(kd), tab(cd), full(diag), full(bd), full(gain)],
        out_specs=pl.BlockSpec((s, per_step * LANES), lambda bi, p: (bi, p)),
        out_shape=jax.ShapeDtypeStruct((t, RET_W), BF16),
        compiler_params=_params("arbitrary", "arbitrary"),
    )(ret, ret_kt, ret, ret, decay, qd, kd, cd, diag, bd, gain)


def _compress_kernel(x_ref, pe_ref, w1_ref, w2_ref, kgain_ref, o_ref):
    is_key = pl.program_id(0) == 0
    half = CMP_STRIDE * HEAD_DIM
    x = x_ref[...]
    n = x.shape[0]
    a = jnp.dot(x, w1_ref[0:half, :], preferred_element_type=F32)
    bm = jnp.dot(x, w1_ref[half:2 * half, :], preferred_element_type=F32)
    pew = jnp.dot(pe_ref[...], w1_ref[...], preferred_element_type=F32)[0:1, :]
    pre = a + pltpu.roll(bm, n - 1, axis=0) + pew
    hid = 0.5 * pre * (1.0 + jnp.tanh(np.sqrt(2.0 / np.pi) * (pre + 0.044715 * pre * pre * pre)))
    out = jnp.dot(hid.astype(BF16), w2_ref[...], preferred_element_type=F32)
    normed = _rms(out, kgain_ref[...])
    o_ref[...] = jnp.where(is_key, normed, out).astype(BF16)


def _compress(chunks, cmp_pe, w_cmp1, w_cmp2, k_norm_g, b, s):
    n = s // CMP_STRIDE
    g = NSA_KV_GROUPS
    flat = CMP_STRIDE * HEAD_DIM
    pe = jnp.broadcast_to(cmp_pe.reshape(2, 1, 2 * flat), (2, 8, 2 * flat)).astype(BF16)
    w1, w2 = w_cmp1.astype(BF16), w_cmp2.astype(BF16)
    kgain = k_norm_g[0][None, :]
    return pl.pallas_call(
        _compress_kernel,
        name="nsa_compress",
        grid=(2, b, g),
        in_specs=[pl.BlockSpec((None, None, n, flat), lambda kv, bi, gi: (kv, gi, bi, 0)),
                  pl.BlockSpec((None, 8, 2 * flat), lambda kv, bi, gi: (kv, 0, 0)),
                  pl.BlockSpec((None, 2 * flat, CMP_HIDDEN), lambda kv, bi, gi: (kv, 0, 0)),
                  pl.BlockSpec((None, CMP_HIDDEN, HEAD_DIM), lambda kv, bi, gi: (kv, 0, 0)),
                  pl.BlockSpec(kgain.shape, lambda kv, bi, gi: (0, 0))],
        out_specs=pl.BlockSpec((None, None, None, n, HEAD_DIM), lambda kv, bi, gi: (kv, bi, gi, 0, 0)),
        out_shape=jax.ShapeDtypeStruct((2, b, g, n, HEAD_DIM), BF16),
        compiler_params=_params("arbitrary", "arbitrary", "arbitrary"),
    )(chunks, pe, w1, w2, kgain)


def _nsa_kernel(slopes_ref, q_ref, kcmp_ref, vcmp_ref, ks_ref, vs_ref, kw_ref, vw_ref, gate_ref, ovl_ref, place_ref,
                o_ref, *, n_sel, n_cmp):
    r_heads = NSA_GROUP_SIZE
    tq = q_ref.shape[0]
    gi = pl.program_id(1)
    t0 = pl.program_id(2) * tq
    nt = (((1,), (1,)), ((), ()))
    tn = (((0,), (0,)), ((), ()))

    q = q_ref[...]
    lane = lax.broadcasted_iota(I32, (tq, HEAD_DIM), 1)
    qs, feats = [], []
    for r in range(r_heads):
        slope = slopes_ref[gi * r_heads + r]
        qs.append((q[:, r * HEAD_DIM:(r + 1) * HEAD_DIM].astype(F32) * ATTN_SCALE).astype(BF16))
        feats.append(jnp.where(lane == 0, slope * SEL_BLOCK, jnp.where(lane == 1, slope, 0.0)))

    def stack_q(extra):
        return jnp.concatenate([jnp.concatenate([qs[r], (feats[r] + extra).astype(BF16)], axis=-1)
                                for r in range(r_heads)], axis=0)

    q4 = stack_q(0.0)
    t_col = t0 + lax.broadcasted_iota(I32, (tq, 1), 0)

    def head_rows(x):
        return [x[r * tq:(r + 1) * tq] for r in range(r_heads)]

    n_pad = kcmp_ref.shape[0]
    c_idx = lax.broadcasted_iota(I32, (tq, n_pad), 1)
    mask_c = (t_col >= c_idx * CMP_STRIDE + (CMP_BLOCK - 1)) & (c_idx < n_cmp)
    s4 = lax.dot_general(q4, kcmp_ref[...], nt, preferred_element_type=F32)
    ps = []
    for s in head_rows(s4):
        s = jnp.where(mask_c, s, NEG_INF)
        p = jnp.where(mask_c, jnp.exp(s - jnp.max(s, axis=-1, keepdims=True)), 0.0)
        ps.append(p * (1.0 / jnp.maximum(jnp.sum(p, axis=-1, keepdims=True), 1e-30)))
    p4 = jnp.concatenate(ps, axis=0).astype(BF16)
    o_cmp = head_rows(jnp.dot(p4, vcmp_ref[...], preferred_element_type=F32))

    n_rows = ovl_ref.shape[0]
    imp4 = lax.dot_general(ovl_ref[...], p4, nt, preferred_element_type=F32)
    imp = imp4[:, 0:tq]
    for r in range(1, r_heads):
        imp = imp + imp4[:, r * tq:(r + 1) * tq]
    j_idx = lax.broadcasted_iota(I32, (n_rows, tq), 0)
    t_row = t0 + lax.broadcasted_iota(I32, (n_rows, tq), 1)
    q_blk = lax.shift_right_arithmetic(t_row, int(np.log2(SEL_BLOCK)))
    valid = (j_idx * SEL_BLOCK <= t_row) & (j_idx < n_sel)
    forced = (j_idx == 0) | (j_idx == q_blk) | (j_idx == q_blk - 1)
    val = jnp.where(j_idx < n_sel, jnp.where(forced, imp + 1e3, jnp.where(valid, imp, -1e3)), -2e3)
    rank = jnp.zeros((n_rows, tq), F32)
    for k in range(n_sel):
        vk = val[k:k + 1, :]
        rank = rank + jnp.where(vk > val, 1.0, jnp.where(vk == val, jnp.where(j_idx > k, 1.0, 0.0), 0.0))
    unselected = jnp.where(valid, jnp.where(rank < TOP_N, 0.0, NEG_INF), NEG_INF).astype(BF16)
    q4_sel = stack_q(lax.dot_general(unselected, place_ref[...], tn, preferred_element_type=F32))

    ck = NSA_KV_CHUNK

    def sel_step(c, carry, causal):
        c0 = pl.multiple_of(c * ck, ck)
        s4 = lax.dot_general(q4_sel, ks_ref[pl.ds(c0, ck), :], nt, preferred_element_type=F32)
        if causal:
            visible = t_col >= c0 + lax.broadcasted_iota(I32, (tq, ck), 1)
        ps, new = [], []
        for r, s in enumerate(head_rows(s4)):
            if causal:
                s = jnp.where(visible, s, NEG_INF)
            m_old, acc_old = carry[r]
            m_new = jnp.maximum(m_old, jnp.max(s, axis=-1, keepdims=True))
            ps.append(jnp.exp(s - m_new))
            new.append((m_new, jnp.exp(m_old - m_new) * acc_old))
        pv = jnp.dot(jnp.concatenate(ps, axis=0).astype(BF16), vs_ref[pl.ds(c0, ck), :], preferred_element_type=F32)
        return tuple((m, acc + o) for (m, acc), o in zip(new, head_rows(pv)))

    def normalise(acc):
        return acc[:, 0:HEAD_DIM] * (1.0 / acc[:, HEAD_DIM:HEAD_DIM + 1])

    init = tuple((jnp.full((tq, 1), NEG_INF, F32), jnp.zeros((tq, LANES), F32)) for _ in range(r_heads))
    c_last = t0 // ck
    carry = lax.fori_loop(0, c_last, lambda c, carry: sel_step(c, carry, False), init)
    o_sel = [normalise(acc) for (_, acc) in sel_step(c_last, carry, True)]

    n_win = WINDOW + tq
    w0 = pl.multiple_of(jnp.maximum(t0 - WINDOW, 0), tq)
    dist_w = t_col - (w0 + lax.broadcasted_iota(I32, (tq, n_win), 1))
    mask_w = (dist_w >= 0) & (dist_w < WINDOW)
    s4 = lax.dot_general(q4, kw_ref[pl.ds(w0, n_win), :], nt, preferred_element_type=F32)
    ps = []
    for s in head_rows(s4):
        s = jnp.where(mask_w, s, NEG_INF)
        ps.append(jnp.exp(s - jnp.max(s, axis=-1, keepdims=True)))
    pv = jnp.dot(jnp.concatenate(ps, axis=0).astype(BF16), vw_ref[pl.ds(w0, n_win), :], preferred_element_type=F32)
    o_win = [normalise(o) for o in head_rows(pv)]

    gate = _sigmoid(gate_ref[...])
    outs = []
    for r in range(r_heads):
        c = N_BRANCHES * r
        outs.append(gate[:, c:c + 1] * o_cmp[r] + gate[:, c + 1:c + 2] * o_sel[r] + gate[:, c + 2:c + 3] * o_win[r])
    o_ref[...] = jnp.concatenate(outs, axis=-1).astype(BF16)


def _nsa_attention(nq, cmp_kv, ks_aug, vs, kw_aug, vw, gates, b, s):
    t = b * s
    g, r = NSA_KV_GROUPS, NSA_GROUP_SIZE
    tq = NSA_Q_TILE
    n_q = s // tq
    n_sel = s // SEL_BLOCK
    n_pad = s // CMP_STRIDE
    n_cmp = n_pad - 1
    n_rows = -(-n_sel // 8) * 8
    assert 2 + n_sel <= HEAD_DIM and s % NSA_KV_CHUNK == 0 and s >= WINDOW + tq and NSA_KV_CHUNK % tq == 0
    slopes = jnp.asarray(2.0 ** (-8.0 * np.arange(1, NSA_HEADS + 1) / NSA_HEADS), F32)
    cmp_start = np.arange(n_pad) * CMP_STRIDE
    cmp_end = cmp_start + CMP_BLOCK - 1
    feat = np.zeros((n_pad, HEAD_DIM), np.float32)
    feat[:, 0], feat[:, 1] = cmp_end // SEL_BLOCK, cmp_end % SEL_BLOCK
    kcmp_aug = jnp.concatenate([cmp_kv[0], jnp.broadcast_to(jnp.asarray(feat, BF16), cmp_kv[0].shape)], axis=-1)
    sel_start = np.arange(n_rows) * SEL_BLOCK
    ovl = ((cmp_start[None, :] < sel_start[:, None] + SEL_BLOCK) & (cmp_start[None, :] + CMP_BLOCK > sel_start[:, None])
           & (np.arange(n_pad)[None, :] < n_cmp) & (np.arange(n_rows)[:, None] < n_sel))
    ovl = jnp.asarray(ovl.astype(np.float32), BF16)
    place = jnp.asarray((np.arange(n_rows)[:, None] + 2 == np.arange(HEAD_DIM)[None, :]).astype(np.float32), BF16)
    seq = lambda w: pl.BlockSpec((None, s, w), lambda bi, gi, qi: (gi, bi, 0))
    cmp = lambda w: pl.BlockSpec((None, None, n_pad, w), lambda bi, gi, qi: (bi, gi, 0, 0))
    return pl.pallas_call(
        functools.partial(_nsa_kernel, n_sel=n_sel, n_cmp=n_cmp),
        name="nsa_attention",
        grid=(b, g, n_q),
        in_specs=[pl.BlockSpec(memory_space=pltpu.SMEM),
                  pl.BlockSpec((tq, r * HEAD_DIM), lambda bi, gi, qi: (bi * n_q + qi, gi)),
                  cmp(2 * HEAD_DIM), cmp(HEAD_DIM), seq(LANES), seq(LANES), seq(LANES), seq(LANES),
                  pl.BlockSpec((None, tq, LANES), lambda bi, gi, qi: (gi, bi * n_q + qi, 0)),
                  pl.BlockSpec(ovl.shape, lambda bi, gi, qi: (0, 0)),
                  pl.BlockSpec(place.shape, lambda bi, gi, qi: (0, 0))],
        out_specs=pl.BlockSpec((tq, r * HEAD_DIM), lambda bi, gi, qi: (bi * n_q + qi, gi)),
        out_shape=jax.ShapeDtypeStruct((t, NSA_W), BF16),
        compiler_params=_params("arbitrary", "arbitrary", "arbitrary"),
    )(slopes, nq, kcmp_aug, cmp_kv[1], ks_aug, vs, kw_aug, vw, gates, ovl, place)


def _even_outproj_kernel(h_ref, a_ref, b_ref, wa_ref, wb_ref, g_ref, wr_ref, br_ref, o_ref, xn_ref, ids_ref, gates_ref):
    y = jnp.dot(a_ref[...], wa_ref[...], preferred_element_type=F32)
    y = y + jnp.dot(b_ref[...], wb_ref[...], preferred_element_type=F32)
    h = h_ref[...] + y
    o_ref[...] = h
    _route(h, g_ref, wr_ref, br_ref, xn_ref, ids_ref, gates_ref)


def _even_outproj(h, y_ret, y_nsa, w_out, router):
    t = h.shape[0]
    tm = ROW_TILE
    wb = w_out.astype(BF16)
    wa, wbt = wb[:RET_W], wb[RET_W:]
    gg, wr, br = _router_operands(*router, tm)
    row = lambda n: pl.BlockSpec((tm, n), lambda i: (i, 0))
    full = lambda a: pl.BlockSpec(a.shape, lambda i: (0, 0))
    r_specs, r_shapes = _router_outputs(t, tm, lambda i: i)
    return pl.pallas_call(
        _even_outproj_kernel,
        name="even_outproj",
        grid=(t // tm,),
        in_specs=[row(D_MODEL), row(RET_W), row(NSA_W), full(wa), full(wbt), full(gg), full(wr), full(br)],
        out_specs=[row(D_MODEL)] + r_specs,
        out_shape=[jax.ShapeDtypeStruct((t, D_MODEL), F32)] + r_shapes,
        compiler_params=_params("arbitrary"),
    )(h, y_ret, y_nsa, wa, wbt, gg, wr, br)


def _odd_mixer_kernel(h_ref, y_ref, mg_ref, g_ref, win_ref, cw_ref, cb_ref, wout_ref, rg_ref, wr_ref, br_ref,
                      o_ref, xn_ref, ids_ref, gates_ref, tail_ref):
    @pl.when(pl.program_id(1) == 0)
    def _():
        tail_ref[...] = jnp.zeros_like(tail_ref)

    ts = h_ref.shape[0]
    h = _moe_sum(h_ref, y_ref, mg_ref)
    xn = _rms(h, g_ref[...]).astype(BF16)
    proj = jnp.dot(xn, win_ref[...], preferred_element_type=F32)
    gate_b, gate_c, hid = proj[:, 0:D_MODEL], proj[:, D_MODEL:2 * D_MODEL], proj[:, 2 * D_MODEL:3 * D_MODEL]
    u = gate_c * hid
    row = lax.broadcasted_iota(I32, (ts, D_MODEL), 0)
    tail = tail_ref[...]
    prev1, prev2 = tail[7:8, :], tail[6:7, :]
    u1 = jnp.where(row >= 1, pltpu.roll(u, 1, axis=0), prev1)
    u2 = jnp.where(row >= 2, pltpu.roll(u, 2, axis=0), jnp.where(row == 1, prev1, prev2))
    cw = cw_ref[...]
    y = cw[0:1, :] * u2 + cw[1:2, :] * u1 + cw[2:3, :] * u + cb_ref[...]
    tail_ref[...] = u[ts - 8:ts, :]
    z = (gate_b * y).astype(BF16)
    h_new = h + jnp.dot(z, wout_ref[...], preferred_element_type=F32)
    o_ref[...] = h_new
    _route(h_new, rg_ref, wr_ref, br_ref, xn_ref, ids_ref, gates_ref)


def _odd_mixer(h, y_slabs, moe_gates, g, w_in, conv_w, conv_b, w_out, router, b, s):
    t = b * s
    ts = ROW_TILE
    n_s = s // ts
    win, wout = w_in.astype(BF16), w_out.astype(BF16)
    gg, cb = g[None, :], conv_b[None, :]
    cw = jnp.pad(conv_w, ((0, 8 - conv_w.shape[0]), (0, 0)))
    rg, wr, br = _router_operands(*router, ts)
    full = lambda a: pl.BlockSpec(a.shape, lambda bi, si: (0, 0), pipeline_mode=pl.Buffered(1))
    row = pl.BlockSpec((ts, D_MODEL), lambda bi, si: (bi * n_s + si, 0))
    y_spec = pl.BlockSpec((ts * TOP_K_IN_GROUP * SLAB, LANES), lambda bi, si: (bi * n_s + si, 0))
    mg_spec = pl.BlockSpec((ts, TOP_K_IN_GROUP), lambda bi, si: (bi * n_s + si, 0))
    r_specs, r_shapes = _router_outputs(t, ts, lambda bi, si: bi * n_s + si)
    return pl.pallas_call(
        _odd_mixer_kernel,
        name="odd_mixer",
        grid=(b, n_s),
        in_specs=[row, y_spec, mg_spec, full(gg), full(win), full(cw), full(cb), full(wout), full(rg), full(wr), full(br)],
        out_specs=[row] + r_specs,
        out_shape=[jax.ShapeDtypeStruct((t, D_MODEL), F32)] + r_shapes,
        scratch_shapes=[pltpu.VMEM((8, D_MODEL), F32)],
        compiler_params=_params("arbitrary", "arbitrary"),
    )(h, y_slabs, moe_gates.T, gg, win, cw, cb, wout, rg, wr, br)


def _store_slabs(ref, x, stride):
    rows = x.shape[0]
    for j in range(SLAB):
        ref[pl.ds(j, rows, stride=stride), :] = x[:, j * LANES:(j + 1) * LANES]


def _load_slabs(ref, rows, stride, offset=0):
    return jnp.concatenate([ref[pl.ds(offset + j, rows, stride=stride), :] for j in range(SLAB)], axis=-1)


def _route(h, g_ref, wr_ref, br_ref, xn_ref, ids_ref, gates_ref):
    xn = _rms(h, g_ref[...])
    _store_slabs(xn_ref, xn, SLAB)
    nt = (((1,), (1,)), ((), ()))
    x_hi = xn.astype(BF16)
    x_lo = (xn - x_hi.astype(F32)).astype(BF16)
    w = wr_ref[...]
    w_hi = w.astype(BF16)
    w_lo = (w - w_hi.astype(F32)).astype(BF16)
    logits = (lax.dot_general(w_hi, x_hi, nt, preferred_element_type=F32)
              + lax.dot_general(w_hi, x_lo, nt, preferred_element_type=F32)
              + lax.dot_general(w_lo, x_hi, nt, preferred_element_type=F32)) + br_ref[...]
    row = lambda i: logits[i:i + 1, :]

    def softmax(xs):
        m = functools.reduce(jnp.maximum, xs)
        es = [jnp.exp(x - m) for x in xs]
        tot = functools.reduce(lambda a, c: a + c, es)
        return [e / tot for e in es]

    def argmax(ps):
        best_p, best_i = ps[0], jnp.zeros_like(ps[0], dtype=I32)
        for i in range(1, len(ps)):
            upd = ps[i] > best_p
            best_p = jnp.where(upd, ps[i], best_p)
            best_i = jnp.where(upd, i, best_i)
        return best_p, best_i

    grp_p, grp_i = argmax(softmax([row(i) for i in range(N_GROUPS)]))
    in_grp = []
    for e in range(EXPERTS_PER_GROUP):
        x = row(N_GROUPS + e)
        for gidx in range(1, N_GROUPS):
            x = jnp.where(grp_i == gidx, row(N_GROUPS + gidx * EXPERTS_PER_GROUP + e), x)
        in_grp.append(x)
    pe = softmax(in_grp)
    p1, i1 = argmax(pe)
    p2, i2 = argmax([jnp.where(i1 == e, -1.0, pe[e]) for e in range(EXPERTS_PER_GROUP)])
    tot = p1 + p2
    ids_ref[...] = jnp.concatenate([grp_i * EXPERTS_PER_GROUP + i1, grp_i * EXPERTS_PER_GROUP + i2], axis=0)
    gates_ref[...] = jnp.concatenate([grp_p * p1 / tot, grp_p * p2 / tot], axis=0)


def _router_operands(g, wg_group, bg_group, wg_expert, bg_expert, tm):
    n_logit = N_GROUPS + N_EXPERTS
    wr = jnp.pad(jnp.concatenate([wg_group, wg_expert], axis=1).T, ((0, ROUTER_ROWS - n_logit), (0, 0)))
    br = jnp.pad(jnp.concatenate([bg_group, bg_expert]), (0, ROUTER_ROWS - n_logit))
    return g[None, :], wr, jnp.broadcast_to(br[:, None], (ROUTER_ROWS, tm))


def _router_outputs(t, tm, row_index):
    specs = [pl.BlockSpec((tm * SLAB, LANES), lambda *i: (row_index(*i), 0)),
             pl.BlockSpec((TOP_K_IN_GROUP, tm), lambda *i: (0, row_index(*i))),
             pl.BlockSpec((TOP_K_IN_GROUP, tm), lambda *i: (0, row_index(*i)))]
    shapes = [jax.ShapeDtypeStruct((t * SLAB, LANES), F32), jax.ShapeDtypeStruct((TOP_K_IN_GROUP, t), I32),
              jax.ShapeDtypeStruct((TOP_K_IN_GROUP, t), F32)]
    return specs, shapes


def _expert_kernel(blk_expert_ref, n_used_ref, plan_hbm, x_hbm, wg_ref, wu_ref, wd_ref, y_hbm,
                   plan, xbuf, ybuf, sem_plan, sem_in, sem_out):
    del blk_expert_ref
    depth = ROW_BUFS
    i = pl.program_id(0)
    n_used = n_used_ref[0]
    cur, prev, ahead = i % depth, (i + depth - 1) % depth, (i + depth - 1) % depth

    def plan_copy(k):
        return pltpu.make_async_copy(plan_hbm.at[k], plan.at[k % PLAN_SLOTS], sem_plan.at[k % PLAN_SLOTS])

    def row_copy(hbm, hbm_row, buf, r, sem, to_hbm):
        h = hbm.at[pl.ds(pl.multiple_of(hbm_row * SLAB, SLAB), SLAB)]
        v = buf.at[pl.ds(r * SLAB, SLAB)]
        return pltpu.make_async_copy(v, h, sem) if to_hbm else pltpu.make_async_copy(h, v, sem)

    def wait_gather(s):
        for r in range(MOE_BLOCK):
            row_copy(x_hbm, 0, xbuf.at[s], r, sem_in.at[s], False).wait()

    def wait_scatter(s):
        for r in range(MOE_BLOCK):
            row_copy(y_hbm, 0, ybuf.at[s], r, sem_out.at[s], True).wait()

    @pl.when(i == 0)
    def _():
        ybuf[...] = jnp.zeros_like(ybuf)
        n_slot = y_hbm.shape[0] // SLAB - depth * MOE_BLOCK
        for s in range(depth - 1):
            for r in range(MOE_BLOCK):
                row_copy(y_hbm, n_slot + s * MOE_BLOCK + r, ybuf.at[s], r, sem_out.at[s], True).start()
        for s in range(depth - 1):
            wait_scatter(s)
        for k in range(depth + 1):
            plan_copy(k).start()
        for k in range(depth + 1):
            plan_copy(k).wait()
        for blk in range(depth - 1):
            for r in range(MOE_BLOCK):
                row_copy(x_hbm, plan[blk + 1, 0, r], xbuf.at[blk], r, sem_in.at[blk], False).start()

    @pl.when(i <= n_used)
    def _():
        @pl.when(i >= 1)
        def _():
            plan_copy(i + depth).wait()

        plan_copy(i + depth + 1).start()
        wait_gather(cur)

        @pl.when(i >= depth - 1)
        def _():
            wait_scatter(cur)

        nxt, prv = (i + depth) % PLAN_SLOTS, i % PLAN_SLOTS
        for r in range(MOE_BLOCK):
            row_copy(x_hbm, plan[nxt, 0, r], xbuf.at[ahead], r, sem_in.at[ahead], False).start(priority=r % 2)
            row_copy(y_hbm, plan[prv, 0, MOE_BLOCK + r], ybuf.at[prev], r, sem_out.at[prev], True).start(priority=(r + 1) % 2)
        x = _load_slabs(xbuf.at[cur], MOE_BLOCK, SLAB).astype(BF16)
        a = jnp.dot(x, wg_ref[...].astype(BF16), preferred_element_type=F32)
        u = jnp.dot(x, wu_ref[...].astype(BF16), preferred_element_type=F32)
        hid = (a * _sigmoid(a) * u).astype(BF16)
        _store_slabs(ybuf.at[cur], jnp.dot(hid, wd_ref[...].astype(BF16), preferred_element_type=F32), SLAB)

        @pl.when(i == n_used)
        def _():
            plan_copy(i + depth + 1).wait()
            for k in range(1, depth):
                wait_gather((i + k) % depth)

                @pl.when(i >= k - 1)
                def _():
                    wait_scatter((i + depth - k) % depth)


def _moe_plan(ids, t):
    n_slot = t * TOP_K_IN_GROUP
    n_blocks = n_slot // MOE_BLOCK + N_EXPERTS
    flat_e = ids.T.reshape(-1)
    order = jnp.argsort(flat_e).astype(I32)
    experts = jnp.arange(N_EXPERTS, dtype=I32)
    counts = jnp.sum(flat_e[:, None] == experts[None, :], axis=0, dtype=I32)
    n_blk = (counts + MOE_BLOCK - 1) // MOE_BLOCK
    blk_end = jnp.cumsum(n_blk)
    start = jnp.cumsum(counts) - counts
    blocks = jnp.arange(n_blocks, dtype=I32)
    blk_expert = jnp.minimum(jnp.sum(blk_end[None, :] <= blocks[:, None], axis=1, dtype=I32), N_EXPERTS - 1)
    row0 = (blocks - (blk_end - n_blk)[blk_expert]) * MOE_BLOCK
    rows = row0[:, None] + jnp.arange(MOE_BLOCK, dtype=I32)[None, :]
    is_slot = (rows >= 0) & (rows < counts[blk_expert][:, None])
    slot = order[jnp.clip(start[blk_expert][:, None] + rows, 0, n_slot - 1)]
    src = jnp.where(is_slot, slot // TOP_K_IN_GROUP, 0).astype(I32)
    dst = jnp.where(is_slot, slot, 0).astype(I32)
    valid = is_slot.astype(I32)
    tail = ROW_BUFS + 2
    n_rows = n_blocks + 1 + tail
    lead = lambda a: jnp.concatenate([jnp.zeros((1, MOE_BLOCK), I32), a, jnp.zeros((tail, MOE_BLOCK), I32)])
    trash = (n_slot + (jnp.arange(n_rows, dtype=I32)[:, None] + ROW_BUFS - 1) % ROW_BUFS * MOE_BLOCK
             + jnp.arange(MOE_BLOCK, dtype=I32)[None, :])
    dst_rows = jnp.where(lead(valid) > 0, lead(dst), trash)
    plan = jnp.concatenate([lead(src), dst_rows], axis=1)
    return blk_expert, blk_end[-1:].astype(I32), plan[:, None, :]


def _experts(xn_slabs, ids, w_gate, w_up, w_down, layer):
    t = xn_slabs.shape[0] // SLAB
    blk_expert, n_used, plan = _moe_plan(ids, t)
    n_blocks = blk_expert.shape[0]
    w_spec = lambda a: pl.BlockSpec((None, None) + a.shape[2:], lambda i, be, nu: (layer, be[i], 0, 0))
    buf = pltpu.VMEM((ROW_BUFS, MOE_BLOCK * SLAB, LANES), F32)
    return pl.pallas_call(
        _expert_kernel,
        name="moe_experts",
        grid_spec=pltpu.PrefetchScalarGridSpec(
            num_scalar_prefetch=2,
            grid=(n_blocks,),
            in_specs=[pl.BlockSpec(memory_space=pl.ANY), pl.BlockSpec(memory_space=pl.ANY),
                      w_spec(w_gate), w_spec(w_up), w_spec(w_down)],
            out_specs=pl.BlockSpec(memory_space=pl.ANY),
            scratch_shapes=[pltpu.SMEM((PLAN_SLOTS, 1, 2 * MOE_BLOCK), I32), buf, buf,
                            pltpu.SemaphoreType.DMA((PLAN_SLOTS,)), pltpu.SemaphoreType.DMA((ROW_BUFS,)),
                            pltpu.SemaphoreType.DMA((ROW_BUFS,))]),
        out_shape=jax.ShapeDtypeStruct(((t * TOP_K_IN_GROUP + ROW_BUFS * MOE_BLOCK) * SLAB, LANES), F32),
        compiler_params=_params("arbitrary"),
    )(blk_expert, n_used, plan, xn_slabs, w_gate, w_up, w_down)


def _moe_sum(h_ref, y_ref, gate_ref):
    gate = gate_ref[...]
    rows = h_ref.shape[0]
    y0 = _load_slabs(y_ref, rows, TOP_K_IN_GROUP * SLAB)
    y1 = _load_slabs(y_ref, rows, TOP_K_IN_GROUP * SLAB, SLAB)
    return h_ref[...] + gate[:, 0:1] * y0 + gate[:, 1:2] * y1


def _combine_kernel(h_ref, y_ref, gate_ref, o_ref):
    o_ref[...] = _moe_sum(h_ref, y_ref, gate_ref)


def _combine(h, y_slabs, gates):
    t = h.shape[0]
    tm = ROW_TILE
    return pl.pallas_call(
        _combine_kernel,
        name="moe_combine",
        grid=(t // tm,),
        in_specs=[pl.BlockSpec((tm, D_MODEL), lambda i: (i, 0)),
                  pl.BlockSpec((tm * TOP_K_IN_GROUP * SLAB, LANES), lambda i: (i, 0)),
                  pl.BlockSpec((tm, TOP_K_IN_GROUP), lambda i: (i, 0))],
        out_specs=pl.BlockSpec((tm, D_MODEL), lambda i: (i, 0)),
        out_shape=jax.ShapeDtypeStruct((t, D_MODEL), F32),
        compiler_params=_params("arbitrary"),
    )(h, y_slabs, gates.T)


def _moe(routed, w_gate, w_up, w_down, layer):
    xn_slabs, ids, gates = routed
    return _experts(xn_slabs, ids, w_gate, w_up, w_down, layer), gates


def _even_mixer(h, g, w_in, cmp_pe, w_cmp1, w_cmp2, q_norm_g, k_norm_g, ret_norm_g, w_out, router, b, s):
    ret, ret_kt, nq, chunks, ks_aug, vs, kw_aug, vw, gates = _even_inproj(h, g, w_in, q_norm_g, k_norm_g, s)
    y_ret = _retention(ret, ret_kt, ret_norm_g, b, s)
    cmp_kv = _compress(chunks, cmp_pe, w_cmp1, w_cmp2, k_norm_g, b, s)
    y_nsa = _nsa_attention(nq, cmp_kv, ks_aug, vs, kw_aug, vw, gates, b, s)
    return _even_outproj(h, y_ret, y_nsa, w_out, router)


def kernel(x, mix_norm_g, ffn_norm_g, ev_w_in, ev_cmp_pe, ev_w_cmp1, ev_w_cmp2, ev_q_norm_g, ev_k_norm_g, ev_ret_norm_g, ev_w_out, od_w_in, od_conv_w, od_conv_b, od_w_out, moe_wg_group, moe_bg_group, moe_wg_expert, moe_bg_expert, moe_w_gate, moe_w_up, moe_w_down):
    b, s, d = x.shape
    h = x.reshape(b * s, d)
    pending = None
    for layer in range(mix_norm_g.shape[0]):
        i = layer // 2
        router = (ffn_norm_g[layer], moe_wg_group[layer], moe_bg_group[layer], moe_wg_expert[layer], moe_bg_expert[layer])
        if layer % 2 == 0:
            if pending is not None:
                h = _combine(h, *pending)
            h, *routed = _even_mixer(h, mix_norm_g[layer], ev_w_in[i], ev_cmp_pe[i], ev_w_cmp1[i], ev_w_cmp2[i],
                                     ev_q_norm_g[i], ev_k_norm_g[i], ev_ret_norm_g[i], ev_w_out[i], router, b, s)
        else:
            h, *routed = _odd_mixer(h, *pending, mix_norm_g[layer], od_w_in[i], od_conv_w[i], od_conv_b[i],
                                    od_w_out[i], router, b, s)
        pending = _moe(routed, moe_w_gate, moe_w_up, moe_w_down, layer)
    return _combine(h, *pending).reshape(b, s, d)
```
